```python
import math
import jax, jax.numpy as jnp
from jax import lax
import numpy as np

D_MODEL = 1024
BATCH = 8
SEQ = 8192
DEPTH = 4

HEAD_DIM = 64
NSA_HEADS = 4
NSA_KV_HEADS = 1
NSA_GROUP = NSA_HEADS // NSA_KV_HEADS
CMP_BLOCK = 32
CMP_STRIDE = 16
SEL_BLOCK = 32
SEL_TOPK = 4
WINDOW = 512
FORCE_SCORE = 1.0e4
S5_WIDTH = 256
S5_GROUP = 16
S5_GROUPS = S5_WIDTH // S5_GROUP
S5_STATE = 64
SB_HEADS = 4
D_FF = 2048
CONV_WIDTH = 3
Q_BLOCK = 128
NORM_EPS = 1e-6
NEG_INF = -1e30
N_BRANCHES = 3

NSA_Q = NSA_HEADS * HEAD_DIM
NSA_KV = NSA_KV_HEADS * HEAD_DIM
SB_WIDTH = SB_HEADS * HEAD_DIM
OFF_NSA_Q = 0
OFF_CMP = OFF_NSA_Q + NSA_Q
OFF_SEL = OFF_CMP + 2 * NSA_KV
OFF_WIN = OFF_SEL + 2 * NSA_KV
OFF_NSA_G = OFF_WIN + 2 * NSA_KV
OFF_S5 = OFF_NSA_G + NSA_HEADS * 3
OFF_SB = OFF_S5 + S5_WIDTH
OFF_MG = OFF_SB + 3 * SB_WIDTH
IN_COLS = OFF_MG + N_BRANCHES * D_MODEL

kernel_name = 'hybrid_nsa_s5_stickbreak_convffn'


def rms_norm(x, gain):
    xf = x.astype(jnp.float32)
    y = xf * lax.rsqrt(jnp.mean(xf * xf, axis=-1, keepdims=True) + NORM_EPS)
    return (y * gain.astype(jnp.float32)).astype(x.dtype)


def alibi_slopes(n):
    return jnp.asarray(2.0 ** (-8.0 * np.arange(1, n + 1) / n), dtype=jnp.float32)


def split_heads(a, n):
    return a.reshape(a.shape[0], a.shape[1], n, HEAD_DIM)


def nsa_mixer(q, kc, vc, ks, vs, kw, vw, gate_logits, q_gain, k_gain, cmp_pe, cmp_wk, cmp_wv):
    B, L = q.shape[0], q.shape[1]
    f32 = jnp.float32
    scale = HEAD_DIM ** -0.5
    q = rms_norm(q, q_gain).reshape(B, L, NSA_KV_HEADS, NSA_GROUP, HEAD_DIM)
    slopes = alibi_slopes(NSA_HEADS).reshape(NSA_KV_HEADS, NSA_GROUP)

    n_cmp = (L - CMP_BLOCK) // CMP_STRIDE + 1
    cmp_start = jnp.arange(n_cmp) * CMP_STRIDE
    cmp_idx = cmp_start[:, None] + jnp.arange(CMP_BLOCK)[None, :]
    pe = cmp_pe[:, None, :]
    k_blocks = kc[:, cmp_idx] + pe
    v_blocks = vc[:, cmp_idx] + pe
    k_cmp = rms_norm(jnp.einsum('bnlhd,hlde->bnhe', k_blocks, cmp_wk), k_gain[0])
    v_cmp = jnp.einsum('bnlhd,hlde->bnhe', v_blocks, cmp_wv)
    cmp_end = cmp_start + CMP_BLOCK - 1

    n_sel = L // SEL_BLOCK
    top_k = min(SEL_TOPK, n_sel)
    ks = rms_norm(ks, k_gain[1])
    ks_blk = ks.reshape(B, n_sel, SEL_BLOCK, NSA_KV_HEADS, HEAD_DIM).transpose(0, 3, 1, 2, 4)
    vs_blk = vs.reshape(B, n_sel, SEL_BLOCK, NSA_KV_HEADS, HEAD_DIM).transpose(0, 3, 1, 2, 4)
    sel_start = jnp.arange(n_sel) * SEL_BLOCK
    sel_ids = jnp.arange(n_sel)
    overlap = jnp.clip(jnp.minimum(cmp_start[:, None] + CMP_BLOCK, sel_start[None, :] + SEL_BLOCK)
                       - jnp.maximum(cmp_start[:, None], sel_start[None, :]), 0).astype(f32) / CMP_BLOCK

    kw = rms_norm(kw, k_gain[2])
    kw_pad = jnp.pad(kw, ((0, 0), (WINDOW, 0), (0, 0), (0, 0)))
    vw_pad = jnp.pad(vw, ((0, 0), (WINDOW, 0), (0, 0), (0, 0)))

    gates = jax.nn.sigmoid(gate_logits.astype(f32)).reshape(B, L, NSA_KV_HEADS, NSA_GROUP, 3)
    bi = jnp.arange(B)[:, None, None, None]
    hi = jnp.arange(NSA_KV_HEADS)[None, :, None, None]

    def block(n):
        t0 = n * Q_BLOCK
        qb = lax.dynamic_slice_in_dim(q, t0, Q_BLOCK, axis=1)
        t = t0 + jnp.arange(Q_BLOCK)
        s = jnp.einsum('bqhgd,bnhd->bhgqn', qb, k_cmp).astype(f32) * scale
        dist = t[:, None] - cmp_end[None, :]
        ok = dist >= 0
        s = jnp.where(ok, s - slopes[:, :, None, None] * dist.astype(f32), NEG_INF)
        p_cmp = jax.nn.softmax(s, axis=-1) * ok
        o_cmp = jnp.einsum('bhgqn,bnhd->bqhgd', p_cmp.astype(v_cmp.dtype), v_cmp)
        imp = jnp.einsum('bhgqn,ns->bhqs', p_cmp, overlap)
        cur = t // SEL_BLOCK
        ok_sel = sel_start[None, :] <= t[:, None]
        forced = (sel_ids[None, :] == 0) | (sel_ids[None, :] == cur[:, None])
        score = jnp.where(ok_sel, jnp.where(forced, FORCE_SCORE, imp), -1.0)
        _, idx = lax.top_k(score, top_k)
        k_sel = ks_blk[bi, hi, idx]
        v_sel = vs_blk[bi, hi, idx]
        pos = idx[..., None] * SEL_BLOCK + jnp.arange(SEL_BLOCK)
        dist = t[None, None, :, None, None] - pos
        ok = (dist >= 0)[:, :, None]
        s = jnp.einsum('bqhgd,bhqkld->bhgqkl', qb, k_sel).astype(f32) * scale
        s = jnp.where(ok, s - slopes[None, :, :, None, None, None] * dist[:, :, None].astype(f32), NEG_INF)
        p = jax.nn.softmax(s, axis=(-2, -1))
        o_sel = jnp.einsum('bhgqkl,bhqkld->bqhgd', p.astype(v_sel.dtype), v_sel)
        kb = lax.dynamic_slice_in_dim(kw_pad, t0, WINDOW + Q_BLOCK, axis=1)
        vb = lax.dynamic_slice_in_dim(vw_pad, t0, WINDOW + Q_BLOCK, axis=1)
        kpos = t0 - WINDOW + jnp.arange(WINDOW + Q_BLOCK)
        dist = t[:, None] - kpos[None, :]
        ok = (dist >= 0) & (dist < WINDOW) & (kpos[None, :] >= 0)
        s = jnp.einsum('bqhgd,bkhd->bhgqk', qb, kb).astype(f32) * scale
        s = jnp.where(ok, s - slopes[:, :, None, None] * dist.astype(f32), NEG_INF)
        p = jax.nn.softmax(s, axis=-1)
        o_win = jnp.einsum('bhgqk,bkhd->bqhgd', p.astype(vb.dtype), vb)
        g = lax.dynamic_slice_in_dim(gates, t0, Q_BLOCK, axis=1)
        o = g[..., 0:1] * o_cmp + g[..., 1:2] * o_sel + g[..., 2:3] * o_win
        return o.astype(q.dtype)

    out = lax.map(block, jnp.arange(L // Q_BLOCK))
    return jnp.moveaxis(out, 0, 1).reshape(B, L, NSA_Q)


def s5_mixer(u, a_re, a_im, log_dt, b_re, b_im, c_re, c_im, d_skip, w_glu):
    B, L = u.shape[0], u.shape[1]
    f32 = jnp.float32
    ug = u.astype(f32).reshape(B, L, S5_GROUPS, S5_GROUP)
    dt = jnp.exp(log_dt.astype(f32))[:, None]
    A = lax.complex(a_re.astype(f32), a_im.astype(f32))
    A_bar = jnp.exp(dt * A)
    B_c = lax.complex(b_re.astype(f32), b_im.astype(f32))
    B_bar = ((A_bar - 1.0) / A)[..., None] * B_c
    C_c = lax.complex(c_re.astype(f32), c_im.astype(f32))
    bu = jnp.einsum('blgc,gpc->blgp', ug.astype(B_bar.dtype), B_bar)

    def combine(e1, e2):
        a1, b1 = e1
        a2, b2 = e2
        return a1 * a2, a2 * b1 + b2

    a_seq = jnp.broadcast_to(A_bar, (1, L, S5_GROUPS, S5_STATE))
    _, states = lax.associative_scan(combine, (a_seq, bu), axis=1)
    y = jnp.real(jnp.einsum('blgp,gcp->blgc', states, C_c)) + d_skip.astype(f32) * ug
    y = jax.nn.gelu(y.reshape(B, L, S5_WIDTH)).astype(u.dtype)
    return y * jax.nn.sigmoid(y @ w_glu)


def stick_breaking_mixer(q, k, v):
    B, L = q.shape[0], q.shape[1]
    f32 = jnp.float32
    q = q * jnp.asarray(HEAD_DIM ** -0.5, q.dtype)
    upper = (jnp.arange(Q_BLOCK)[:, None] >= jnp.arange(Q_BLOCK)[None, :]).astype(f32)
    outs = []
    for n in range(L // Q_BLOCK):
        nk = n + 1
        t0 = n * Q_BLOCK
        qb = q[:, t0:t0 + Q_BLOCK]
        kp = k[:, :nk * Q_BLOCK]
        vp = v[:, :nk * Q_BLOCK]
        t = t0 + jnp.arange(Q_BLOCK)
        causal = jnp.arange(nk * Q_BLOCK)[None, :] < t[:, None]
        z = jnp.einsum('bqhd,bshd->bhqs', qb, kp).astype(f32)
        log_keep = jnp.where(causal, jax.nn.log_sigmoid(-z), 0.0)
        r_in = jnp.einsum('bhqmj,js->bhqms',
                          log_keep.reshape(B, SB_HEADS, Q_BLOCK, nk, Q_BLOCK), upper)
        tot = r_in[..., 0]
        later = lax.cumsum(tot, axis=3, reverse=True) - tot
        r = (r_in + later[..., None]).reshape(B, SB_HEADS, Q_BLOCK, nk * Q_BLOCK)
        weight = jnp.where(causal, jnp.exp(jnp.minimum(z + r, 0.0)), 0.0)
        outs.append(jnp.einsum('bhqs,bshd->bqhd', weight.astype(v.dtype), vp))
    return jnp.concatenate(outs, axis=1).reshape(B, L, SB_WIDTH)


def conv_ffn(h, w_up, conv_w, conv_b, w_down):
    u = h @ w_up
    c = u.shape[-1]
    u = lax.conv_general_dilated(u, conv_w[:, None, :].astype(u.dtype), window_strides=(1,),
                                 padding=[(CONV_WIDTH - 1, 0)], dimension_numbers=('NWC', 'WIO', 'NWC'),
                                 feature_group_count=c) + conv_b
    gate, val = jnp.split(u, 2, axis=-1)
    return (jax.nn.gelu(gate) * val) @ w_down


def setup_inputs(seed: int = 0) -> dict:
    key = jax.random.key(seed)
    ks = jax.random.split(key, 32)
    f32 = jnp.float32

    def nrm(k, shape, scale):
        return jax.random.normal(k, shape, f32) * scale

    hd = HEAD_DIM
    return {
        'x': nrm(ks[0], (BATCH, SEQ, D_MODEL), 1.0),
        'norm_mix': 1.0 + nrm(ks[1], (DEPTH, D_MODEL), 0.05),
        'w_in': nrm(ks[2], (DEPTH, D_MODEL, IN_COLS), D_MODEL ** -0.5),
        'nsa_q_gain': 1.0 + nrm(ks[3], (DEPTH, hd), 0.05),
        'nsa_k_gain': 1.0 + nrm(ks[4], (DEPTH, 3, hd), 0.05),
        'cmp_pe': nrm(ks[5], (DEPTH, CMP_BLOCK, hd), 0.1),
        'cmp_wk': nrm(ks[6], (DEPTH, NSA_KV_HEADS, CMP_BLOCK, hd, hd), (CMP_BLOCK * hd) ** -0.5),
        'cmp_wv': nrm(ks[7], (DEPTH, NSA_KV_HEADS, CMP_BLOCK, hd, hd), (CMP_BLOCK * hd) ** -0.5),
        's5_a_re': -0.5 + nrm(ks[8], (DEPTH, S5_GROUPS, S5_STATE), 0.01),
        's5_a_im': math.pi * jnp.arange(S5_STATE, dtype=f32) + nrm(ks[9], (DEPTH, S5_GROUPS, S5_STATE), 0.01),
        's5_log_dt': jax.random.uniform(ks[10], (DEPTH, S5_GROUPS), f32, math.log(1e-3), math.log(1e-1)),
        's5_b_re': nrm(ks[11], (DEPTH, S5_GROUPS, S5_STATE, S5_GROUP), (2 * S5_GROUP) ** -0.5),
        's5_b_im': nrm(ks[12], (DEPTH, S5_GROUPS, S5_STATE, S5_GROUP), (2 * S5_GROUP) ** -0.5),
        's5_c_re': nrm(ks[13], (DEPTH, S5_GROUPS, S5_GROUP, S5_STATE), S5_STATE ** -0.5),
        's5_c_im': nrm(ks[14], (DEPTH, S5_GROUPS, S5_GROUP, S5_STATE), S5_STATE ** -0.5),
        's5_d': nrm(ks[15], (DEPTH, S5_GROUPS, S5_GROUP), 1.0),
        's5_w_glu': nrm(ks[16], (DEPTH, S5_WIDTH, S5_WIDTH), S5_WIDTH ** -0.5),
        'w_br_nsa': nrm(ks[17], (DEPTH, NSA_Q, D_MODEL), NSA_Q ** -0.5),
        'w_br_s5': nrm(ks[18], (DEPTH, S5_WIDTH, D_MODEL), S5_WIDTH ** -0.5),
        'w_br_sb': nrm(ks[19], (DEPTH, SB_WIDTH, D_MODEL), SB_WIDTH ** -0.5),
        'w_out': nrm(ks[20], (DEPTH, D_MODEL, D_MODEL), D_MODEL ** -0.5),
        'norm_ffn': 1.0 + nrm(ks[21], (DEPTH, D_MODEL), 0.05),
        'w_up': nrm(ks[22], (DEPTH, D_MODEL, 2 * D_FF), D_MODEL ** -0.5),
        'conv_w': nrm(ks[23], (DEPTH, CONV_WIDTH, 2 * D_FF), CONV_WIDTH ** -0.5),
        'conv_b': nrm(ks[24], (DEPTH, 2 * D_FF), 0.02),
        'w_down': nrm(ks[25], (DEPTH, D_FF, D_MODEL), D_FF ** -0.5),
    }


def reference(x, norm_mix, w_in, nsa_q_gain, nsa_k_gain, cmp_pe, cmp_wk, cmp_wv,
              s5_a_re, s5_a_im, s5_log_dt, s5_b_re, s5_b_im, s5_c_re, s5_c_im, s5_d, s5_w_glu,
              w_br_nsa, w_br_s5, w_br_sb, w_out, norm_ffn, w_up, conv_w, conv_b, w_down):
    B, L, _ = x.shape
    for i in range(DEPTH):
        h = rms_norm(x, norm_mix[i])
        proj = h @ w_in[i]
        q_a = split_heads(proj[..., OFF_NSA_Q:OFF_NSA_Q + NSA_Q], NSA_HEADS)
        kc = split_heads(proj[..., OFF_CMP:OFF_CMP + NSA_KV], NSA_KV_HEADS)
        vc = split_heads(proj[..., OFF_CMP + NSA_KV:OFF_SEL], NSA_KV_HEADS)
        ks_ = split_heads(proj[..., OFF_SEL:OFF_SEL + NSA_KV], NSA_KV_HEADS)
        vs_ = split_heads(proj[..., OFF_SEL + NSA_KV:OFF_WIN], NSA_KV_HEADS)
        kw = split_heads(proj[..., OFF_WIN:OFF_WIN + NSA_KV], NSA_KV_HEADS)
        vw = split_heads(proj[..., OFF_WIN + NSA_KV:OFF_NSA_G], NSA_KV_HEADS)
        nsa_gates = proj[..., OFF_NSA_G:OFF_S5]
        u_s5 = proj[..., OFF_S5:OFF_SB]
        q_c = split_heads(proj[..., OFF_SB:OFF_SB + SB_WIDTH], SB_HEADS)
        k_c = split_heads(proj[..., OFF_SB + SB_WIDTH:OFF_SB + 2 * SB_WIDTH], SB_HEADS)
        v_c = split_heads(proj[..., OFF_SB + 2 * SB_WIDTH:OFF_MG], SB_HEADS)
        merge_g = jax.nn.sigmoid(proj[..., OFF_MG:].reshape(B, L, N_BRANCHES, D_MODEL))

        o_a = nsa_mixer(q_a, kc, vc, ks_, vs_, kw, vw, nsa_gates, nsa_q_gain[i], nsa_k_gain[i],
                        cmp_pe[i], cmp_wk[i], cmp_wv[i])
        o_b = s5_mixer(u_s5, s5_a_re[i], s5_a_im[i], s5_log_dt[i], s5_b_re[i], s5_b_im[i],
                       s5_c_re[i], s5_c_im[i], s5_d[i], s5_w_glu[i])
        o_c = stick_breaking_mixer(q_c, k_c, v_c)
        merged = (merge_g[:, :, 0] * (o_a @ w_br_nsa[i])
                  + merge_g[:, :, 1] * (o_b @ w_br_s5[i])
                  + merge_g[:, :, 2] * (o_c @ w_br_sb[i]))
        x = x + merged @ w_out[i]
        x = x + conv_ffn(rms_norm(x, norm_ffn[i]), w_up[i], conv_w[i], conv_b[i], w_down[i])
    return x
```

```python
import functools
import math

import numpy as np
import jax
import jax.numpy as jnp
from jax import lax
from jax.experimental import pallas as pl
from jax.experimental.pallas import tpu as pltpu

HEAD_DIM = 64
NSA_HEADS = 4
CMP_BLOCK = 32
CMP_STRIDE = 16
SEL_BLOCK = 32
SEL_TOPK = 4
WINDOW = 512
FORCE_SCORE = 1.0e4
S5_WIDTH = 256
S5_GROUP = 16
S5_GROUPS = S5_WIDTH // S5_GROUP
S5_STATE = 64
SB_HEADS = 4
NORM_EPS = 1e-6
NEG_INF = -1e30
LANES = 128
VMEM_LIMIT = 56 * 1024 * 1024

F32 = jnp.float32
BF16 = jnp.bfloat16


def _log2(n):
    assert n & (n - 1) == 0
    return n.bit_length() - 1


SEL_SHIFT = _log2(SEL_BLOCK)


def _dot(a, b):
    return jnp.dot(a, b, preferred_element_type=F32)


def _dot_nt(a, b):
    return lax.dot_general(a, b, (((1,), (1,)), ((), ())), preferred_element_type=F32)


def _const_spec(shape):
    nd = len(shape)
    return pl.BlockSpec(shape, lambda *_: (0,) * nd, pipeline_mode=pl.Buffered(1))


def _params(*sem):
    return pltpu.CompilerParams(dimension_semantics=sem, vmem_limit_bytes=VMEM_LIMIT)


def _gelu(x):
    return 0.5 * x * (1.0 + jnp.tanh(math.sqrt(2.0 / math.pi) * (x + 0.044715 * (x * x * x))))


def _sigmoid(x):
    return 1.0 / (1.0 + jnp.exp(-x))


def _split3(x):
    h1 = x.astype(BF16)
    r1 = x - h1.astype(F32)
    h2 = r1.astype(BF16)
    h3 = (r1 - h2.astype(F32)).astype(BF16)
    return h1, h2, h3


C_QN = 0
C_KV = 512
C_G = 896
C_S5 = 1024
C_SBQ = 1280
C_SBKV = 1792
C_MG = 2304
C_END = C_MG + 3072


def _inproj_kernel(x_ref, gain_ref, w_ref, qg_ref, kg_ref,
                   qn_ref, ckv_ref, skv_ref, wkv_ref, g_ref, s5_ref, sbq_ref, sbkv_ref, mg_ref):
    x = x_ref[0]
    h = x * lax.rsqrt(jnp.mean(x * x, axis=-1, keepdims=True) + NORM_EPS) * gain_ref[...]
    hb = h.astype(BF16)
    lane = lax.broadcasted_iota(jnp.int32, (1, LANES), 1)
    lo = lane < HEAD_DIM

    seg = _dot(hb, w_ref[:, C_QN:C_S5])
    scale = HEAD_DIM ** -0.5
    for hd in range(NSA_HEADS):
        q = seg[:, hd * LANES:(hd + 1) * LANES]
        ms = jnp.sum(q * q, axis=-1, keepdims=True) * (1.0 / HEAD_DIM)
        qn = q * lax.rsqrt(ms + NORM_EPS) * qg_ref[...] * scale
        qn_ref[0, :, hd * LANES:(hd + 1) * LANES] = qn.astype(BF16)
    ckv_ref[0] = seg[:, C_KV:C_KV + LANES].astype(BF16)
    for n, ref in ((1, skv_ref), (2, wkv_ref)):
        kv = seg[:, C_KV + n * LANES:C_KV + (n + 1) * LANES]
        ms = jnp.sum(jnp.where(lo, kv * kv, 0.0), axis=-1, keepdims=True) * (1.0 / HEAD_DIM)
        sc = jnp.where(lo, lax.rsqrt(ms + NORM_EPS) * kg_ref[n - 1:n, :], 1.0)
        ref[0] = (kv * sc).astype(BF16)
    g_ref[0] = _sigmoid(seg[:, C_G:C_G + LANES])

    seg = _dot(hb, w_ref[:, C_S5:C_MG])
    s5_ref[0] = seg[:, 0:S5_WIDTH]
    sbq_ref[0] = (seg[:, C_SBQ - C_S5:C_SBKV - C_S5] * scale).astype(BF16)
    sbkv_ref[0] = seg[:, C_SBKV - C_S5:C_MG - C_S5].astype(BF16)

    for c in range(3):
        seg = _dot(hb, w_ref[:, C_MG + c * 1024:C_MG + (c + 1) * 1024])
        mg_ref[0, :, c * 1024:(c + 1) * 1024] = _sigmoid(seg).astype(BF16)


def _inproj(x, gain, w, qg, kg, tl):
    B, L, D = x.shape
    grid = (B, L // tl)
    row = lambda b, l: (b, l, 0)

    def out(n, dt):
        return jax.ShapeDtypeStruct((B, L, n), dt), pl.BlockSpec((1, tl, n), row)

    outs = [out(512, BF16), out(LANES, BF16), out(LANES, BF16), out(LANES, BF16), out(LANES, F32),
            out(S5_WIDTH, F32), out(512, BF16), out(512, BF16), out(3072, BF16)]
    return pl.pallas_call(
        _inproj_kernel,
        grid=grid,
        in_specs=[pl.BlockSpec((1, tl, D), row), _const_spec((1, D)), _const_spec((D, C_END)),
                  _const_spec((1, LANES)), _const_spec((2, LANES))],
        out_specs=[o[1] for o in outs],
        out_shape=[o[0] for o in outs],
        compiler_params=_params("parallel", "parallel"),
        name="inproj",
    )(x, gain, w, qg, kg)


def _cmpkv_kernel(x_ref, pe_ref, wt_ref, wb_ref, kg_ref, o_ref):
    x = x_ref[0].astype(F32)
    a = _dot((x + pe_ref[0:1, :]).astype(BF16), wt_ref[...])
    b = _dot((x + pe_ref[1:2, :]).astype(BF16), wb_ref[...])
    n = x.shape[0]
    kv = a + pltpu.roll(b, n - 1, 0)
    lane = lax.broadcasted_iota(jnp.int32, (1, LANES), 1)
    lo = lane < HEAD_DIM
    ms = jnp.sum(jnp.where(lo, kv * kv, 0.0), axis=-1, keepdims=True) * (1.0 / HEAD_DIM)
    sc = jnp.where(lo, lax.rsqrt(ms + NORM_EPS) * kg_ref[...], 1.0)
    rowi = lax.broadcasted_iota(jnp.int32, (n, 1), 0)
    o_ref[0] = jnp.where(rowi < n - 1, kv * sc, 0.0).astype(BF16)


def _cmpkv(x16, pe2, wt, wb, kg):
    B, n, w = x16.shape
    return pl.pallas_call(
        _cmpkv_kernel,
        grid=(B,),
        in_specs=[pl.BlockSpec((1, n, w), lambda b: (b, 0, 0)), _const_spec((2, w)),
                  _const_spec((w, LANES)), _const_spec((w, LANES)), _const_spec((1, LANES))],
        out_specs=pl.BlockSpec((1, n, LANES), lambda b: (b, 0, 0)),
        out_shape=jax.ShapeDtypeStruct((B, n, LANES), BF16),
        compiler_params=_params("parallel"),
        name="cmpkv",
    )(x16, pe2, wt, wb, kg)


def _slope(hd):
    return 2.0 ** (-8.0 * (hd + 1) / NSA_HEADS)


def _nsa_cmp_kernel(q_ref, kvc_ref, g_ref, ov_ref, ocmp_ref, idx_ref, flag_ref, *, tq, n_cmp):
    t0 = pl.program_id(1) * tq
    kvc = kvc_ref[0]
    nc = kvc.shape[0]
    g = g_ref[0]
    col = lax.broadcasted_iota(jnp.int32, (tq, nc), 1)
    t = t0 + lax.broadcasted_iota(jnp.int32, (tq, nc), 0)
    dist = t - (col * CMP_STRIDE + (CMP_BLOCK - 1))
    ok = (dist >= 0) & (col < n_cmp)
    distf = dist.astype(F32)
    psum = jnp.zeros((tq, nc), F32)
    for hd in range(NSA_HEADS):
        q = q_ref[0, :, hd * LANES:(hd + 1) * LANES]
        s = _dot_nt(q, kvc)
        s = jnp.where(ok, s - _slope(hd) * distf, NEG_INF)
        e = jnp.exp(s - jnp.max(s, axis=-1, keepdims=True))
        p = jnp.where(ok, e * (1.0 / jnp.sum(e, axis=-1, keepdims=True)), 0.0)
        psum = psum + p
        o = _dot(p.astype(BF16), kvc)
        ocmp_ref[0, :, hd * LANES:(hd + 1) * LANES] = o * g[:, 3 * hd:3 * hd + 1]

    ov = ov_ref[...]
    h1, h2, h3 = _split3(psum)
    imp = _dot(h1, ov) + _dot(h2, ov) + _dot(h3, ov)
    ns = imp.shape[1]
    scol = lax.broadcasted_iota(jnp.int32, (tq, ns), 1)
    scolf = scol.astype(F32)
    trow = t0 + lax.broadcasted_iota(jnp.int32, (tq, ns), 0)
    ok_sel = scol * SEL_BLOCK <= trow
    forced = (scol == 0) | (scol == (trow >> SEL_SHIFT))
    score = jnp.where(ok_sel, jnp.where(forced, FORCE_SCORE, imp), -1.0)
    lane = lax.broadcasted_iota(jnp.int32, (tq, LANES), 1)
    idx_out = jnp.zeros((tq, LANES), jnp.int32)
    hit = jnp.zeros((tq, LANES), F32)
    for k in range(SEL_TOPK):
        m = jnp.max(score, axis=-1, keepdims=True)
        ikf = jnp.min(jnp.where(score == m, scolf, 1e9), axis=-1, keepdims=True)
        score = jnp.where(scolf == ikf, -3e38, score)
        ik = ikf.astype(jnp.int32)
        idx_out = jnp.where(lane == k, ik, idx_out)
        hit = jnp.where(lane == (ik >> _log2(tq // SEL_BLOCK)), 1.0, hit)
    idx_ref[0] = idx_out
    flag_ref[0, 0] = jnp.max(hit, axis=0, keepdims=True).astype(jnp.int32)


def _nsa_cmp(qn, kvc, gates, ov, tq, n_cmp):
    B, L, _ = qn.shape
    nq = L // tq
    nc = kvc.shape[1]
    row = lambda b, i: (b, i, 0)
    return pl.pallas_call(
        functools.partial(_nsa_cmp_kernel, tq=tq, n_cmp=n_cmp),
        grid=(B, nq),
        in_specs=[pl.BlockSpec((1, tq, 512), row), pl.BlockSpec((1, nc, LANES), lambda b, i: (b, 0, 0)),
                  pl.BlockSpec((1, tq, LANES), row), _const_spec(ov.shape)],
        out_specs=[pl.BlockSpec((1, tq, 512), row), pl.BlockSpec((1, tq, LANES), row),
                   pl.BlockSpec((1, 1, 1, LANES), lambda b, i: (b, i, 0, 0))],
        out_shape=[jax.ShapeDtypeStruct((B, L, 512), F32), jax.ShapeDtypeStruct((B, L, LANES), jnp.int32),
                   jax.ShapeDtypeStruct((B, nq, 1, LANES), jnp.int32)],
        compiler_params=_params("parallel", "parallel"),
        name="nsa_cmp",
    )(qn, kvc, gates, ov)


def _nsa_sw_kernel(bits_ref, q_ref, skv_ref, wkv_ref, idx_ref, g_ref, ocmp_ref, o_ref,
                   m_sc, l_sc, acc_sc, *, tq, nwords):
    b = pl.program_id(0)
    qi = pl.program_id(1)
    nq = pl.num_programs(1)
    t0 = qi * tq
    rows = NSA_HEADS * tq
    qs = jnp.concatenate([q_ref[0, :, hd * LANES:(hd + 1) * LANES] for hd in range(NSA_HEADS)], axis=0)
    rowi = lax.broadcasted_iota(jnp.int32, (rows, 1), 0)
    tcol = t0 + (rowi & (tq - 1))
    head = rowi >> _log2(tq)
    slope = jnp.where(head == 0, _slope(0), jnp.where(head == 1, _slope(1),
                      jnp.where(head == 2, _slope(2), _slope(3))))
    idx = idx_ref[0]
    idx4 = jnp.concatenate([idx] * NSA_HEADS, axis=0)
    iks = [idx4[:, k:k + 1] for k in range(SEL_TOPK)]

    m_sc[...] = jnp.full((rows, 1), NEG_INF, F32)
    l_sc[...] = jnp.zeros((rows, 1), F32)
    acc_sc[...] = jnp.zeros((rows, LANES), F32)

    def sel_step(j, carry):
        word = bits_ref[(b * nq + qi) * nwords + (j >> 5)]
        needed = (lax.shift_right_logical(word, j & 31) & 1) == 1

        @pl.when(needed)
        def _():
            kv = skv_ref[0, pl.ds(pl.multiple_of(j * tq, tq), tq), :]
            s = _dot_nt(qs, kv)
            kpos = j * tq + lax.broadcasted_iota(jnp.int32, (1, tq), 1)
            blk = kpos >> SEL_SHIFT
            dist = tcol - kpos
            sel = (blk == iks[0]) | (blk == iks[1]) | (blk == iks[2]) | (blk == iks[3])
            msk = sel & (dist >= 0)
            s = jnp.where(msk, s - slope * dist.astype(F32), NEG_INF)
            m_old = m_sc[...]
            m_new = jnp.maximum(m_old, jnp.max(s, axis=-1, keepdims=True))
            p = jnp.where(msk, jnp.exp(s - m_new), 0.0)
            alpha = jnp.exp(m_old - m_new)
            l_sc[...] = alpha * l_sc[...] + jnp.sum(p, axis=-1, keepdims=True)
            acc_sc[...] = alpha * acc_sc[...] + _dot(p.astype(BF16), kv)
            m_sc[...] = m_new
        return carry

    lax.fori_loop(0, qi + 1, sel_step, 0)
    o_sel = acc_sc[...] * (1.0 / l_sc[...])

    nband = WINDOW + tq
    start = pl.multiple_of(jnp.maximum(t0 - WINDOW, 0), tq)
    band = wkv_ref[0, pl.ds(start, nband), :]
    s = _dot_nt(qs, band)
    kpos = start + lax.broadcasted_iota(jnp.int32, (1, nband), 1)
    dist = tcol - kpos
    ok = (dist >= 0) & (dist < WINDOW)
    s = jnp.where(ok, s - slope * dist.astype(F32), NEG_INF)
    e = jnp.exp(s - jnp.max(s, axis=-1, keepdims=True))
    p = e * (1.0 / jnp.sum(e, axis=-1, keepdims=True))
    o_win = _dot(p.astype(BF16), band)

    g = g_ref[0]
    lane = lax.broadcasted_iota(jnp.int32, (1, LANES), 1)
    for hd in range(NSA_HEADS):
        r = slice(hd * tq, (hd + 1) * tq)
        o = (ocmp_ref[0, :, hd * LANES:(hd + 1) * LANES]
             + g[:, 3 * hd + 1:3 * hd + 2] * o_sel[r]
             + g[:, 3 * hd + 2:3 * hd + 3] * o_win[r])
        o_ref[0, :, hd * LANES:(hd + 1) * LANES] = jnp.where(lane >= HEAD_DIM, o, 0.0).astype(BF16)


def _nsa_sw(bits, qn, skv, wkv, idx, gates, ocmp, tq, nwords):
    B, L, _ = qn.shape
    nq = L // tq
    row = lambda b, i, s: (b, i, 0)
    full = lambda b, i, s: (b, 0, 0)
    rows = NSA_HEADS * tq
    return pl.pallas_call(
        functools.partial(_nsa_sw_kernel, tq=tq, nwords=nwords),
        grid_spec=pltpu.PrefetchScalarGridSpec(
            num_scalar_prefetch=1,
            grid=(B, nq),
            in_specs=[pl.BlockSpec((1, tq, 512), row), pl.BlockSpec((1, L, LANES), full),
                      pl.BlockSpec((1, L, LANES), full), pl.BlockSpec((1, tq, LANES), row),
                      pl.BlockSpec((1, tq, LANES), row), pl.BlockSpec((1, tq, 512), row)],
            out_specs=pl.BlockSpec((1, tq, 512), row),
            scratch_shapes=[pltpu.VMEM((rows, 1), F32), pltpu.VMEM((rows, 1), F32),
                            pltpu.VMEM((rows, LANES), F32)],
        ),
        out_shape=jax.ShapeDtypeStruct((B, L, 512), BF16),
        compiler_params=_params("parallel", "arbitrary"),
        name="nsa_sw",
    )(bits, qn, skv, wkv, idx, gates, ocmp)


S5_X = S5_GROUPS * S5_STATE


def _s5_kernel(u_ref, bcat_ref, ccat_ref, a_ref, d_ref, wglu_ref, o_ref, x_sc, st_sc, *, steps, nb, nch):
    @pl.when(pl.program_id(0) == 0)
    def _():
        st_sc[...] = jnp.zeros_like(st_sc)

    u = u_ref[...]
    x_sc[...] = _dot(u.astype(BF16), bcat_ref[...])
    cw = S5_X // nch
    for c in range(nch):
        re = slice(c * cw, (c + 1) * cw)
        im = slice(S5_X + c * cw, S5_X + (c + 1) * cw)
        ar = jnp.broadcast_to(a_ref[0:1, re], (nb, cw))
        ai = jnp.broadcast_to(a_ref[0:1, im], (nb, cw))

        def body(t, carry):
            xr, xi = carry
            r = pl.ds(pl.multiple_of(t * nb, nb), nb)
            nr = ar * xr - ai * xi + x_sc[r, re]
            ni = ar * xi + ai * xr + x_sc[r, im]
            x_sc[r, re] = nr
            x_sc[r, im] = ni
            return nr, ni

        xr, xi = lax.fori_loop(0, steps, body, (st_sc[:, re], st_sc[:, im]), unroll=8)
        st_sc[:, re] = xr
        st_sc[:, im] = xi

    y = _dot(x_sc[...].astype(BF16), ccat_ref[...]) + d_ref[...] * u
    y = _gelu(y)
    o_ref[...] = (y * _sigmoid(_dot(y.astype(BF16), wglu_ref[...]))).astype(BF16)


def _s5(u_tm, bcat, ccat, a_cat, d, wglu, nb, steps):
    n = u_tm.shape[0]
    rows = steps * nb
    return pl.pallas_call(
        functools.partial(_s5_kernel, steps=steps, nb=nb, nch=2),
        grid=(n // rows,),
        in_specs=[pl.BlockSpec((rows, S5_WIDTH), lambda i: (i, 0)), _const_spec(bcat.shape),
                  _const_spec(ccat.shape), _const_spec(a_cat.shape), _const_spec(d.shape),
                  _const_spec(wglu.shape)],
        out_specs=pl.BlockSpec((rows, S5_WIDTH), lambda i: (i, 0)),
        out_shape=jax.ShapeDtypeStruct((n, S5_WIDTH), BF16),
        scratch_shapes=[pltpu.VMEM((rows, 2 * S5_X), F32), pltpu.VMEM((nb, 2 * S5_X), F32)],
        compiler_params=_params("arbitrary"),
        name="s5",
    )(u_tm, bcat, ccat, a_cat, d, wglu)


def _sb_kernel(q_ref, kv_ref, u2_ref, o_ref, acc_sc, c_sc, *, tq, tk):
    i = pl.program_id(2)
    q = q_ref[0]
    u2 = u2_ref[...]
    acc_sc[...] = jnp.zeros_like(acc_sc)
    c_sc[...] = jnp.zeros_like(c_sc)
    nd = tq // tk
    t = i * tq + lax.broadcasted_iota(jnp.int32, (tq, tk), 0)
    lanei = lax.broadcasted_iota(jnp.int32, (tq, tk), 1)

    def step(j, masked):
        kv = kv_ref[0, pl.ds(pl.multiple_of(j * tk, tk), tk), :]
        z = _dot_nt(q, kv)
        lk = -(jnp.maximum(z, 0.0) + jnp.log(1.0 + jnp.exp(-jnp.abs(z))))
        if masked:
            causal = (j * tk + lanei) < t
            lk = jnp.where(causal, lk, 0.0)
        hi = lk.astype(BF16)
        lo = (lk - hi.astype(F32)).astype(BF16)
        rr = _dot(jnp.concatenate([hi, lo], axis=-1), u2)
        c = c_sc[...]
        w = jnp.exp(jnp.minimum(z + rr[:, :tk] + c, 0.0))
        if masked:
            w = jnp.where(causal, w, 0.0)
        acc_sc[...] += _dot(w.astype(BF16), kv)
        c_sc[...] = c + rr[:, tk:]

    for jj in range(nd):
        step(i * nd + (nd - 1 - jj), True)

    def body(jr, carry):
        step(i * nd - 1 - jr, False)
        return carry

    lax.fori_loop(0, i * nd, body, 0)
    lane = lax.broadcasted_iota(jnp.int32, (1, LANES), 1)
    o_ref[0] = jnp.where(lane >= HEAD_DIM, acc_sc[...], 0.0).astype(BF16)


def _sb(sbq, sbkv, u2, tq, tk):
    B, L, _ = sbq.shape
    return pl.pallas_call(
        functools.partial(_sb_kernel, tq=tq, tk=tk),
        grid=(B, SB_HEADS, L // tq),
        in_specs=[pl.BlockSpec((1, tq, LANES), lambda b, h, i: (b, i, h)),
                  pl.BlockSpec((1, L, LANES), lambda b, h, i: (b, 0, h)),
                  _const_spec(u2.shape)],
        out_specs=pl.BlockSpec((1, tq, LANES), lambda b, h, i: (b, i, h)),
        out_shape=jax.ShapeDtypeStruct((B, L, SB_HEADS * LANES), BF16),
        scratch_shapes=[pltpu.VMEM((tq, LANES), F32), pltpu.VMEM((tq, LANES), F32)],
        compiler_params=_params("parallel", "parallel", "arbitrary"),
        name="sb_attn",
    )(sbq, sbkv, u2)


FFN_CHUNK = 512
HALO = 8


def _merge_ffn_kernel(x_ref, oa_ref, ob_ref, oc_ref, mg_ref, wa_ref, wb_ref, wc_ref, wout_ref, gain_ref,
                      wup_ref, cw_ref, cb_ref, wdown_ref, out_ref, ubuf, carry, *, tl, dff):
    @pl.when(pl.program_id(1) == 0)
    def _():
        carry[...] = jnp.zeros_like(carry)

    d = x_ref.shape[2]
    merged = (mg_ref[0, :, 0:d].astype(F32) * _dot(oa_ref[0], wa_ref[...])
              + mg_ref[0, :, d:2 * d].astype(F32) * _dot(ob_ref[0], wb_ref[...])
              + mg_ref[0, :, 2 * d:3 * d].astype(F32) * _dot(oc_ref[0], wc_ref[...]))
    x1 = x_ref[0] + _dot(merged.astype(BF16), wout_ref[...])
    h = x1 * lax.rsqrt(jnp.mean(x1 * x1, axis=-1, keepdims=True) + NORM_EPS) * gain_ref[...]
    hb = h.astype(BF16)

    def conv_cols(off):
        cols = slice(off, off + FFN_CHUNK)
        u = _dot(hb, wup_ref[:, cols])
        ubuf[0:HALO, :] = carry[:, cols]
        ubuf[HALO:HALO + tl, :] = u
        carry[:, cols] = u[tl - HALO:tl, :]
        return (cw_ref[2:3, cols] * u + cw_ref[1:2, cols] * ubuf[HALO - 1:HALO - 1 + tl, :]
                + cw_ref[0:1, cols] * ubuf[HALO - 2:HALO - 2 + tl, :] + cb_ref[:, cols])

    acc = x1
    for j in range(dff // FFN_CHUNK):
        gate = conv_cols(j * FFN_CHUNK)
        val = conv_cols(dff + j * FFN_CHUNK)
        act = (_gelu(gate) * val).astype(BF16)
        acc = acc + _dot(act, wdown_ref[j * FFN_CHUNK:(j + 1) * FFN_CHUNK, :])
    out_ref[0] = acc


def _merge_ffn(x, oa, ob, oc, mg, wa, wb, wc, wout, gain, wup, cw, cb, wdown, tl):
    B, L, D = x.shape
    dff = wdown.shape[0]
    row = lambda b, l: (b, l, 0)
    weights = [wa, wb, wc, wout, gain, wup, cw, cb, wdown]
    return pl.pallas_call(
        functools.partial(_merge_ffn_kernel, tl=tl, dff=dff),
        grid=(B, L // tl),
        in_specs=[pl.BlockSpec((1, tl, D), row), pl.BlockSpec((1, tl, oa.shape[2]), row),
                  pl.BlockSpec((1, tl, ob.shape[2]), row), pl.BlockSpec((1, tl, oc.shape[2]), row),
                  pl.BlockSpec((1, tl, mg.shape[2]), row)] + [_const_spec(w.shape) for w in weights],
        out_specs=pl.BlockSpec((1, tl, D), row),
        out_shape=jax.ShapeDtypeStruct((B, L, D), F32),
        scratch_shapes=[pltpu.VMEM((HALO + tl, FFN_CHUNK), F32), pltpu.VMEM((HALO, 2 * dff), F32)],
        compiler_params=_params("parallel", "arbitrary"),
        name="merge_ffn",
    )(x, oa, ob, oc, mg, *weights)


def _pad_heads(w, n_heads):
    d = w.shape[0]
    w = w.reshape(d, n_heads, HEAD_DIM)
    return jnp.concatenate([w, jnp.zeros_like(w)], axis=-1).reshape(d, n_heads * LANES)


def _pack_w_in(w):
    d = w.shape[0]
    o_g = 640
    o_s5 = o_g + NSA_HEADS * 3
    o_sb = o_s5 + S5_WIDTH
    o_mg = o_sb + 3 * SB_HEADS * HEAD_DIM
    hw = SB_HEADS * HEAD_DIM
    gates = jnp.pad(w[:, o_g:o_s5], ((0, 0), (0, LANES - NSA_HEADS * 3)))
    sbk = w[:, o_sb + hw:o_sb + 2 * hw].reshape(d, SB_HEADS, 1, HEAD_DIM)
    sbv = w[:, o_sb + 2 * hw:o_mg].reshape(d, SB_HEADS, 1, HEAD_DIM)
    sbkv = jnp.concatenate([sbk, sbv], axis=2).reshape(d, SB_HEADS * LANES)
    packed = jnp.concatenate([_pad_heads(w[:, 0:256], NSA_HEADS), w[:, 256:o_g], gates, w[:, o_s5:o_sb],
                              _pad_heads(w[:, o_sb:o_sb + hw], SB_HEADS), sbkv, w[:, o_mg:]], axis=1)
    return packed.astype(BF16)


def _pad_rows(w, n_heads):
    d = w.shape[1]
    w = w.reshape(n_heads, HEAD_DIM, d)
    return jnp.concatenate([jnp.zeros_like(w), w], axis=1).reshape(n_heads * LANES, d).astype(BF16)


def _pad_gain(g, fill):
    return jnp.concatenate([g, jnp.full_like(g, fill)], axis=-1)


def _cmp_weights(wk, wv):
    z = jnp.zeros_like(wk)
    wkv = jnp.concatenate([jnp.concatenate([wk, z], axis=2), jnp.concatenate([z, wv], axis=2)], axis=1)
    half = CMP_BLOCK // 2
    wt = wkv[:half].reshape(half * LANES, LANES)
    wb = wkv[half:].reshape(half * LANES, LANES)
    return wt.astype(BF16), wb.astype(BF16)


def _s5_params(a_re, a_im, log_dt, b_re, b_im, c_re, c_im):
    dt = jnp.exp(log_dt.astype(F32))[:, None]
    A = lax.complex(a_re.astype(F32), a_im.astype(F32))
    A_bar = jnp.exp(dt * A)
    B_bar = ((A_bar - 1.0) / A)[..., None] * lax.complex(b_re.astype(F32), b_im.astype(F32))
    eye = jnp.eye(S5_GROUPS, dtype=F32)

    def bdiag_in(m):
        return jnp.einsum('gpc,gh->gchp', m, eye).reshape(S5_WIDTH, S5_X)

    def bdiag_out(m):
        return jnp.einsum('gcp,gh->gphc', m, eye).reshape(S5_X, S5_WIDTH)

    bcat = jnp.concatenate([bdiag_in(jnp.real(B_bar)), bdiag_in(jnp.imag(B_bar))], axis=1).astype(BF16)
    ccat = jnp.concatenate([bdiag_out(c_re.astype(F32)), -bdiag_out(c_im.astype(F32))], axis=0).astype(BF16)
    a_cat = jnp.concatenate([jnp.real(A_bar).reshape(1, S5_X), jnp.imag(A_bar).reshape(1, S5_X)], axis=1)
    return bcat, ccat, a_cat


def _overlap_matrix(n_rows, n_cmp, n_sel):
    cs = np.arange(n_rows)[:, None] * CMP_STRIDE
    ss = np.arange(n_sel)[None, :] * SEL_BLOCK
    ov = np.clip(np.minimum(cs + CMP_BLOCK, ss + SEL_BLOCK) - np.maximum(cs, ss), 0, None) / CMP_BLOCK
    ov[n_cmp:] = 0.0
    return jnp.asarray(ov, dtype=BF16)


def _sb_suffix_matrix(tk):
    u = (np.arange(tk)[:, None] >= np.arange(tk)[None, :]).astype(np.float32)
    blk = np.concatenate([u, np.ones_like(u)], axis=1)
    return jnp.asarray(np.concatenate([blk, blk], axis=0), dtype=BF16)


def kernel(x, norm_mix, w_in, nsa_q_gain, nsa_k_gain, cmp_pe, cmp_wk, cmp_wv, s5_a_re, s5_a_im, s5_log_dt,
           s5_b_re, s5_b_im, s5_c_re, s5_c_im, s5_d, s5_w_glu, w_br_nsa, w_br_s5, w_br_sb, w_out, norm_ffn,
           w_up, conv_w, conv_b, w_down):
    B, L, D = x.shape
    depth = w_in.shape[0]
    tq = 128
    tl_in = min(512, L)
    tl_ffn = min(256, L)
    sb_tq, sb_tk = min(256, L), 128
    s5_steps = min(128, L)
    assert L % 512 == 0 and L >= WINDOW + tq and D == 1024
    n_cmp = (L - CMP_BLOCK) // CMP_STRIDE + 1
    n_rows = L // CMP_STRIDE
    n_sel = L // SEL_BLOCK
    nq = L // tq
    nwords = (nq + 31) // 32
    ov = _overlap_matrix(n_rows, n_cmp, n_sel)
    u2 = _sb_suffix_matrix(sb_tk)

    for i in range(depth):
        qg = _pad_gain(nsa_q_gain[i][None, :], 0.0)
        kg = _pad_gain(nsa_k_gain[i], 1.0)
        qn, ckv, skv, wkv, gates, u_s5, sbq, sbkv, mg = _inproj(
            x, norm_mix[i][None, :], _pack_w_in(w_in[i]), qg, kg[1:3], tl_in)

        pe = jnp.concatenate([cmp_pe[i], cmp_pe[i]], axis=-1).reshape(2, (CMP_BLOCK // 2) * LANES)
        wt, wb = _cmp_weights(cmp_wk[i, 0], cmp_wv[i, 0])
        kvc = _cmpkv(ckv.reshape(B, n_rows, CMP_STRIDE * LANES), pe, wt, wb, kg[0:1])
        ocmp, idx, flags = _nsa_cmp(qn, kvc, gates, ov, tq, n_cmp)
        fl = jnp.pad(flags[:, :, 0, :nq], ((0, 0), (0, 0), (0, nwords * 32 - nq))).reshape(B, nq, nwords, 32)
        bits = jnp.sum(fl << jnp.arange(32, dtype=jnp.int32), axis=-1, dtype=jnp.int32).reshape(-1)
        o_a = _nsa_sw(bits, qn, skv, wkv, idx, gates, ocmp, tq, nwords)

        bcat, ccat, a_cat = _s5_params(s5_a_re[i], s5_a_im[i], s5_log_dt[i], s5_b_re[i], s5_b_im[i],
                                       s5_c_re[i], s5_c_im[i])
        u_tm = jnp.transpose(u_s5, (1, 0, 2)).reshape(L * B, S5_WIDTH)
        o_b = _s5(u_tm, bcat, ccat, a_cat, s5_d[i].reshape(1, S5_WIDTH), s5_w_glu[i].astype(BF16), B, s5_steps)
        o_b = jnp.transpose(o_b.reshape(L, B, S5_WIDTH), (1, 0, 2))

        o_c = _sb(sbq, sbkv, u2, sb_tq, sb_tk)

        x = _merge_ffn(x, o_a, o_b, o_c, mg,
                       _pad_rows(w_br_nsa[i], NSA_HEADS), w_br_s5[i].astype(BF16),
                       _pad_rows(w_br_sb[i], SB_HEADS), w_out[i].astype(BF16), norm_ffn[i][None, :],
                       w_up[i].astype(BF16), conv_w[i], conv_b[i][None, :], w_down[i].astype(BF16), tl_ffn)
    return x
```

```python
import functools
import math

import numpy as np
import jax
import jax.numpy as jnp
from jax import lax
from jax.experimental import pallas as pl
from jax.experimental.pallas import tpu as pltpu

HEAD_DIM = 64
NSA_HEADS = 4
CMP_BLOCK = 32
CMP_STRIDE = 16
SEL_BLOCK = 32
SEL_TOPK = 4
WINDOW = 512
FORCE_SCORE = 1.0e4
S5_WIDTH = 256
S5_GROUP = 16
S5_GROUPS = S5_WIDTH // S5_GROUP
S5_STATE = 64
SB_HEADS = 4
NORM_EPS = 1e-6
NEG_INF = -1e30
LANES = 128
VMEM_LIMIT = 56 * 1024 * 1024

F32 = jnp.float32
BF16 = jnp.bfloat16


def _log2(n):
    assert n & (n - 1) == 0
    return n.bit_length() - 1


SEL_SHIFT = _log2(SEL_BLOCK)
LOG2E = math.log2(math.e)


def _dot(a, b):
    return jnp.dot(a, b, preferred_element_type=F32)


def _dot_nt(a, b):
    return lax.dot_general(a, b, (((1,), (1,)), ((), ())), preferred_element_type=F32)


def _const_spec(shape):
    nd = len(shape)
    return pl.BlockSpec(shape, lambda *_: (0,) * nd, pipeline_mode=pl.Buffered(1))


def _params(*sem):
    return pltpu.CompilerParams(dimension_semantics=sem, vmem_limit_bytes=VMEM_LIMIT)


def _gelu(x):
    return 0.5 * x * (1.0 + jnp.tanh(math.sqrt(2.0 / math.pi) * (x + 0.044715 * (x * x * x))))


def _sigmoid(x):
    return 1.0 / (1.0 + jnp.exp(-x))


def _split3(x):
    h1 = x.astype(BF16)
    r1 = x - h1.astype(F32)
    h2 = r1.astype(BF16)
    h3 = (r1 - h2.astype(F32)).astype(BF16)
    return h1, h2, h3


C_QN = 0
C_KV = 512
C_G = 896
C_S5 = 1024
C_SBQ = 1280
C_SBKV = 1792
C_MG = 2304
C_END = C_MG + 3072


def _inproj_kernel(x_ref, gain_ref, w_ref, qg_ref, kg_ref,
                   qn_ref, ckv_ref, skv_ref, wkv_ref, g_ref, s5_ref, sbq_ref, sbkv_ref, mg_ref):
    x = x_ref[0]
    h = x * lax.rsqrt(jnp.mean(x * x, axis=-1, keepdims=True) + NORM_EPS) * gain_ref[...]
    hb = h.astype(BF16)
    lane = lax.broadcasted_iota(jnp.int32, (1, LANES), 1)
    lo = lane < HEAD_DIM

    seg = _dot(hb, w_ref[:, C_QN:C_S5])
    scale = HEAD_DIM ** -0.5
    for hd in range(NSA_HEADS):
        q = seg[:, hd * LANES:(hd + 1) * LANES]
        ms = jnp.sum(q * q, axis=-1, keepdims=True) * (1.0 / HEAD_DIM)
        qn = q * lax.rsqrt(ms + NORM_EPS) * qg_ref[...] * scale
        qn_ref[0, :, hd * LANES:(hd + 1) * LANES] = qn.astype(BF16)
    ckv_ref[0] = seg[:, C_KV:C_KV + LANES].astype(BF16)
    for n, ref in ((1, skv_ref), (2, wkv_ref)):
        kv = seg[:, C_KV + n * LANES:C_KV + (n + 1) * LANES]
        ms = jnp.sum(jnp.where(lo, kv * kv, 0.0), axis=-1, keepdims=True) * (1.0 / HEAD_DIM)
        sc = jnp.where(lo, lax.rsqrt(ms + NORM_EPS) * kg_ref[n - 1:n, :], 1.0)
        ref[0] = (kv * sc).astype(BF16)
    g_ref[0] = _sigmoid(seg[:, C_G:C_G + LANES])

    seg = _dot(hb, w_ref[:, C_S5:C_MG])
    s5_ref[0] = seg[:, 0:S5_WIDTH]
    sbq_ref[0] = (seg[:, C_SBQ - C_S5:C_SBKV - C_S5] * (scale * LOG2E)).astype(BF16)
    sbkv_ref[0] = seg[:, C_SBKV - C_S5:C_MG - C_S5].astype(BF16)

    for c in range(3):
        seg = _dot(hb, w_ref[:, C_MG + c * 1024:C_MG + (c + 1) * 1024])
        mg_ref[0, :, c * 1024:(c + 1) * 1024] = _sigmoid(seg).astype(BF16)


def _inproj(x, gain, w, qg, kg, tl):
    B, L, D = x.shape
    grid = (B, L // tl)
    row = lambda b, l: (b, l, 0)

    def out(n, dt):
        return jax.ShapeDtypeStruct((B, L, n), dt), pl.BlockSpec((1, tl, n), row)

    outs = [out(512, BF16), out(LANES, BF16), out(LANES, BF16), out(LANES, BF16), out(LANES, F32),
            out(S5_WIDTH, F32), out(512, BF16), out(512, BF16), out(3072, BF16)]
    return pl.pallas_call(
        _inproj_kernel,
        grid=grid,
        in_specs=[pl.BlockSpec((1, tl, D), row), _const_spec((1, D)), _const_spec((D, C_END)),
                  _const_spec((1, LANES)), _const_spec((2, LANES))],
        out_specs=[o[1] for o in outs],
        out_shape=[o[0] for o in outs],
        compiler_params=_params("parallel", "parallel"),
        name="inproj",
    )(x, gain, w, qg, kg)


def _cmpkv_kernel(x_ref, pe_ref, wt_ref, wb_ref, kg_ref, o_ref):
    x = x_ref[0].astype(F32)
    a = _dot((x + pe_ref[0:1, :]).astype(BF16), wt_ref[...])
    b = _dot((x + pe_ref[1:2, :]).astype(BF16), wb_ref[...])
    n = x.shape[0]
    kv = a + pltpu.roll(b, n - 1, 0)
    lane = lax.broadcasted_iota(jnp.int32, (1, LANES), 1)
    lo = lane < HEAD_DIM
    ms = jnp.sum(jnp.where(lo, kv * kv, 0.0), axis=-1, keepdims=True) * (1.0 / HEAD_DIM)
    sc = jnp.where(lo, lax.rsqrt(ms + NORM_EPS) * kg_ref[...], 1.0)
    rowi = lax.broadcasted_iota(jnp.int32, (n, 1), 0)
    o_ref[0] = jnp.where(rowi < n - 1, kv * sc, 0.0).astype(BF16)


def _cmpkv(x16, pe2, wt, wb, kg):
    B, n, w = x16.shape
    return pl.pallas_call(
        _cmpkv_kernel,
        grid=(B,),
        in_specs=[pl.BlockSpec((1, n, w), lambda b: (b, 0, 0)), _const_spec((2, w)),
                  _const_spec((w, LANES)), _const_spec((w, LANES)), _const_spec((1, LANES))],
        out_specs=pl.BlockSpec((1, n, LANES), lambda b: (b, 0, 0)),
        out_shape=jax.ShapeDtypeStruct((B, n, LANES), BF16),
        compiler_params=_params("parallel"),
        name="cmpkv",
    )(x16, pe2, wt, wb, kg)


def _slope(hd):
    return 2.0 ** (-8.0 * (hd + 1) / NSA_HEADS)


def _nsa_cmp_kernel(q_ref, kvc_ref, g_ref, ov_ref, ocmp_ref, idx_ref, flag_ref, *, tq, tsel, n_cmp):
    t0 = pl.program_id(1) * tq
    kvc = kvc_ref[0]
    nc = kvc.shape[0]
    g = g_ref[0]
    col = lax.broadcasted_iota(jnp.int32, (tq, nc), 1)
    t = t0 + lax.broadcasted_iota(jnp.int32, (tq, nc), 0)
    dist = t - (col * CMP_STRIDE + (CMP_BLOCK - 1))
    ok = (dist >= 0) & (col < n_cmp)
    distf = dist.astype(F32)
    psum = jnp.zeros((tq, nc), F32)
    for hd in range(NSA_HEADS):
        q = q_ref[0, :, hd * LANES:(hd + 1) * LANES]
        s = _dot_nt(q, kvc)
        s = jnp.where(ok, s - _slope(hd) * distf, NEG_INF)
        e = jnp.exp(s - jnp.max(s, axis=-1, keepdims=True))
        p = jnp.where(ok, e * (1.0 / jnp.sum(e, axis=-1, keepdims=True)), 0.0)
        psum = psum + p
        o = _dot(p.astype(BF16), kvc)
        ocmp_ref[0, :, hd * LANES:(hd + 1) * LANES] = o * g[:, 3 * hd:3 * hd + 1]

    ov = ov_ref[...]
    h1, h2, h3 = _split3(psum)
    imp = _dot(h1, ov) + _dot(h2, ov) + _dot(h3, ov)
    ns = imp.shape[1]
    scol = lax.broadcasted_iota(jnp.int32, (tq, ns), 1)
    scolf = scol.astype(F32)
    trow = t0 + lax.broadcasted_iota(jnp.int32, (tq, ns), 0)
    free = (scol * SEL_BLOCK <= trow) & (scol != 0) & (scol != (trow >> SEL_SHIFT))
    score = jnp.where(free, imp, -1.0)
    lane = lax.broadcasted_iota(jnp.int32, (tq, LANES), 1)
    idx_out = jnp.zeros((tq, LANES), jnp.int32)
    hit = jnp.zeros((tq, LANES), F32)
    for k in range(SEL_TOPK - 2):
        m = jnp.max(score, axis=-1, keepdims=True)
        ikf = jnp.min(jnp.where(score == m, scolf, 1e9), axis=-1, keepdims=True)
        score = jnp.where(scolf == ikf, -3e38, score)
        ik = ikf.astype(jnp.int32)
        idx_out = jnp.where(lane == k, ik, idx_out)
        hit = jnp.where(lane == (ik >> _log2(tsel // SEL_BLOCK)), 1.0, hit)
    idx_ref[0] = idx_out
    for sub in range(tq // tsel):
        flag_ref[0, sub] = jnp.max(hit[sub * tsel:(sub + 1) * tsel], axis=0, keepdims=True).astype(jnp.int32)


def _nsa_cmp(qn, kvc, gates, ov, tq, tsel, n_cmp):
    B, L, _ = qn.shape
    nq = L // tq
    nsub = tq // tsel
    nc = kvc.shape[1]
    row = lambda b, i: (b, i, 0)
    return pl.pallas_call(
        functools.partial(_nsa_cmp_kernel, tq=tq, tsel=tsel, n_cmp=n_cmp),
        grid=(B, nq),
        in_specs=[pl.BlockSpec((1, tq, 512), row), pl.BlockSpec((1, nc, LANES), lambda b, i: (b, 0, 0)),
                  pl.BlockSpec((1, tq, LANES), row), _const_spec(ov.shape)],
        out_specs=[pl.BlockSpec((1, tq, 512), row), pl.BlockSpec((1, tq, LANES), row),
                   pl.BlockSpec((1, nsub, 1, LANES), lambda b, i: (b, i, 0, 0))],
        out_shape=[jax.ShapeDtypeStruct((B, L, 512), F32), jax.ShapeDtypeStruct((B, L, LANES), jnp.int32),
                   jax.ShapeDtypeStruct((B, L // tsel, 1, LANES), jnp.int32)],
        compiler_params=_params("parallel", "parallel"),
        name="nsa_cmp",
    )(qn, kvc, gates, ov)


def _nsa_sw_kernel(bits_ref, q_ref, skv_ref, wkv_ref, idx_ref, g_ref, ocmp_ref, o_ref,
                   m_sc, l_sc, acc_sc, *, tq, nwords):
    b = pl.program_id(0)
    qi = pl.program_id(1)
    nq = pl.num_programs(1)
    t0 = qi * tq
    rows = NSA_HEADS * tq
    qs = jnp.concatenate([q_ref[0, :, hd * LANES:(hd + 1) * LANES] for hd in range(NSA_HEADS)], axis=0)
    rowi = lax.broadcasted_iota(jnp.int32, (rows, LANES), 0)
    ti = t0 + (rowi & (tq - 1))
    tf = ti.astype(F32)
    cur_blk = ti >> SEL_SHIFT
    head = rowi >> _log2(tq)
    slope = jnp.where(head == 0, _slope(0), jnp.where(head == 1, _slope(1),
                      jnp.where(head == 2, _slope(2), _slope(3))))
    idx4 = jnp.concatenate([idx_ref[0]] * NSA_HEADS, axis=0)
    picks = [jnp.broadcast_to(idx4[:, k:k + 1], (rows, LANES)) for k in range(SEL_TOPK - 2)]
    lane_k = lax.broadcasted_iota(jnp.int32, (1, tq), 1)

    def scores(j, kv, forced_blocks, valid):
        kpos = j * tq + lane_k
        blk = kpos >> SEL_SHIFT
        distf = tf - kpos.astype(F32)
        msk = (blk == picks[0]) | (blk == picks[1])
        if forced_blocks:
            min_dist = jnp.where(valid, 0.0, 3e38)
            msk = (msk | (blk == 0) | (blk == cur_blk)) & (distf >= min_dist)
        s = jnp.where(msk, _dot_nt(qs, kv) - slope * distf, NEG_INF)
        return s, msk

    jprev = jnp.maximum(qi - 1, 0)
    tiles = [(0, skv_ref[0, 0:tq, :], qi >= 0), (jprev, skv_ref[0, pl.ds(pl.multiple_of(jprev * tq, tq), tq), :], qi >= 2),
             (qi, skv_ref[0, pl.ds(pl.multiple_of(qi * tq, tq), tq), :], qi >= 1)]
    sm = [scores(j, kv, True, valid) for j, kv, valid in tiles]
    m0 = jnp.maximum(jnp.maximum(jnp.max(sm[0][0], axis=-1, keepdims=True),
                                 jnp.max(sm[1][0], axis=-1, keepdims=True)),
                     jnp.max(sm[2][0], axis=-1, keepdims=True))
    l0 = jnp.zeros((rows, 1), F32)
    acc0 = jnp.zeros((rows, LANES), F32)
    for (s, msk), (_, kv, _) in zip(sm, tiles):
        p = jnp.where(msk, jnp.exp(s - m0), 0.0)
        l0 = l0 + jnp.sum(p, axis=-1, keepdims=True)
        acc0 = acc0 + _dot(p.astype(BF16), kv)
    m_sc[...] = jnp.broadcast_to(m0, (rows, LANES))
    l_sc[...] = jnp.broadcast_to(l0, (rows, LANES))
    acc_sc[...] = acc0

    def sel_step(j, carry):
        word = bits_ref[(b * nq + qi) * nwords + (j >> 5)]
        needed = (lax.shift_right_logical(word, j & 31) & 1) == 1

        @pl.when(needed)
        def _():
            kv = skv_ref[0, pl.ds(pl.multiple_of(j * tq, tq), tq), :]
            s, msk = scores(j, kv, False, None)
            m_old = m_sc[...]
            m_new = jnp.maximum(m_old, jnp.max(s, axis=-1, keepdims=True))
            p = jnp.where(msk, jnp.exp(s - m_new), 0.0)
            alpha = jnp.exp(m_old - m_new)
            l_sc[...] = alpha * l_sc[...] + jnp.sum(p, axis=-1, keepdims=True)
            acc_sc[...] = alpha * acc_sc[...] + _dot(p.astype(BF16), kv)
            m_sc[...] = m_new
        return carry

    lax.fori_loop(1, jnp.maximum(qi - 1, 1), sel_step, 0)
    o_sel = acc_sc[...] / l_sc[...]

    nband = WINDOW + tq
    start = pl.multiple_of(jnp.maximum(t0 - WINDOW, 0), tq)
    band = wkv_ref[0, pl.ds(start, nband), :]
    kposf = (start + lax.broadcasted_iota(jnp.int32, (1, nband), 1)).astype(F32)
    tw = jnp.concatenate([tf] * (nband // LANES), axis=1)
    sw = jnp.concatenate([slope] * (nband // LANES), axis=1)
    distf = tw - kposf
    ok = (distf >= 0.0) & (distf < float(WINDOW))
    s = jnp.where(ok, _dot_nt(qs, band) - sw * distf, NEG_INF)
    e = jnp.exp(s - jnp.max(s, axis=-1, keepdims=True))
    p = e * (1.0 / jnp.sum(e, axis=-1, keepdims=True))
    o_win = _dot(p.astype(BF16), band)

    g = g_ref[0]
    lane = lax.broadcasted_iota(jnp.int32, (1, LANES), 1)
    for hd in range(NSA_HEADS):
        r = slice(hd * tq, (hd + 1) * tq)
        o = (ocmp_ref[0, :, hd * LANES:(hd + 1) * LANES]
             + g[:, 3 * hd + 1:3 * hd + 2] * o_sel[r]
             + g[:, 3 * hd + 2:3 * hd + 3] * o_win[r])
        o_ref[0, :, hd * LANES:(hd + 1) * LANES] = jnp.where(lane >= HEAD_DIM, o, 0.0).astype(BF16)


def _nsa_sw(bits, qn, skv, wkv, idx, gates, ocmp, tq, nwords):
    B, L, _ = qn.shape
    nq = L // tq
    row = lambda b, i, s: (b, i, 0)
    full = lambda b, i, s: (b, 0, 0)
    rows = NSA_HEADS * tq
    return pl.pallas_call(
        functools.partial(_nsa_sw_kernel, tq=tq, nwords=nwords),
        grid_spec=pltpu.PrefetchScalarGridSpec(
            num_scalar_prefetch=1,
            grid=(B, nq),
            in_specs=[pl.BlockSpec((1, tq, 512), row), pl.BlockSpec((1, L, LANES), full),
                      pl.BlockSpec((1, L, LANES), full), pl.BlockSpec((1, tq, LANES), row),
                      pl.BlockSpec((1, tq, LANES), row), pl.BlockSpec((1, tq, 512), row)],
            out_specs=pl.BlockSpec((1, tq, 512), row),
            scratch_shapes=[pltpu.VMEM((rows, LANES), F32), pltpu.VMEM((rows, LANES), F32),
                            pltpu.VMEM((rows, LANES), F32)],
        ),
        out_shape=jax.ShapeDtypeStruct((B, L, 512), BF16),
        compiler_params=_params("parallel", "arbitrary"),
        name="nsa_sw",
    )(bits, qn, skv, wkv, idx, gates, ocmp)


S5_X = S5_GROUPS * S5_STATE


def _s5_kernel(u_ref, bcat_ref, ccat_ref, a_ref, d_ref, wglu_ref, o_ref, x_sc, st_sc, *, steps, nb, nch):
    @pl.when(pl.program_id(0) == 0)
    def _():
        st_sc[...] = jnp.zeros_like(st_sc)

    u = u_ref[...]
    x_sc[...] = _dot(u.astype(BF16), bcat_ref[...])
    cw = S5_X // nch
    for c in range(nch):
        re = slice(c * cw, (c + 1) * cw)
        im = slice(S5_X + c * cw, S5_X + (c + 1) * cw)
        ar = jnp.broadcast_to(a_ref[0:1, re], (nb, cw))
        ai = jnp.broadcast_to(a_ref[0:1, im], (nb, cw))

        def body(t, carry):
            xr, xi = carry
            r = pl.ds(pl.multiple_of(t * nb, nb), nb)
            nr = ar * xr - ai * xi + x_sc[r, re]
            ni = ar * xi + ai * xr + x_sc[r, im]
            x_sc[r, re] = nr
            x_sc[r, im] = ni
            return nr, ni

        xr, xi = lax.fori_loop(0, steps, body, (st_sc[:, re], st_sc[:, im]), unroll=8)
        st_sc[:, re] = xr
        st_sc[:, im] = xi

    y = _dot(x_sc[...].astype(BF16), ccat_ref[...]) + d_ref[...] * u
    y = _gelu(y)
    o_ref[...] = (y * _sigmoid(_dot(y.astype(BF16), wglu_ref[...]))).astype(BF16)


def _s5(u_tm, bcat, ccat, a_cat, d, wglu, nb, steps):
    n = u_tm.shape[0]
    rows = steps * nb
    return pl.pallas_call(
        functools.partial(_s5_kernel, steps=steps, nb=nb, nch=2),
        grid=(n // rows,),
        in_specs=[pl.BlockSpec((rows, S5_WIDTH), lambda i: (i, 0)), _const_spec(bcat.shape),
                  _const_spec(ccat.shape), _const_spec(a_cat.shape), _const_spec(d.shape),
                  _const_spec(wglu.shape)],
        out_specs=pl.BlockSpec((rows, S5_WIDTH), lambda i: (i, 0)),
        out_shape=jax.ShapeDtypeStruct((n, S5_WIDTH), BF16),
        scratch_shapes=[pltpu.VMEM((rows, 2 * S5_X), F32), pltpu.VMEM((nb, 2 * S5_X), F32)],
        compiler_params=_params("arbitrary"),
        name="s5",
    )(u_tm, bcat, ccat, a_cat, d, wglu)


SIGN_BIT = np.int32(-2 ** 31)


def _sb_kernel(q_ref, kv_ref, u2_ref, o_ref, acc_sc, c_sc, z_sc, w_sc, *, tq, tk):
    i = pl.program_id(2)
    q = q_ref[0]
    u2 = u2_ref[...]
    acc_sc[...] = jnp.zeros_like(acc_sc)
    c_sc[...] = jnp.zeros_like(c_sc)
    nsub = tk // LANES
    jd = (i * tq) // tk
    t = i * tq + lax.broadcasted_iota(jnp.int32, (tq, tk), 0)
    lanei = lax.broadcasted_iota(jnp.int32, (tq, tk), 1)

    def kv_block(j):
        return kv_ref[0, pl.ds(pl.multiple_of(j * tk, tk), tk), :]

    def step(j, first):
        z = z_sc[...]
        z_next = _dot_nt(q, kv_block(jnp.maximum(j - 1, 0)))
        if not first:
            acc_sc[...] += _dot(w_sc[...], kv_block(j + 1))
        neg_abs = lax.bitcast_convert_type(lax.bitcast_convert_type(z, jnp.int32) | SIGN_BIT, F32)
        sp = jnp.maximum(z, 0.0) + jnp.log2(1.0 + jnp.exp2(neg_abs))
        if first:
            causal = (j * tk + lanei) < t
            sp = jnp.where(causal, sp, 0.0)
        hi = sp.astype(BF16)
        lo = (sp - hi.astype(F32)).astype(BF16)
        c = c_sc[...]
        ws = [None] * nsub
        for s in reversed(range(nsub)):
            cols = slice(s * LANES, (s + 1) * LANES)
            rr = _dot(jnp.concatenate([hi[:, cols], lo[:, cols]], axis=-1), u2)
            ws[s] = jnp.exp2(jnp.minimum(z[:, cols] + rr[:, :LANES] + c, 0.0))
            c = c + rr[:, LANES:]
        w = jnp.concatenate(ws, axis=-1)
        if first:
            w = jnp.where(causal, w, 0.0)
        w_sc[...] = w.astype(BF16)
        z_sc[...] = z_next
        c_sc[...] = c

    z_sc[...] = _dot_nt(q, kv_block(jd))
    step(jd, True)

    def body(jr, carry):
        step(jd - 1 - jr, False)
        return carry

    lax.fori_loop(0, jd, body, 0)
    acc = acc_sc[...] + _dot(w_sc[...], kv_block(0))
    lane = lax.broadcasted_iota(jnp.int32, (1, LANES), 1)
    o_ref[0] = jnp.where(lane >= HEAD_DIM, acc, 0.0).astype(BF16)


def _sb(sbq, sbkv, u2, tq, tk):
    B, L, _ = sbq.shape
    assert tk % tq == 0 and L % tk == 0
    return pl.pallas_call(
        functools.partial(_sb_kernel, tq=tq, tk=tk),
        grid=(B, SB_HEADS, L // tq),
        in_specs=[pl.BlockSpec((1, tq, LANES), lambda b, h, i: (b, i, h)),
                  pl.BlockSpec((1, L, LANES), lambda b, h, i: (b, 0, h)),
                  _const_spec(u2.shape)],
        out_specs=pl.BlockSpec((1, tq, LANES), lambda b, h, i: (b, i, h)),
        out_shape=jax.ShapeDtypeStruct((B, L, SB_HEADS * LANES), BF16),
        scratch_shapes=[pltpu.VMEM((tq, LANES), F32), pltpu.VMEM((tq, LANES), F32),
                        pltpu.VMEM((tq, tk), F32), pltpu.VMEM((tq, tk), BF16)],
        compiler_params=_params("parallel", "parallel", "arbitrary"),
        name="sb_attn",
    )(sbq, sbkv, u2)


FFN_CHUNK = 512
HALO = 8


def _merge_ffn_kernel(x_ref, oa_ref, ob_ref, oc_ref, mg_ref, wa_ref, wb_ref, wc_ref, wout_ref, gain_ref,
                      wup_ref, cw_ref, cb_ref, wdown_ref, out_ref, ubuf, carry, *, tl, dff):
    @pl.when(pl.program_id(1) == 0)
    def _():
        carry[...] = jnp.zeros_like(carry)

    d = x_ref.shape[2]
    merged = (mg_ref[0, :, 0:d].astype(F32) * _dot(oa_ref[0], wa_ref[...])
              + mg_ref[0, :, d:2 * d].astype(F32) * _dot(ob_ref[0], wb_ref[...])
              + mg_ref[0, :, 2 * d:3 * d].astype(F32) * _dot(oc_ref[0], wc_ref[...]))
    x1 = x_ref[0] + _dot(merged.astype(BF16), wout_ref[...])
    h = x1 * lax.rsqrt(jnp.mean(x1 * x1, axis=-1, keepdims=True) + NORM_EPS) * gain_ref[...]
    hb = h.astype(BF16)

    def conv_cols(off):
        cols = slice(off, off + FFN_CHUNK)
        u = _dot(hb, wup_ref[:, cols])
        ubuf[0:HALO, :] = carry[:, cols]
        ubuf[HALO:HALO + tl, :] = u
        carry[:, cols] = u[tl - HALO:tl, :]
        return (cw_ref[2:3, cols] * u + cw_ref[1:2, cols] * ubuf[HALO - 1:HALO - 1 + tl, :]
                + cw_ref[0:1, cols] * ubuf[HALO - 2:HALO - 2 + tl, :] + cb_ref[:, cols])

    acc = x1
    for j in range(dff // FFN_CHUNK):
        gate = conv_cols(j * FFN_CHUNK)
        val = conv_cols(dff + j * FFN_CHUNK)
        act = (_gelu(gate) * val).astype(BF16)
        acc = acc + _dot(act, wdown_ref[j * FFN_CHUNK:(j + 1) * FFN_CHUNK, :])
    out_ref[0] = acc


def _merge_ffn(x, oa, ob, oc, mg, wa, wb, wc, wout, gain, wup, cw, cb, wdown, tl):
    B, L, D = x.shape
    dff = wdown.shape[0]
    row = lambda b, l: (b, l, 0)
    weights = [wa, wb, wc, wout, gain, wup, cw, cb, wdown]
    return pl.pallas_call(
        functools.partial(_merge_ffn_kernel, tl=tl, dff=dff),
        grid=(B, L // tl),
        in_specs=[pl.BlockSpec((1, tl, D), row), pl.BlockSpec((1, tl, oa.shape[2]), row),
                  pl.BlockSpec((1, tl, ob.shape[2]), row), pl.BlockSpec((1, tl, oc.shape[2]), row),
                  pl.BlockSpec((1, tl, mg.shape[2]), row)] + [_const_spec(w.shape) for w in weights],
        out_specs=pl.BlockSpec((1, tl, D), row),
        out_shape=jax.ShapeDtypeStruct((B, L, D), F32),
        scratch_shapes=[pltpu.VMEM((HALO + tl, FFN_CHUNK), F32), pltpu.VMEM((HALO, 2 * dff), F32)],
        compiler_params=_params("parallel", "arbitrary"),
        name="merge_ffn",
    )(x, oa, ob, oc, mg, *weights)


def _pad_heads(w, n_heads):
    d = w.shape[0]
    w = w.reshape(d, n_heads, HEAD_DIM)
    return jnp.concatenate([w, jnp.zeros_like(w)], axis=-1).reshape(d, n_heads * LANES)


def _pack_w_in(w):
    d = w.shape[0]
    o_g = 640
    o_s5 = o_g + NSA_HEADS * 3
    o_sb = o_s5 + S5_WIDTH
    o_mg = o_sb + 3 * SB_HEADS * HEAD_DIM
    hw = SB_HEADS * HEAD_DIM
    gates = jnp.pad(w[:, o_g:o_s5], ((0, 0), (0, LANES - NSA_HEADS * 3)))
    sbk = w[:, o_sb + hw:o_sb + 2 * hw].reshape(d, SB_HEADS, 1, HEAD_DIM)
    sbv = w[:, o_sb + 2 * hw:o_mg].reshape(d, SB_HEADS, 1, HEAD_DIM)
    sbkv = jnp.concatenate([sbk, sbv], axis=2).reshape(d, SB_HEADS * LANES)
    packed = jnp.concatenate([_pad_heads(w[:, 0:256], NSA_HEADS), w[:, 256:o_g], gates, w[:, o_s5:o_sb],
                              _pad_heads(w[:, o_sb:o_sb + hw], SB_HEADS), sbkv, w[:, o_mg:]], axis=1)
    return packed.astype(BF16)


def _pad_rows(w, n_heads):
    d = w.shape[1]
    w = w.reshape(n_heads, HEAD_DIM, d)
    return jnp.concatenate([jnp.zeros_like(w), w], axis=1).reshape(n_heads * LANES, d).astype(BF16)


def _pad_gain(g, fill):
    return jnp.concatenate([g, jnp.full_like(g, fill)], axis=-1)


def _cmp_weights(wk, wv):
    z = jnp.zeros_like(wk)
    wkv = jnp.concatenate([jnp.concatenate([wk, z], axis=2), jnp.concatenate([z, wv], axis=2)], axis=1)
    half = CMP_BLOCK // 2
    wt = wkv[:half].reshape(half * LANES, LANES)
    wb = wkv[half:].reshape(half * LANES, LANES)
    return wt.astype(BF16), wb.astype(BF16)


def _s5_params(a_re, a_im, log_dt, b_re, b_im, c_re, c_im):
    dt = jnp.exp(log_dt.astype(F32))[:, None]
    A = lax.complex(a_re.astype(F32), a_im.astype(F32))
    A_bar = jnp.exp(dt * A)
    B_bar = ((A_bar - 1.0) / A)[..., None] * lax.complex(b_re.astype(F32), b_im.astype(F32))
    eye = jnp.eye(S5_GROUPS, dtype=F32)

    def bdiag_in(m):
        return jnp.einsum('gpc,gh->gchp', m, eye).reshape(S5_WIDTH, S5_X)

    def bdiag_out(m):
        return jnp.einsum('gcp,gh->gphc', m, eye).reshape(S5_X, S5_WIDTH)

    bcat = jnp.concatenate([bdiag_in(jnp.real(B_bar)), bdiag_in(jnp.imag(B_bar))], axis=1).astype(BF16)
    ccat = jnp.concatenate([bdiag_out(c_re.astype(F32)), -bdiag_out(c_im.astype(F32))], axis=0).astype(BF16)
    a_cat = jnp.concatenate([jnp.real(A_bar).reshape(1, S5_X), jnp.imag(A_bar).reshape(1, S5_X)], axis=1)
    return bcat, ccat, a_cat


def _overlap_matrix(n_rows, n_cmp, n_sel):
    cs = np.arange(n_rows)[:, None] * CMP_STRIDE
    ss = np.arange(n_sel)[None, :] * SEL_BLOCK
    ov = np.clip(np.minimum(cs + CMP_BLOCK, ss + SEL_BLOCK) - np.maximum(cs, ss), 0, None) / CMP_BLOCK
    ov[n_cmp:] = 0.0
    return jnp.asarray(ov, dtype=BF16)


def _sb_suffix_matrix(tk):
    u = (np.arange(tk)[:, None] >= np.arange(tk)[None, :]).astype(np.float32)
    blk = np.concatenate([u, np.ones_like(u)], axis=1)
    return jnp.asarray(-np.concatenate([blk, blk], axis=0), dtype=BF16)


def kernel(x, norm_mix, w_in, nsa_q_gain, nsa_k_gain, cmp_pe, cmp_wk, cmp_wv, s5_a_re, s5_a_im, s5_log_dt,
           s5_b_re, s5_b_im, s5_c_re, s5_c_im, s5_d, s5_w_glu, w_br_nsa, w_br_s5, w_br_sb, w_out, norm_ffn,
           w_up, conv_w, conv_b, w_down):
    B, L, D = x.shape
    depth = w_in.shape[0]
    tq = 128
    tq_cmp = 256
    tl_in = min(512, L)
    tl_ffn = min(256, L)
    sb_tq, sb_tk = 256, 512
    s5_steps = min(128, L)
    assert L % 512 == 0 and L >= WINDOW + tq and D == 1024
    n_cmp = (L - CMP_BLOCK) // CMP_STRIDE + 1
    n_rows = L // CMP_STRIDE
    n_sel = L // SEL_BLOCK
    nq = L // tq
    nwords = (nq + 31) // 32
    ov = _overlap_matrix(n_rows, n_cmp, n_sel)
    u2 = _sb_suffix_matrix(LANES)

    for i in range(depth):
        qg = _pad_gain(nsa_q_gain[i][None, :], 0.0)
        kg = _pad_gain(nsa_k_gain[i], 1.0)
        qn, ckv, skv, wkv, gates, u_s5, sbq, sbkv, mg = _inproj(
            x, norm_mix[i][None, :], _pack_w_in(w_in[i]), qg, kg[1:3], tl_in)

        pe = jnp.concatenate([cmp_pe[i], cmp_pe[i]], axis=-1).reshape(2, (CMP_BLOCK // 2) * LANES)
        wt, wb = _cmp_weights(cmp_wk[i, 0], cmp_wv[i, 0])
        kvc = _cmpkv(ckv.reshape(B, n_rows, CMP_STRIDE * LANES), pe, wt, wb, kg[0:1])
        ocmp, idx, flags = _nsa_cmp(qn, kvc, gates, ov, tq_cmp, tq, n_cmp)
        fl = jnp.pad(flags[:, :, 0, :nq], ((0, 0), (0, 0), (0, nwords * 32 - nq))).reshape(B, nq, nwords, 32)
        bits = jnp.sum(fl << jnp.arange(32, dtype=jnp.int32), axis=-1, dtype=jnp.int32).reshape(-1)
        o_a = _nsa_sw(bits, qn, skv, wkv, idx, gates, ocmp, tq, nwords)

        bcat, ccat, a_cat = _s5_params(s5_a_re[i], s5_a_im[i], s5_log_dt[i], s5_b_re[i], s5_b_im[i],
                                       s5_c_re[i], s5_c_im[i])
        u_tm = jnp.transpose(u_s5, (1, 0, 2)).reshape(L * B, S5_WIDTH)
        o_b = _s5(u_tm, bcat, ccat, a_cat, s5_d[i].reshape(1, S5_WIDTH), s5_w_glu[i].astype(BF16), B, s5_steps)
        o_b = jnp.transpose(o_b.reshape(L, B, S5_WIDTH), (1, 0, 2))

        o_c = _sb(sbq, sbkv, u2, sb_tq, sb_tk)

        x = _merge_ffn(x, o_a, o_b, o_c, mg,
                       _pad_rows(w_br_nsa[i], NSA_HEADS), w_br_s5[i].astype(BF16),
                       _pad_rows(w_br_sb[i], SB_HEADS), w_out[i].astype(BF16), norm_ffn[i][None, :],
                       w_up[i].astype(BF16), conv_w[i], conv_b[i][None, :], w_down[i].astype(BF16), tl_ffn)
    return x
```

```python
import functools
import math

import numpy as np
import jax
import jax.numpy as jnp
from jax import lax
from jax.experimental import pallas as pl
from jax.experimental.pallas import tpu as pltpu

HEAD_DIM = 64
NSA_HEADS = 4
CMP_BLOCK = 32
CMP_STRIDE = 16
SEL_BLOCK = 32
SEL_TOPK = 4
WINDOW = 512
FORCE_SCORE = 1.0e4
S5_WIDTH = 256
S5_GROUP = 16
S5_GROUPS = S5_WIDTH // S5_GROUP
S5_STATE = 64
SB_HEADS = 4
NORM_EPS = 1e-6
NEG_INF = -1e30
LANES = 128
VMEM_LIMIT = 56 * 1024 * 1024

F32 = jnp.float32
BF16 = jnp.bfloat16


def _log2(n):
    assert n & (n - 1) == 0
    return n.bit_length() - 1


SEL_SHIFT = _log2(SEL_BLOCK)
LOG2E = math.log2(math.e)


def _dot(a, b):
    return jnp.dot(a, b, preferred_element_type=F32)


def _dot_nt(a, b):
    return lax.dot_general(a, b, (((1,), (1,)), ((), ())), preferred_element_type=F32)


def _const_spec(shape):
    nd = len(shape)
    return pl.BlockSpec(shape, lambda *_: (0,) * nd, pipeline_mode=pl.Buffered(1))


def _params(*sem):
    return pltpu.CompilerParams(dimension_semantics=sem, vmem_limit_bytes=VMEM_LIMIT)


def _gelu(x):
    return 0.5 * x * (1.0 + jnp.tanh(math.sqrt(2.0 / math.pi) * (x + 0.044715 * (x * x * x))))


def _sigmoid(x):
    return 1.0 / (1.0 + jnp.exp(-x))


def _split3(x):
    h1 = x.astype(BF16)
    r1 = x - h1.astype(F32)
    h2 = r1.astype(BF16)
    h3 = (r1 - h2.astype(F32)).astype(BF16)
    return h1, h2, h3


C_QN = 0
C_KV = 512
C_G = 896
C_S5 = 1024
C_SBQ = 1280
C_SBKV = 1792
C_MG = 2304
C_END = C_MG + 3072


def _inproj_kernel(x_ref, gain_ref, w_ref, qg_ref, kg_ref,
                   qn_ref, ckv_ref, skv_ref, wkv_ref, g_ref, s5_ref, sbq_ref, sbkv_ref, mg_ref):
    x = x_ref[0]
    h = x * lax.rsqrt(jnp.mean(x * x, axis=-1, keepdims=True) + NORM_EPS) * gain_ref[...]
    hb = h.astype(BF16)
    lane = lax.broadcasted_iota(jnp.int32, (1, LANES), 1)
    lo = lane < HEAD_DIM

    seg = _dot(hb, w_ref[:, C_QN:C_S5])
    scale = HEAD_DIM ** -0.5
    for hd in range(NSA_HEADS):
        q = seg[:, hd * LANES:(hd + 1) * LANES]
        ms = jnp.sum(q * q, axis=-1, keepdims=True) * (1.0 / HEAD_DIM)
        qn = q * lax.rsqrt(ms + NORM_EPS) * qg_ref[...] * scale
        qn_ref[0, :, hd * LANES:(hd + 1) * LANES] = qn.astype(BF16)
    ckv_ref[0] = seg[:, C_KV:C_KV + LANES].astype(BF16)
    for n, ref in ((1, skv_ref), (2, wkv_ref)):
        kv = seg[:, C_KV + n * LANES:C_KV + (n + 1) * LANES]
        ms = jnp.sum(jnp.where(lo, kv * kv, 0.0), axis=-1, keepdims=True) * (1.0 / HEAD_DIM)
        sc = jnp.where(lo, lax.rsqrt(ms + NORM_EPS) * kg_ref[n - 1:n, :], 1.0)
        ref[0] = (kv * sc).astype(BF16)
    g_ref[0] = _sigmoid(seg[:, C_G:C_G + LANES])

    seg = _dot(hb, w_ref[:, C_S5:C_MG])
    s5_ref[0] = seg[:, 0:S5_WIDTH]
    sbq_ref[0] = (seg[:, C_SBQ - C_S5:C_SBKV - C_S5] * (scale * LOG2E)).astype(BF16)
    sbkv_ref[0] = seg[:, C_SBKV - C_S5:C_MG - C_S5].astype(BF16)

    for c in range(3):
        seg = _dot(hb, w_ref[:, C_MG + c * 1024:C_MG + (c + 1) * 1024])
        mg_ref[0, :, c * 1024:(c + 1) * 1024] = _sigmoid(seg).astype(BF16)


def _inproj(x, gain, w, qg, kg, tl):
    B, L, D = x.shape
    grid = (B, L // tl)
    row = lambda b, l: (b, l, 0)

    def out(n, dt):
        return jax.ShapeDtypeStruct((B, L, n), dt), pl.BlockSpec((1, tl, n), row)

    outs = [out(512, BF16), out(LANES, BF16), out(LANES, BF16), out(LANES, BF16), out(LANES, F32),
            out(S5_WIDTH, F32), out(512, BF16), out(512, BF16), out(3072, BF16)]
    return pl.pallas_call(
        _inproj_kernel,
        grid=grid,
        in_specs=[pl.BlockSpec((1, tl, D), row), _const_spec((1, D)), _const_spec((D, C_END)),
                  _const_spec((1, LANES)), _const_spec((2, LANES))],
        out_specs=[o[1] for o in outs],
        out_shape=[o[0] for o in outs],
        compiler_params=_params("parallel", "parallel"),
        name="inproj",
    )(x, gain, w, qg, kg)


def _cmpkv_kernel(x_ref, pe_ref, wt_ref, wb_ref, kg_ref, o_ref):
    x = x_ref[0].astype(F32)
    a = _dot((x + pe_ref[0:1, :]).astype(BF16), wt_ref[...])
    b = _dot((x + pe_ref[1:2, :]).astype(BF16), wb_ref[...])
    n = x.shape[0]
    kv = a + pltpu.roll(b, n - 1, 0)
    lane = lax.broadcasted_iota(jnp.int32, (1, LANES), 1)
    lo = lane < HEAD_DIM
    ms = jnp.sum(jnp.where(lo, kv * kv, 0.0), axis=-1, keepdims=True) * (1.0 / HEAD_DIM)
    sc = jnp.where(lo, lax.rsqrt(ms + NORM_EPS) * kg_ref[...], 1.0)
    rowi = lax.broadcasted_iota(jnp.int32, (n, 1), 0)
    o_ref[0] = jnp.where(rowi < n - 1, kv * sc, 0.0).astype(BF16)


def _cmpkv(x16, pe2, wt, wb, kg):
    B, n, w = x16.shape
    return pl.pallas_call(
        _cmpkv_kernel,
        grid=(B,),
        in_specs=[pl.BlockSpec((1, n, w), lambda b: (b, 0, 0)), _const_spec((2, w)),
                  _const_spec((w, LANES)), _const_spec((w, LANES)), _const_spec((1, LANES))],
        out_specs=pl.BlockSpec((1, n, LANES), lambda b: (b, 0, 0)),
        out_shape=jax.ShapeDtypeStruct((B, n, LANES), BF16),
        compiler_params=_params("parallel"),
        name="cmpkv",
    )(x16, pe2, wt, wb, kg)


def _slope(hd):
    return 2.0 ** (-8.0 * (hd + 1) / NSA_HEADS)


def _nsa_cmp_kernel(q_ref, kvc_ref, g_ref, ov_ref, ocmp_ref, idx_ref, flag_ref, *, tq, tsel, n_cmp):
    t0 = pl.program_id(1) * tq
    kvc = kvc_ref[0]
    nc = kvc.shape[0]
    g = g_ref[0]
    col = lax.broadcasted_iota(jnp.int32, (tq, nc), 1)
    t = t0 + lax.broadcasted_iota(jnp.int32, (tq, nc), 0)
    dist = t - (col * CMP_STRIDE + (CMP_BLOCK - 1))
    ok = (dist >= 0) & (col < n_cmp)
    distf = dist.astype(F32)
    psum = jnp.zeros((tq, nc), F32)
    for hd in range(NSA_HEADS):
        q = q_ref[0, :, hd * LANES:(hd + 1) * LANES]
        s = _dot_nt(q, kvc)
        s = jnp.where(ok, s - _slope(hd) * distf, NEG_INF)
        e = jnp.exp(s - jnp.max(s, axis=-1, keepdims=True))
        p = jnp.where(ok, e * (1.0 / jnp.sum(e, axis=-1, keepdims=True)), 0.0)
        psum = psum + p
        o = _dot(p.astype(BF16), kvc)
        ocmp_ref[0, :, hd * LANES:(hd + 1) * LANES] = o * g[:, 3 * hd:3 * hd + 1]

    ov = ov_ref[...]
    h1, h2, h3 = _split3(psum)
    imp = _dot(h1, ov) + _dot(h2, ov) + _dot(h3, ov)
    ns = imp.shape[1]
    scol = lax.broadcasted_iota(jnp.int32, (tq, ns), 1)
    scolf = scol.astype(F32)
    trow = t0 + lax.broadcasted_iota(jnp.int32, (tq, ns), 0)
    free = (scol * SEL_BLOCK <= trow) & (scol != 0) & (scol != (trow >> SEL_SHIFT))
    score = jnp.where(free, imp, -1.0)
    lane = lax.broadcasted_iota(jnp.int32, (tq, LANES), 1)
    idx_out = jnp.zeros((tq, LANES), jnp.int32)
    hit = jnp.zeros((tq, LANES), F32)
    for k in range(SEL_TOPK - 2):
        m = jnp.max(score, axis=-1, keepdims=True)
        ikf = jnp.min(jnp.where(score == m, scolf, 1e9), axis=-1, keepdims=True)
        score = jnp.where(scolf == ikf, -3e38, score)
        ik = ikf.astype(jnp.int32)
        idx_out = jnp.where(lane == k, ik, idx_out)
        hit = jnp.where(lane == (ik >> _log2(tsel // SEL_BLOCK)), 1.0, hit)
    idx_ref[0] = idx_out
    for sub in range(tq // tsel):
        flag_ref[0, sub] = jnp.max(hit[sub * tsel:(sub + 1) * tsel], axis=0, keepdims=True).astype(jnp.int32)


def _nsa_cmp(qn, kvc, gates, ov, tq, tsel, n_cmp):
    B, L, _ = qn.shape
    nq = L // tq
    nsub = tq // tsel
    nc = kvc.shape[1]
    row = lambda b, i: (b, i, 0)
    return pl.pallas_call(
        functools.partial(_nsa_cmp_kernel, tq=tq, tsel=tsel, n_cmp=n_cmp),
        grid=(B, nq),
        in_specs=[pl.BlockSpec((1, tq, 512), row), pl.BlockSpec((1, nc, LANES), lambda b, i: (b, 0, 0)),
                  pl.BlockSpec((1, tq, LANES), row), _const_spec(ov.shape)],
        out_specs=[pl.BlockSpec((1, tq, 512), row), pl.BlockSpec((1, tq, LANES), row),
                   pl.BlockSpec((1, nsub, 1, LANES), lambda b, i: (b, i, 0, 0))],
        out_shape=[jax.ShapeDtypeStruct((B, L, 512), F32), jax.ShapeDtypeStruct((B, L, LANES), jnp.int32),
                   jax.ShapeDtypeStruct((B, L // tsel, 1, LANES), jnp.int32)],
        compiler_params=_params("parallel", "parallel"),
        name="nsa_cmp",
    )(qn, kvc, gates, ov)


def _nsa_sw_kernel(bits_ref, q_ref, skv_ref, wkv_ref, idx_ref, g_ref, ocmp_ref, o_ref,
                   m_sc, l_sc, acc_sc, *, tq, nwords):
    b = pl.program_id(0)
    qi = pl.program_id(1)
    nq = pl.num_programs(1)
    t0 = qi * tq
    rows = NSA_HEADS * tq
    qs = jnp.concatenate([q_ref[0, :, hd * LANES:(hd + 1) * LANES] for hd in range(NSA_HEADS)], axis=0)
    rowi = lax.broadcasted_iota(jnp.int32, (rows, LANES), 0)
    ti = t0 + (rowi & (tq - 1))
    tf = ti.astype(F32)
    cur_blk = ti >> SEL_SHIFT
    head = rowi >> _log2(tq)
    slope = jnp.where(head == 0, _slope(0), jnp.where(head == 1, _slope(1),
                      jnp.where(head == 2, _slope(2), _slope(3))))
    idx4 = jnp.concatenate([idx_ref[0]] * NSA_HEADS, axis=0)
    picks = [jnp.broadcast_to(idx4[:, k:k + 1], (rows, LANES)) for k in range(SEL_TOPK - 2)]
    lane_k = lax.broadcasted_iota(jnp.int32, (1, tq), 1)

    def scores(j, kv, forced_blocks, valid):
        kpos = j * tq + lane_k
        blk = kpos >> SEL_SHIFT
        distf = tf - kpos.astype(F32)
        msk = (blk == picks[0]) | (blk == picks[1])
        if forced_blocks:
            min_dist = jnp.where(valid, 0.0, 3e38)
            msk = (msk | (blk == 0) | (blk == cur_blk)) & (distf >= min_dist)
        s = jnp.where(msk, _dot_nt(qs, kv) - slope * distf, NEG_INF)
        return s, msk

    jprev = jnp.maximum(qi - 1, 0)
    tiles = [(0, skv_ref[0, 0:tq, :], qi >= 0), (jprev, skv_ref[0, pl.ds(pl.multiple_of(jprev * tq, tq), tq), :], qi >= 2),
             (qi, skv_ref[0, pl.ds(pl.multiple_of(qi * tq, tq), tq), :], qi >= 1)]
    sm = [scores(j, kv, True, valid) for j, kv, valid in tiles]
    m0 = jnp.maximum(jnp.maximum(jnp.max(sm[0][0], axis=-1, keepdims=True),
                                 jnp.max(sm[1][0], axis=-1, keepdims=True)),
                     jnp.max(sm[2][0], axis=-1, keepdims=True))
    l0 = jnp.zeros((rows, 1), F32)
    acc0 = jnp.zeros((rows, LANES), F32)
    for (s, msk), (_, kv, _) in zip(sm, tiles):
        p = jnp.where(msk, jnp.exp(s - m0), 0.0)
        l0 = l0 + jnp.sum(p, axis=-1, keepdims=True)
        acc0 = acc0 + _dot(p.astype(BF16), kv)
    m_sc[...] = jnp.broadcast_to(m0, (rows, LANES))
    l_sc[...] = jnp.broadcast_to(l0, (rows, LANES))
    acc_sc[...] = acc0

    def sel_step(j, carry):
        word = bits_ref[(b * nq + qi) * nwords + (j >> 5)]
        needed = (lax.shift_right_logical(word, j & 31) & 1) == 1

        @pl.when(needed)
        def _():
            kv = skv_ref[0, pl.ds(pl.multiple_of(j * tq, tq), tq), :]
            s, msk = scores(j, kv, False, None)
            m_old = m_sc[...]
            m_new = jnp.maximum(m_old, jnp.max(s, axis=-1, keepdims=True))
            p = jnp.where(msk, jnp.exp(s - m_new), 0.0)
            alpha = jnp.exp(m_old - m_new)
            l_sc[...] = alpha * l_sc[...] + jnp.sum(p, axis=-1, keepdims=True)
            acc_sc[...] = alpha * acc_sc[...] + _dot(p.astype(BF16), kv)
            m_sc[...] = m_new
        return carry

    lax.fori_loop(1, jnp.maximum(qi - 1, 1), sel_step, 0)
    o_sel = acc_sc[...] / l_sc[...]

    nband = WINDOW + tq
    start = pl.multiple_of(jnp.maximum(t0 - WINDOW, 0), tq)
    band = wkv_ref[0, pl.ds(start, nband), :]
    kposf = (start + lax.broadcasted_iota(jnp.int32, (1, nband), 1)).astype(F32)
    tw = jnp.concatenate([tf] * (nband // LANES), axis=1)
    sw = jnp.concatenate([slope] * (nband // LANES), axis=1)
    distf = tw - kposf
    ok = (distf >= 0.0) & (distf < float(WINDOW))
    s = jnp.where(ok, _dot_nt(qs, band) - sw * distf, NEG_INF)
    e = jnp.exp(s - jnp.max(s, axis=-1, keepdims=True))
    p = e * (1.0 / jnp.sum(e, axis=-1, keepdims=True))
    o_win = _dot(p.astype(BF16), band)

    g = g_ref[0]
    lane = lax.broadcasted_iota(jnp.int32, (1, LANES), 1)
    for hd in range(NSA_HEADS):
        r = slice(hd * tq, (hd + 1) * tq)
        o = (ocmp_ref[0, :, hd * LANES:(hd + 1) * LANES]
             + g[:, 3 * hd + 1:3 * hd + 2] * o_sel[r]
             + g[:, 3 * hd + 2:3 * hd + 3] * o_win[r])
        o_ref[0, :, hd * LANES:(hd + 1) * LANES] = jnp.where(lane >= HEAD_DIM, o, 0.0).astype(BF16)


def _nsa_sw(bits, qn, skv, wkv, idx, gates, ocmp, tq, nwords):
    B, L, _ = qn.shape
    nq = L // tq
    row = lambda b, i, s: (b, i, 0)
    full = lambda b, i, s: (b, 0, 0)
    rows = NSA_HEADS * tq
    return pl.pallas_call(
        functools.partial(_nsa_sw_kernel, tq=tq, nwords=nwords),
        grid_spec=pltpu.PrefetchScalarGridSpec(
            num_scalar_prefetch=1,
            grid=(B, nq),
            in_specs=[pl.BlockSpec((1, tq, 512), row), pl.BlockSpec((1, L, LANES), full),
                      pl.BlockSpec((1, L, LANES), full), pl.BlockSpec((1, tq, LANES), row),
                      pl.BlockSpec((1, tq, LANES), row), pl.BlockSpec((1, tq, 512), row)],
            out_specs=pl.BlockSpec((1, tq, 512), row),
            scratch_shapes=[pltpu.VMEM((rows, LANES), F32), pltpu.VMEM((rows, LANES), F32),
                            pltpu.VMEM((rows, LANES), F32)],
        ),
        out_shape=jax.ShapeDtypeStruct((B, L, 512), BF16),
        compiler_params=_params("parallel", "arbitrary"),
        name="nsa_sw",
    )(bits, qn, skv, wkv, idx, gates, ocmp)


S5_X = S5_GROUPS * S5_STATE
S5_PAD = 8


def _s5_kernel(u_ref, bcat_ref, ccat_ref, a_ref, d_ref, wglu_ref, o_ref, x_sc, st_sc, *, steps, nb, nch):
    @pl.when(pl.program_id(0) == 0)
    def _():
        st_sc[...] = jnp.zeros_like(st_sc)

    stride = steps + S5_PAD
    ng = S5_X // LANES
    for b in range(nb):
        bu = _dot(u_ref[b].astype(BF16), bcat_ref[...])
        for k in range(2 * ng):
            x_sc[k, b * stride:b * stride + steps, :] = bu[:, k * LANES:(k + 1) * LANES]
    gpc = ng // nch
    for c in range(nch):
        ks = list(range(c * gpc, (c + 1) * gpc))
        ar = [jnp.broadcast_to(a_ref[0:1, k * LANES:(k + 1) * LANES], (nb, LANES)) for k in ks]
        ai = [jnp.broadcast_to(a_ref[0:1, S5_X + k * LANES:S5_X + (k + 1) * LANES], (nb, LANES)) for k in ks]

        def body(t, carry):
            xr, xi = carry
            r = pl.ds(t, nb, stride=stride)
            nr, ni = [], []
            for n, k in enumerate(ks):
                vr = ar[n] * xr[n] - ai[n] * xi[n] + x_sc[k, r, :]
                vi = ar[n] * xi[n] + ai[n] * xr[n] + x_sc[ng + k, r, :]
                x_sc[k, r, :] = vr
                x_sc[ng + k, r, :] = vi
                nr.append(vr)
                ni.append(vi)
            return tuple(nr), tuple(ni)

        init = (tuple(st_sc[:, k * LANES:(k + 1) * LANES] for k in ks),
                tuple(st_sc[:, S5_X + k * LANES:S5_X + (k + 1) * LANES] for k in ks))
        xr, xi = lax.fori_loop(0, steps, body, init, unroll=8)
        for n, k in enumerate(ks):
            st_sc[:, k * LANES:(k + 1) * LANES] = xr[n]
            st_sc[:, S5_X + k * LANES:S5_X + (k + 1) * LANES] = xi[n]

    for b in range(nb):
        xs = jnp.concatenate([x_sc[k, b * stride:b * stride + steps, :] for k in range(2 * ng)], axis=-1)
        y = _dot(xs.astype(BF16), ccat_ref[...]) + d_ref[...] * u_ref[b]
        y = _gelu(y)
        o_ref[b] = (y * _sigmoid(_dot(y.astype(BF16), wglu_ref[...]))).astype(BF16)


def _s5(u, bcat, ccat, a_cat, d, wglu, steps):
    nb, L, _ = u.shape
    return pl.pallas_call(
        functools.partial(_s5_kernel, steps=steps, nb=nb, nch=2),
        grid=(L // steps,),
        in_specs=[pl.BlockSpec((nb, steps, S5_WIDTH), lambda i: (0, i, 0)), _const_spec(bcat.shape),
                  _const_spec(ccat.shape), _const_spec(a_cat.shape), _const_spec(d.shape),
                  _const_spec(wglu.shape)],
        out_specs=pl.BlockSpec((nb, steps, S5_WIDTH), lambda i: (0, i, 0)),
        out_shape=jax.ShapeDtypeStruct((nb, L, S5_WIDTH), BF16),
        scratch_shapes=[pltpu.VMEM((2 * S5_X // LANES, nb * (steps + S5_PAD), LANES), F32),
                        pltpu.VMEM((nb, 2 * S5_X), F32)],
        compiler_params=_params("arbitrary"),
        name="s5",
    )(u, bcat, ccat, a_cat, d, wglu)


SIGN_BIT = np.int32(-2 ** 31)


def _sb_kernel(q_ref, kv_ref, u2_ref, o_ref, acc_sc, c_sc, z_sc, w_sc, *, tq, tk):
    i = pl.program_id(2)
    q = q_ref[0]
    u2 = u2_ref[...]
    acc_sc[...] = jnp.zeros_like(acc_sc)
    c_sc[...] = jnp.zeros_like(c_sc)
    nsub = tk // LANES
    jd = (i * tq) // tk
    t = i * tq + lax.broadcasted_iota(jnp.int32, (tq, tk), 0)
    lanei = lax.broadcasted_iota(jnp.int32, (tq, tk), 1)

    def kv_block(j):
        return kv_ref[0, pl.ds(pl.multiple_of(j * tk, tk), tk), :]

    def step(j, first):
        z = z_sc[...]
        z_next = _dot_nt(q, kv_block(jnp.maximum(j - 1, 0)))
        if not first:
            acc_sc[...] += _dot(w_sc[...], kv_block(j + 1))
        neg_abs = lax.bitcast_convert_type(lax.bitcast_convert_type(z, jnp.int32) | SIGN_BIT, F32)
        sp = jnp.maximum(z, 0.0) + jnp.log2(1.0 + jnp.exp2(neg_abs))
        if first:
            causal = (j * tk + lanei) < t
            sp = jnp.where(causal, sp, 0.0)
        spb = sp.astype(BF16)
        c = c_sc[...]
        ws = [None] * nsub
        for s in reversed(range(nsub)):
            cols = slice(s * LANES, (s + 1) * LANES)
            rr = _dot(spb[:, cols], u2)
            ws[s] = jnp.exp2(jnp.minimum(z[:, cols] + rr + c, 0.0))
            c = c + rr[:, 0:1]
        w = jnp.concatenate(ws, axis=-1)
        if first:
            w = jnp.where(causal, w, 0.0)
        w_sc[...] = w.astype(BF16)
        z_sc[...] = z_next
        c_sc[...] = c

    z_sc[...] = _dot_nt(q, kv_block(jd))
    step(jd, True)

    def body(jr, carry):
        step(jd - 1 - jr, False)
        return carry

    lax.fori_loop(0, jd, body, 0)
    acc = acc_sc[...] + _dot(w_sc[...], kv_block(0))
    lane = lax.broadcasted_iota(jnp.int32, (1, LANES), 1)
    o_ref[0] = jnp.where(lane >= HEAD_DIM, acc, 0.0).astype(BF16)


def _sb(sbq, sbkv, u2, tq, tk):
    B, L, _ = sbq.shape
    assert tk % tq == 0 and L % tk == 0
    return pl.pallas_call(
        functools.partial(_sb_kernel, tq=tq, tk=tk),
        grid=(B, SB_HEADS, L // tq),
        in_specs=[pl.BlockSpec((1, tq, LANES), lambda b, h, i: (b, i, h)),
                  pl.BlockSpec((1, L, LANES), lambda b, h, i: (b, 0, h)),
                  _const_spec(u2.shape)],
        out_specs=pl.BlockSpec((1, tq, LANES), lambda b, h, i: (b, i, h)),
        out_shape=jax.ShapeDtypeStruct((B, L, SB_HEADS * LANES), BF16),
        scratch_shapes=[pltpu.VMEM((tq, LANES), F32), pltpu.VMEM((tq, LANES), F32),
                        pltpu.VMEM((tq, tk), F32), pltpu.VMEM((tq, tk), BF16)],
        compiler_params=_params("parallel", "parallel", "arbitrary"),
        name="sb_attn",
    )(sbq, sbkv, u2)


FFN_CHUNK = 512
HALO = 8


def _merge_ffn_kernel(x_ref, oa_ref, ob_ref, oc_ref, mg_ref, wa_ref, wb_ref, wc_ref, wout_ref, gain_ref,
                      wup_ref, cw_ref, cb_ref, wdown_ref, out_ref, ubuf, carry, *, tl, dff):
    @pl.when(pl.program_id(1) == 0)
    def _():
        carry[...] = jnp.zeros_like(carry)

    d = x_ref.shape[2]
    merged = (mg_ref[0, :, 0:d].astype(F32) * _dot(oa_ref[0], wa_ref[...])
              + mg_ref[0, :, d:2 * d].astype(F32) * _dot(ob_ref[0], wb_ref[...])
              + mg_ref[0, :, 2 * d:3 * d].astype(F32) * _dot(oc_ref[0], wc_ref[...]))
    x1 = x_ref[0] + _dot(merged.astype(BF16), wout_ref[...])
    h = x1 * lax.rsqrt(jnp.mean(x1 * x1, axis=-1, keepdims=True) + NORM_EPS) * gain_ref[...]
    hb = h.astype(BF16)

    def conv_cols(off):
        cols = slice(off, off + FFN_CHUNK)
        u = _dot(hb, wup_ref[:, cols])
        ubuf[0:HALO, :] = carry[:, cols]
        ubuf[HALO:HALO + tl, :] = u
        carry[:, cols] = u[tl - HALO:tl, :]
        return (cw_ref[2:3, cols] * u + cw_ref[1:2, cols] * ubuf[HALO - 1:HALO - 1 + tl, :]
                + cw_ref[0:1, cols] * ubuf[HALO - 2:HALO - 2 + tl, :] + cb_ref[:, cols])

    acc = x1
    for j in range(dff // FFN_CHUNK):
        gate = conv_cols(j * FFN_CHUNK)
        val = conv_cols(dff + j * FFN_CHUNK)
        act = (_gelu(gate) * val).astype(BF16)
        acc = acc + _dot(act, wdown_ref[j * FFN_CHUNK:(j + 1) * FFN_CHUNK, :])
    out_ref[0] = acc


def _merge_ffn(x, oa, ob, oc, mg, wa, wb, wc, wout, gain, wup, cw, cb, wdown, tl):
    B, L, D = x.shape
    dff = wdown.shape[0]
    row = lambda b, l: (b, l, 0)
    weights = [wa, wb, wc, wout, gain, wup, cw, cb, wdown]
    return pl.pallas_call(
        functools.partial(_merge_ffn_kernel, tl=tl, dff=dff),
        grid=(B, L // tl),
        in_specs=[pl.BlockSpec((1, tl, D), row), pl.BlockSpec((1, tl, oa.shape[2]), row),
                  pl.BlockSpec((1, tl, ob.shape[2]), row), pl.BlockSpec((1, tl, oc.shape[2]), row),
                  pl.BlockSpec((1, tl, mg.shape[2]), row)] + [_const_spec(w.shape) for w in weights],
        out_specs=pl.BlockSpec((1, tl, D), row),
        out_shape=jax.ShapeDtypeStruct((B, L, D), F32),
        scratch_shapes=[pltpu.VMEM((HALO + tl, FFN_CHUNK), F32), pltpu.VMEM((HALO, 2 * dff), F32)],
        compiler_params=_params("parallel", "arbitrary"),
        name="merge_ffn",
    )(x, oa, ob, oc, mg, *weights)


def _pad_heads(w, n_heads):
    d = w.shape[0]
    w = w.reshape(d, n_heads, HEAD_DIM)
    return jnp.concatenate([w, jnp.zeros_like(w)], axis=-1).reshape(d, n_heads * LANES)


def _pack_w_in(w):
    d = w.shape[0]
    o_g = 640
    o_s5 = o_g + NSA_HEADS * 3
    o_sb = o_s5 + S5_WIDTH
    o_mg = o_sb + 3 * SB_HEADS * HEAD_DIM
    hw = SB_HEADS * HEAD_DIM
    gates = jnp.pad(w[:, o_g:o_s5], ((0, 0), (0, LANES - NSA_HEADS * 3)))
    sbk = w[:, o_sb + hw:o_sb + 2 * hw].reshape(d, SB_HEADS, 1, HEAD_DIM)
    sbv = w[:, o_sb + 2 * hw:o_mg].reshape(d, SB_HEADS, 1, HEAD_DIM)
    sbkv = jnp.concatenate([sbk, sbv], axis=2).reshape(d, SB_HEADS * LANES)
    packed = jnp.concatenate([_pad_heads(w[:, 0:256], NSA_HEADS), w[:, 256:o_g], gates, w[:, o_s5:o_sb],
                              _pad_heads(w[:, o_sb:o_sb + hw], SB_HEADS), sbkv, w[:, o_mg:]], axis=1)
    return packed.astype(BF16)


def _pad_rows(w, n_heads):
    d = w.shape[1]
    w = w.reshape(n_heads, HEAD_DIM, d)
    return jnp.concatenate([jnp.zeros_like(w), w], axis=1).reshape(n_heads * LANES, d).astype(BF16)


def _pad_gain(g, fill):
    return jnp.concatenate([g, jnp.full_like(g, fill)], axis=-1)


def _cmp_weights(wk, wv):
    z = jnp.zeros_like(wk)
    wkv = jnp.concatenate([jnp.concatenate([wk, z], axis=2), jnp.concatenate([z, wv], axis=2)], axis=1)
    half = CMP_BLOCK // 2
    wt = wkv[:half].reshape(half * LANES, LANES)
    wb = wkv[half:].reshape(half * LANES, LANES)
    return wt.astype(BF16), wb.astype(BF16)


def _s5_params(a_re, a_im, log_dt, b_re, b_im, c_re, c_im):
    dt = jnp.exp(log_dt.astype(F32))[:, None]
    A = lax.complex(a_re.astype(F32), a_im.astype(F32))
    A_bar = jnp.exp(dt * A)
    B_bar = ((A_bar - 1.0) / A)[..., None] * lax.complex(b_re.astype(F32), b_im.astype(F32))
    eye = jnp.eye(S5_GROUPS, dtype=F32)

    def bdiag_in(m):
        return jnp.einsum('gpc,gh->gchp', m, eye).reshape(S5_WIDTH, S5_X)

    def bdiag_out(m):
        return jnp.einsum('gcp,gh->gphc', m, eye).reshape(S5_X, S5_WIDTH)

    bcat = jnp.concatenate([bdiag_in(jnp.real(B_bar)), bdiag_in(jnp.imag(B_bar))], axis=1).astype(BF16)
    ccat = jnp.concatenate([bdiag_out(c_re.astype(F32)), -bdiag_out(c_im.astype(F32))], axis=0).astype(BF16)
    a_cat = jnp.concatenate([jnp.real(A_bar).reshape(1, S5_X), jnp.imag(A_bar).reshape(1, S5_X)], axis=1)
    return bcat, ccat, a_cat


def _overlap_matrix(n_rows, n_cmp, n_sel):
    cs = np.arange(n_rows)[:, None] * CMP_STRIDE
    ss = np.arange(n_sel)[None, :] * SEL_BLOCK
    ov = np.clip(np.minimum(cs + CMP_BLOCK, ss + SEL_BLOCK) - np.maximum(cs, ss), 0, None) / CMP_BLOCK
    ov[n_cmp:] = 0.0
    return jnp.asarray(ov, dtype=BF16)


def _sb_suffix_matrix(tk):
    u = (np.arange(tk)[:, None] >= np.arange(tk)[None, :]).astype(np.float32)
    return jnp.asarray(-u, dtype=BF16)


def kernel(x, norm_mix, w_in, nsa_q_gain, nsa_k_gain, cmp_pe, cmp_wk, cmp_wv, s5_a_re, s5_a_im, s5_log_dt,
           s5_b_re, s5_b_im, s5_c_re, s5_c_im, s5_d, s5_w_glu, w_br_nsa, w_br_s5, w_br_sb, w_out, norm_ffn,
           w_up, conv_w, conv_b, w_down):
    B, L, D = x.shape
    depth = w_in.shape[0]
    tq = 128
    tq_cmp = 256
    tl_in = min(512, L)
    tl_ffn = min(512, L)
    sb_tq, sb_tk = 512, 512
    s5_steps = min(256, L)
    assert L % 512 == 0 and L >= WINDOW + tq and D == 1024
    n_cmp = (L - CMP_BLOCK) // CMP_STRIDE + 1
    n_rows = L // CMP_STRIDE
    n_sel = L // SEL_BLOCK
    nq = L // tq
    nwords = (nq + 31) // 32
    ov = _overlap_matrix(n_rows, n_cmp, n_sel)
    u2 = _sb_suffix_matrix(LANES)

    for i in range(depth):
        qg = _pad_gain(nsa_q_gain[i][None, :], 0.0)
        kg = _pad_gain(nsa_k_gain[i], 1.0)
        qn, ckv, skv, wkv, gates, u_s5, sbq, sbkv, mg = _inproj(
            x, norm_mix[i][None, :], _pack_w_in(w_in[i]), qg, kg[1:3], tl_in)

        pe = jnp.concatenate([cmp_pe[i], cmp_pe[i]], axis=-1).reshape(2, (CMP_BLOCK // 2) * LANES)
        wt, wb = _cmp_weights(cmp_wk[i, 0], cmp_wv[i, 0])
        kvc = _cmpkv(ckv.reshape(B, n_rows, CMP_STRIDE * LANES), pe, wt, wb, kg[0:1])
        ocmp, idx, flags = _nsa_cmp(qn, kvc, gates, ov, tq_cmp, tq, n_cmp)
        fl = jnp.pad(flags[:, :, 0, :nq], ((0, 0), (0, 0), (0, nwords * 32 - nq))).reshape(B, nq, nwords, 32)
        bits = jnp.sum(fl << jnp.arange(32, dtype=jnp.int32), axis=-1, dtype=jnp.int32).reshape(-1)
        o_a = _nsa_sw(bits, qn, skv, wkv, idx, gates, ocmp, tq, nwords)

        bcat, ccat, a_cat = _s5_params(s5_a_re[i], s5_a_im[i], s5_log_dt[i], s5_b_re[i], s5_b_im[i],
                                       s5_c_re[i], s5_c_im[i])
        o_b = _s5(u_s5, bcat, ccat, a_cat, s5_d[i].reshape(1, S5_WIDTH), s5_w_glu[i].astype(BF16), s5_steps)

        o_c = _sb(sbq, sbkv, u2, sb_tq, sb_tk)

        x = _merge_ffn(x, o_a, o_b, o_c, mg,
                       _pad_rows(w_br_nsa[i], NSA_HEADS), w_br_s5[i].astype(BF16),
                       _pad_rows(w_br_sb[i], SB_HEADS), w_out[i].astype(BF16), norm_ffn[i][None, :],
                       w_up[i].astype(BF16), conv_w[i], conv_b[i][None, :], w_down[i].astype(BF16), tl_ffn)
    return x
```

```python
import functools
import math

import numpy as np
import jax
import jax.numpy as jnp
from jax import lax
from jax.experimental import pallas as pl
from jax.experimental.pallas import tpu as pltpu

HEAD_DIM = 64
NSA_HEADS = 4
CMP_BLOCK = 32
CMP_STRIDE = 16
SEL_BLOCK = 32
SEL_TOPK = 4
WINDOW = 512
FORCE_SCORE = 1.0e4
S5_WIDTH = 256
S5_GROUP = 16
S5_GROUPS = S5_WIDTH // S5_GROUP
S5_STATE = 64
SB_HEADS = 4
NORM_EPS = 1e-6
NEG_INF = -1e30
LANES = 128
VMEM_LIMIT = 56 * 1024 * 1024

F32 = jnp.float32
BF16 = jnp.bfloat16


def _log2(n):
    assert n & (n - 1) == 0
    return n.bit_length() - 1


SEL_SHIFT = _log2(SEL_BLOCK)
LOG2E = math.log2(math.e)


def _dot(a, b):
    return jnp.dot(a, b, preferred_element_type=F32)


def _dot_nt(a, b):
    return lax.dot_general(a, b, (((1,), (1,)), ((), ())), preferred_element_type=F32)


def _const_spec(shape):
    nd = len(shape)
    return pl.BlockSpec(shape, lambda *_: (0,) * nd, pipeline_mode=pl.Buffered(1))


def _params(*sem):
    return pltpu.CompilerParams(dimension_semantics=sem, vmem_limit_bytes=VMEM_LIMIT)


def _gelu(x):
    return 0.5 * x * (1.0 + jnp.tanh(math.sqrt(2.0 / math.pi) * (x + 0.044715 * (x * x * x))))


def _sigmoid(x):
    return 1.0 / (1.0 + jnp.exp(-x))


def _split3(x):
    h1 = x.astype(BF16)
    r1 = x - h1.astype(F32)
    h2 = r1.astype(BF16)
    h3 = (r1 - h2.astype(F32)).astype(BF16)
    return h1, h2, h3


C_QN = 0
C_KV = 512
C_G = 896
C_S5 = 1024
C_SBQ = 1280
C_SBKV = 1792
C_MG = 2304
C_END = C_MG + 3072


def _inproj_kernel(x_ref, gain_ref, w_ref, qg_ref, kg_ref,
                   qn_ref, ckv_ref, skv_ref, wkv_ref, g_ref, s5_ref, sbq_ref, sbkv_ref, mg_ref):
    x = x_ref[0]
    h = x * lax.rsqrt(jnp.mean(x * x, axis=-1, keepdims=True) + NORM_EPS) * gain_ref[...]
    hb = h.astype(BF16)
    lane = lax.broadcasted_iota(jnp.int32, (1, LANES), 1)
    lo = lane < HEAD_DIM

    seg = _dot(hb, w_ref[:, C_QN:C_S5])
    scale = HEAD_DIM ** -0.5
    for hd in range(NSA_HEADS):
        q = seg[:, hd * LANES:(hd + 1) * LANES]
        ms = jnp.sum(q * q, axis=-1, keepdims=True) * (1.0 / HEAD_DIM)
        qn = q * lax.rsqrt(ms + NORM_EPS) * qg_ref[...] * scale
        qn_ref[0, :, hd * LANES:(hd + 1) * LANES] = qn.astype(BF16)
    ckv_ref[0] = seg[:, C_KV:C_KV + LANES].astype(BF16)
    for n, ref in ((1, skv_ref), (2, wkv_ref)):
        kv = seg[:, C_KV + n * LANES:C_KV + (n + 1) * LANES]
        ms = jnp.sum(jnp.where(lo, kv * kv, 0.0), axis=-1, keepdims=True) * (1.0 / HEAD_DIM)
        sc = jnp.where(lo, lax.rsqrt(ms + NORM_EPS) * kg_ref[n - 1:n, :], 1.0)
        ref[0] = (kv * sc).astype(BF16)
    g_ref[0] = _sigmoid(seg[:, C_G:C_G + LANES])

    seg = _dot(hb, w_ref[:, C_S5:C_MG])
    s5_ref[0] = seg[:, 0:S5_WIDTH]
    sbq_ref[0] = (seg[:, C_SBQ - C_S5:C_SBKV - C_S5] * (scale * LOG2E)).astype(BF16)
    sbkv_ref[0] = seg[:, C_SBKV - C_S5:C_MG - C_S5].astype(BF16)

    for c in range(3):
        seg = _dot(hb, w_ref[:, C_MG + c * 1024:C_MG + (c + 1) * 1024])
        mg_ref[0, :, c * 1024:(c + 1) * 1024] = _sigmoid(seg).astype(BF16)


def _inproj(x, gain, w, qg, kg, tl):
    B, L, D = x.shape
    grid = (B, L // tl)
    row = lambda b, l: (b, l, 0)

    def out(n, dt):
        return jax.ShapeDtypeStruct((B, L, n), dt), pl.BlockSpec((1, tl, n), row)

    outs = [out(512, BF16), out(LANES, BF16), out(LANES, BF16), out(LANES, BF16), out(LANES, F32),
            out(S5_WIDTH, F32), out(512, BF16), out(512, BF16), out(3072, BF16)]
    return pl.pallas_call(
        _inproj_kernel,
        grid=grid,
        in_specs=[pl.BlockSpec((1, tl, D), row), _const_spec((1, D)), _const_spec((D, C_END)),
                  _const_spec((1, LANES)), _const_spec((2, LANES))],
        out_specs=[o[1] for o in outs],
        out_shape=[o[0] for o in outs],
        compiler_params=_params("parallel", "parallel"),
        name="inproj",
    )(x, gain, w, qg, kg)


def _cmpkv_kernel(x_ref, pe_ref, wt_ref, wb_ref, kg_ref, o_ref):
    x = x_ref[0].astype(F32)
    a = _dot((x + pe_ref[0:1, :]).astype(BF16), wt_ref[...])
    b = _dot((x + pe_ref[1:2, :]).astype(BF16), wb_ref[...])
    n = x.shape[0]
    kv = a + pltpu.roll(b, n - 1, 0)
    lane = lax.broadcasted_iota(jnp.int32, (1, LANES), 1)
    lo = lane < HEAD_DIM
    ms = jnp.sum(jnp.where(lo, kv * kv, 0.0), axis=-1, keepdims=True) * (1.0 / HEAD_DIM)
    sc = jnp.where(lo, lax.rsqrt(ms + NORM_EPS) * kg_ref[...], 1.0)
    rowi = lax.broadcasted_iota(jnp.int32, (n, 1), 0)
    o_ref[0] = jnp.where(rowi < n - 1, kv * sc, 0.0).astype(BF16)


def _cmpkv(x16, pe2, wt, wb, kg):
    B, n, w = x16.shape
    return pl.pallas_call(
        _cmpkv_kernel,
        grid=(B,),
        in_specs=[pl.BlockSpec((1, n, w), lambda b: (b, 0, 0)), _const_spec((2, w)),
                  _const_spec((w, LANES)), _const_spec((w, LANES)), _const_spec((1, LANES))],
        out_specs=pl.BlockSpec((1, n, LANES), lambda b: (b, 0, 0)),
        out_shape=jax.ShapeDtypeStruct((B, n, LANES), BF16),
        compiler_params=_params("parallel"),
        name="cmpkv",
    )(x16, pe2, wt, wb, kg)


def _slope(hd):
    return 2.0 ** (-8.0 * (hd + 1) / NSA_HEADS)


def _nsa_cmp_kernel(q_ref, kvc_ref, g_ref, ov_ref, ocmp_ref, idx_ref, flag_ref, *, tq, tsel, n_cmp):
    qi = pl.program_id(1)
    nc_all, ns_all = ov_ref.shape
    need = (qi + 1) * (tq // CMP_STRIDE)
    bounds = list(range(LANES, nc_all, LANES)) + [nc_all]
    for v, nc in enumerate(bounds):
        lo = bounds[v - 1] if v else 0
        ns = min(ns_all, max(LANES, nc // (SEL_BLOCK // CMP_STRIDE)))
        pl.when((need > lo) & (need <= nc))(functools.partial(
            _nsa_cmp_body, q_ref, kvc_ref, g_ref, ov_ref, ocmp_ref, idx_ref, flag_ref, qi * tq,
            tq=tq, tsel=tsel, n_cmp=n_cmp, nc=nc, ns=ns))


def _nsa_cmp_body(q_ref, kvc_ref, g_ref, ov_ref, ocmp_ref, idx_ref, flag_ref, t0, *, tq, tsel, n_cmp, nc, ns):
    kvc = kvc_ref[0, 0:nc, :]
    g = g_ref[0]
    col = lax.broadcasted_iota(jnp.int32, (tq, nc), 1)
    t = t0 + lax.broadcasted_iota(jnp.int32, (tq, nc), 0)
    dist = t - (col * CMP_STRIDE + (CMP_BLOCK - 1))
    ok = (dist >= 0) & (col < n_cmp)
    distf = dist.astype(F32)
    psum = jnp.zeros((tq, nc), F32)
    for hd in range(NSA_HEADS):
        q = q_ref[0, :, hd * LANES:(hd + 1) * LANES]
        s = _dot_nt(q, kvc)
        s = jnp.where(ok, s - _slope(hd) * distf, NEG_INF)
        e = jnp.exp(s - jnp.max(s, axis=-1, keepdims=True))
        p = jnp.where(ok, e * (1.0 / jnp.sum(e, axis=-1, keepdims=True)), 0.0)
        psum = psum + p
        o = _dot(p.astype(BF16), kvc)
        ocmp_ref[0, :, hd * LANES:(hd + 1) * LANES] = o * g[:, 3 * hd:3 * hd + 1]

    ov = ov_ref[0:nc, 0:ns]
    h1, h2, h3 = _split3(psum)
    imp = _dot(h1, ov) + _dot(h2, ov) + _dot(h3, ov)
    scol =lax.broadcasted_iota(jnp.int32, (tq, ns), 1)
    scolf = scol.astype(F32)
    trow = t0 + lax.broadcasted_iota(jnp.int32, (tq, ns), 0)
    free = (scol * SEL_BLOCK <= trow) & (scol != 0) & (scol != (trow >> SEL_SHIFT))
    score = jnp.where(free, imp, -1.0)
    lane = lax.broadcasted_iota(jnp.int32, (tq, LANES), 1)
    idx_out = jnp.zeros((tq, LANES), jnp.int32)
    hit = jnp.zeros((tq, LANES), F32)
    for k in range(SEL_TOPK - 2):
        m = jnp.max(score, axis=-1, keepdims=True)
        ikf = jnp.min(jnp.where(score == m, scolf, 1e9), axis=-1, keepdims=True)
        score = jnp.where(scolf == ikf, -3e38, score)
        ik = ikf.astype(jnp.int32)
        idx_out = jnp.where(lane == k, ik, idx_out)
        hit = jnp.where(lane == (ik >> _log2(tsel // SEL_BLOCK)), 1.0, hit)
    idx_ref[0] = idx_out
    for sub in range(tq // tsel):
        flag_ref[0, sub] = jnp.max(hit[sub * tsel:(sub + 1) * tsel], axis=0, keepdims=True).astype(jnp.int32)


def _nsa_cmp(qn, kvc, gates, ov, tq, tsel, n_cmp):
    B, L, _ = qn.shape
    nq = L // tq
    nsub = tq // tsel
    nc = kvc.shape[1]
    row = lambda b, i: (b, i, 0)
    return pl.pallas_call(
        functools.partial(_nsa_cmp_kernel, tq=tq, tsel=tsel, n_cmp=n_cmp),
        grid=(B, nq),
        in_specs=[pl.BlockSpec((1, tq, 512), row), pl.BlockSpec((1, nc, LANES), lambda b, i: (b, 0, 0)),
                  pl.BlockSpec((1, tq, LANES), row), _const_spec(ov.shape)],
        out_specs=[pl.BlockSpec((1, tq, 512), row), pl.BlockSpec((1, tq, LANES), row),
                   pl.BlockSpec((1, nsub, 1, LANES), lambda b, i: (b, i, 0, 0))],
        out_shape=[jax.ShapeDtypeStruct((B, L, 512), F32), jax.ShapeDtypeStruct((B, L, LANES), jnp.int32),
                   jax.ShapeDtypeStruct((B, L // tsel, 1, LANES), jnp.int32)],
        compiler_params=_params("parallel", "parallel"),
        name="nsa_cmp",
    )(qn, kvc, gates, ov)


def _nsa_sw_kernel(bits_ref, q_ref, skv_ref, wkv_ref, idx_ref, g_ref, ocmp_ref, o_ref,
                   m_sc, l_sc, acc_sc, *, tq, nwords):
    b = pl.program_id(0)
    qi = pl.program_id(1)
    nq = pl.num_programs(1)
    t0 = qi * tq
    rows = NSA_HEADS * tq
    qs = jnp.concatenate([q_ref[0, :, hd * LANES:(hd + 1) * LANES] for hd in range(NSA_HEADS)], axis=0)
    rowi = lax.broadcasted_iota(jnp.int32, (rows, LANES), 0)
    ti = t0 + (rowi & (tq - 1))
    tf = ti.astype(F32)
    cur_blk = ti >> SEL_SHIFT
    head = rowi >> _log2(tq)
    slope = jnp.where(head == 0, _slope(0), jnp.where(head == 1, _slope(1),
                      jnp.where(head == 2, _slope(2), _slope(3))))
    idx4 = jnp.concatenate([idx_ref[0]] * NSA_HEADS, axis=0)
    picks = [jnp.broadcast_to(idx4[:, k:k + 1], (rows, LANES)) for k in range(SEL_TOPK - 2)]
    lane_k = lax.broadcasted_iota(jnp.int32, (1, tq), 1)

    def scores(j, kv, forced_blocks, valid):
        kpos = j * tq + lane_k
        blk = kpos >> SEL_SHIFT
        distf = tf - kpos.astype(F32)
        msk = (blk == picks[0]) | (blk == picks[1])
        if forced_blocks:
            min_dist = jnp.where(valid, 0.0, 3e38)
            msk = (msk | (blk == 0) | (blk == cur_blk)) & (distf >= min_dist)
        s = jnp.where(msk, _dot_nt(qs, kv) - slope * distf, NEG_INF)
        return s, msk

    jprev = jnp.maximum(qi - 1, 0)
    tiles = [(0, skv_ref[0, 0:tq, :], qi >= 0), (jprev, skv_ref[0, pl.ds(pl.multiple_of(jprev * tq, tq), tq), :], qi >= 2),
             (qi, skv_ref[0, pl.ds(pl.multiple_of(qi * tq, tq), tq), :], qi >= 1)]
    sm = [scores(j, kv, True, valid) for j, kv, valid in tiles]
    m0 = jnp.maximum(jnp.maximum(jnp.max(sm[0][0], axis=-1, keepdims=True),
                                 jnp.max(sm[1][0], axis=-1, keepdims=True)),
                     jnp.max(sm[2][0], axis=-1, keepdims=True))
    l0 = jnp.zeros((rows, 1), F32)
    acc0 = jnp.zeros((rows, LANES), F32)
    for (s, msk), (_, kv, _) in zip(sm, tiles):
        p = jnp.where(msk, jnp.exp(s - m0), 0.0)
        l0 = l0 + jnp.sum(p, axis=-1, keepdims=True)
        acc0 = acc0 + _dot(p.astype(BF16), kv)
    m_sc[...] = jnp.broadcast_to(m0, (rows, LANES))
    l_sc[...] = jnp.broadcast_to(l0, (rows, LANES))
    acc_sc[...] = acc0

    def sel_step(j, carry):
        word = bits_ref[(b * nq + qi) * nwords + (j >> 5)]
        needed = (lax.shift_right_logical(word, j & 31) & 1) == 1

        @pl.when(needed)
        def _():
            kv = skv_ref[0, pl.ds(pl.multiple_of(j * tq, tq), tq), :]
            s, msk = scores(j, kv, False, None)
            m_old = m_sc[...]
            m_new = jnp.maximum(m_old, jnp.max(s, axis=-1, keepdims=True))
            p = jnp.where(msk, jnp.exp(s - m_new), 0.0)
            alpha = jnp.exp(m_old - m_new)
            l_sc[...] = alpha * l_sc[...] + jnp.sum(p, axis=-1, keepdims=True)
            acc_sc[...] = alpha * acc_sc[...] + _dot(p.astype(BF16), kv)
            m_sc[...] = m_new
        return carry

    lax.fori_loop(1, jnp.maximum(qi - 1, 1), sel_step, 0)
    o_sel = acc_sc[...] / l_sc[...]

    nband = WINDOW + tq
    start = pl.multiple_of(jnp.maximum(t0 - WINDOW, 0), tq)
    band = wkv_ref[0, pl.ds(start, nband), :]
    kposf = (start + lax.broadcasted_iota(jnp.int32, (1, nband), 1)).astype(F32)
    tw = jnp.concatenate([tf] * (nband // LANES), axis=1)
    sw = jnp.concatenate([slope] * (nband // LANES), axis=1)
    distf = tw - kposf
    ok = (distf >= 0.0) & (distf < float(WINDOW))
    s = jnp.where(ok, _dot_nt(qs, band) - sw * distf, NEG_INF)
    e = jnp.exp(s - jnp.max(s, axis=-1, keepdims=True))
    p = e * (1.0 / jnp.sum(e, axis=-1, keepdims=True))
    o_win = _dot(p.astype(BF16), band)

    g = g_ref[0]
    lane = lax.broadcasted_iota(jnp.int32, (1, LANES), 1)
    for hd in range(NSA_HEADS):
        r = slice(hd * tq, (hd + 1) * tq)
        o = (ocmp_ref[0, :, hd * LANES:(hd + 1) * LANES]
             + g[:, 3 * hd + 1:3 * hd + 2] * o_sel[r]
             + g[:, 3 * hd + 2:3 * hd + 3] * o_win[r])
        o_ref[0, :, hd * LANES:(hd + 1) * LANES] = jnp.where(lane >= HEAD_DIM, o, 0.0).astype(BF16)


def _nsa_sw(bits, qn, skv, wkv, idx, gates, ocmp, tq, nwords):
    B, L, _ = qn.shape
    nq = L // tq
    row = lambda b, i, s: (b, i, 0)
    full = lambda b, i, s: (b, 0, 0)
    rows = NSA_HEADS * tq
    return pl.pallas_call(
        functools.partial(_nsa_sw_kernel, tq=tq, nwords=nwords),
        grid_spec=pltpu.PrefetchScalarGridSpec(
            num_scalar_prefetch=1,
            grid=(B, nq),
            in_specs=[pl.BlockSpec((1, tq, 512), row), pl.BlockSpec((1, L, LANES), full),
                      pl.BlockSpec((1, L, LANES), full), pl.BlockSpec((1, tq, LANES), row),
                      pl.BlockSpec((1, tq, LANES), row), pl.BlockSpec((1, tq, 512), row)],
            out_specs=pl.BlockSpec((1, tq, 512), row),
            scratch_shapes=[pltpu.VMEM((rows, LANES), F32), pltpu.VMEM((rows, LANES), F32),
                            pltpu.VMEM((rows, LANES), F32)],
        ),
        out_shape=jax.ShapeDtypeStruct((B, L, 512), BF16),
        compiler_params=_params("parallel", "arbitrary"),
        name="nsa_sw",
    )(bits, qn, skv, wkv, idx, gates, ocmp)


S5_X = S5_GROUPS * S5_STATE
S5_PAD = 8


def _s5_kernel(u_ref, bcat_ref, ccat_ref, a_ref, d_ref, wglu_ref, o_ref, x_sc, st_sc, *, steps, nb, nch):
    @pl.when(pl.program_id(0) == 0)
    def _():
        st_sc[...] = jnp.zeros_like(st_sc)

    stride = steps + S5_PAD
    ng = S5_X // LANES
    for b in range(nb):
        bu = _dot(u_ref[b].astype(BF16), bcat_ref[...])
        for k in range(2 * ng):
            x_sc[k, b * stride:b * stride + steps, :] = bu[:, k * LANES:(k + 1) * LANES]
    gpc = ng // nch
    for c in range(nch):
        ks = list(range(c * gpc, (c + 1) * gpc))
        ar = [jnp.broadcast_to(a_ref[0:1, k * LANES:(k + 1) * LANES], (nb, LANES)) for k in ks]
        ai = [jnp.broadcast_to(a_ref[0:1, S5_X + k * LANES:S5_X + (k + 1) * LANES], (nb, LANES)) for k in ks]

        def body(t, carry):
            xr, xi = carry
            r = pl.ds(t, nb, stride=stride)
            nr, ni = [], []
            for n, k in enumerate(ks):
                vr = ar[n] * xr[n] - ai[n] * xi[n] + x_sc[k, r, :]
                vi = ar[n] * xi[n] + ai[n] * xr[n] + x_sc[ng + k, r, :]
                x_sc[k, r, :] = vr
                x_sc[ng + k, r, :] = vi
                nr.append(vr)
                ni.append(vi)
            return tuple(nr), tuple(ni)

        init = (tuple(st_sc[:, k * LANES:(k + 1) * LANES] for k in ks),
                tuple(st_sc[:, S5_X + k * LANES:S5_X + (k + 1) * LANES] for k in ks))
        xr, xi = lax.fori_loop(0, steps, body, init, unroll=8)
        for n, k in enumerate(ks):
            st_sc[:, k * LANES:(k + 1) * LANES] = xr[n]
            st_sc[:, S5_X + k * LANES:S5_X + (k + 1) * LANES] = xi[n]

    for b in range(nb):
        xs = jnp.concatenate([x_sc[k, b * stride:b * stride + steps, :] for k in range(2 * ng)], axis=-1)
        y = _dot(xs.astype(BF16), ccat_ref[...]) + d_ref[...] * u_ref[b]
        y = _gelu(y)
        o_ref[b] = (y * _sigmoid(_dot(y.astype(BF16), wglu_ref[...]))).astype(BF16)


def _s5(u, bcat, ccat, a_cat, d, wglu, steps):
    nb, L, _ = u.shape
    return pl.pallas_call(
        functools.partial(_s5_kernel, steps=steps, nb=nb, nch=2),
        grid=(L // steps,),
        in_specs=[pl.BlockSpec((nb, steps, S5_WIDTH), lambda i: (0, i, 0)), _const_spec(bcat.shape),
                  _const_spec(ccat.shape), _const_spec(a_cat.shape), _const_spec(d.shape),
                  _const_spec(wglu.shape)],
        out_specs=pl.BlockSpec((nb, steps, S5_WIDTH), lambda i: (0, i, 0)),
        out_shape=jax.ShapeDtypeStruct((nb, L, S5_WIDTH), BF16),
        scratch_shapes=[pltpu.VMEM((2 * S5_X // LANES, nb * (steps + S5_PAD), LANES), F32),
                        pltpu.VMEM((nb, 2 * S5_X), F32)],
        compiler_params=_params("arbitrary"),
        name="s5",
    )(u, bcat, ccat, a_cat, d, wglu)


SIGN_BIT = np.int32(-2 ** 31)


def _sb_kernel(q_ref, kv_ref, tri_ref, o_ref, acc_sc, c_sc, z_sc, w_sc, *, tq, tk):
    i = pl.program_id(2)
    q = q_ref[0]
    half = tk // 2
    acc_sc[...] = jnp.zeros_like(acc_sc)
    c_sc[...] = jnp.zeros_like(c_sc)
    jd = (i * tq) // tk
    t = i * tq + lax.broadcasted_iota(jnp.int32, (tq, tk), 0)
    lanei = lax.broadcasted_iota(jnp.int32, (tq, tk), 1)

    def kv_block(j):
        return kv_ref[0, pl.ds(pl.multiple_of(j * tk, tk), tk), :]

    def step(j, first):
        z = z_sc[...]
        z_next = _dot_nt(q, kv_block(jnp.maximum(j - 1, 0)))
        if not first:
            acc_sc[...] += _dot(w_sc[...], kv_block(j + 1))
        neg_abs = lax.bitcast_convert_type(lax.bitcast_convert_type(z, jnp.int32) | SIGN_BIT, F32)
        sp = jnp.maximum(z, 0.0) + jnp.log2(1.0 + jnp.exp2(neg_abs))
        if first:
            causal = (j * tk + lanei) < t
            sp = jnp.where(causal, sp, 0.0)
        spb = sp.astype(BF16)
        r_lo = _dot(spb, tri_ref[...])
        r_hi = _dot(spb[:, half:], tri_ref[0:half, :])
        c = c_sc[...]
        cw = jnp.concatenate([c] * (half // LANES), axis=-1)
        w = jnp.concatenate([jnp.exp2(jnp.minimum(z[:, :half] + r_lo + cw, 0.0)),
                             jnp.exp2(jnp.minimum(z[:, half:] + r_hi + cw, 0.0))], axis=-1)
        if first:
            w = jnp.where(causal, w, 0.0)
        w_sc[...] = w.astype(BF16)
        z_sc[...] = z_next
        c_sc[...] = c + r_lo[:, 0:1]

    z_sc[...] = _dot_nt(q, kv_block(jd))
    step(jd, True)

    def body(jr, carry):
        step(jd - 1 - jr, False)
        return carry

    lax.fori_loop(0, jd, body, 0)
    acc = acc_sc[...] + _dot(w_sc[...], kv_block(0))
    lane = lax.broadcasted_iota(jnp.int32, (1, LANES), 1)
    o_ref[0] = jnp.where(lane >= HEAD_DIM, acc, 0.0).astype(BF16)


def _sb(sbq, sbkv, u2, tq, tk):
    B, L, _ = sbq.shape
    assert tk % tq == 0 and L % tk == 0
    return pl.pallas_call(
        functools.partial(_sb_kernel, tq=tq, tk=tk),
        grid=(B, SB_HEADS, L // tq),
        in_specs=[pl.BlockSpec((1, tq, LANES), lambda b, h, i: (b, i, h)),
                  pl.BlockSpec((1, L, LANES), lambda b, h, i: (b, 0, h)),
                  _const_spec(u2.shape)],
        out_specs=pl.BlockSpec((1, tq, LANES), lambda b, h, i: (b, i, h)),
        out_shape=jax.ShapeDtypeStruct((B, L, SB_HEADS * LANES), BF16),
        scratch_shapes=[pltpu.VMEM((tq, LANES), F32), pltpu.VMEM((tq, LANES), F32),
                        pltpu.VMEM((tq, tk), F32), pltpu.VMEM((tq, tk), BF16)],
        compiler_params=_params("parallel", "parallel", "arbitrary"),
        name="sb_attn",
    )(sbq, sbkv, u2)


FFN_CHUNK = 1024
HALO = 8


def _merge_ffn_kernel(x_ref, oa_ref, ob_ref, oc_ref, mg_ref, wa_ref, wb_ref, wc_ref, wout_ref, gain_ref,
                      wup_ref, cw_ref, cb_ref, wdown_ref, out_ref, ubuf, carry, *, tl, dff):
    @pl.when(pl.program_id(1) == 0)
    def _():
        carry[...] = jnp.zeros_like(carry)

    d = x_ref.shape[2]
    merged = (mg_ref[0, :, 0:d].astype(F32) * _dot(oa_ref[0], wa_ref[...])
              + mg_ref[0, :, d:2 * d].astype(F32) * _dot(ob_ref[0], wb_ref[...])
              + mg_ref[0, :, 2 * d:3 * d].astype(F32) * _dot(oc_ref[0], wc_ref[...]))
    x1 = x_ref[0] + _dot(merged.astype(BF16), wout_ref[...])
    h = x1 * lax.rsqrt(jnp.mean(x1 * x1, axis=-1, keepdims=True) + NORM_EPS) * gain_ref[...]
    hb = h.astype(BF16)

    def conv_cols(off):
        cols = slice(off, off + FFN_CHUNK)
        u = _dot(hb, wup_ref[:, cols])
        ubuf[0:HALO, :] = carry[:, cols]
        ubuf[HALO:HALO + tl, :] = u
        carry[:, cols] = u[tl - HALO:tl, :]
        return (cw_ref[2:3, cols] * u + cw_ref[1:2, cols] * ubuf[HALO - 1:HALO - 1 + tl, :]
                + cw_ref[0:1, cols] * ubuf[HALO - 2:HALO - 2 + tl, :] + cb_ref[:, cols])

    acc = x1
    for j in range(dff // FFN_CHUNK):
        gate = conv_cols(j * FFN_CHUNK)
        val = conv_cols(dff + j * FFN_CHUNK)
        act = (_gelu(gate) * val).astype(BF16)
        acc = acc + _dot(act, wdown_ref[j * FFN_CHUNK:(j + 1) * FFN_CHUNK, :])
    out_ref[0] = acc


def _merge_ffn(x, oa, ob, oc, mg, wa, wb, wc, wout, gain, wup, cw, cb, wdown, tl):
    B, L, D = x.shape
    dff = wdown.shape[0]
    row = lambda b, l: (b, l, 0)
    weights = [wa, wb, wc, wout, gain, wup, cw, cb, wdown]
    return pl.pallas_call(
        functools.partial(_merge_ffn_kernel, tl=tl, dff=dff),
        grid=(B, L // tl),
        in_specs=[pl.BlockSpec((1, tl, D), row), pl.BlockSpec((1, tl, oa.shape[2]), row),
                  pl.BlockSpec((1, tl, ob.shape[2]), row), pl.BlockSpec((1, tl, oc.shape[2]), row),
                  pl.BlockSpec((1, tl, mg.shape[2]), row)] + [_const_spec(w.shape) for w in weights],
        out_specs=pl.BlockSpec((1, tl, D), row),
        out_shape=jax.ShapeDtypeStruct((B, L, D), F32),
        scratch_shapes=[pltpu.VMEM((HALO + tl, FFN_CHUNK), F32), pltpu.VMEM((HALO, 2 * dff), F32)],
        compiler_params=_params("parallel", "arbitrary"),
        name="merge_ffn",
    )(x, oa, ob, oc, mg, *weights)


def _pad_heads(w, n_heads):
    d = w.shape[0]
    w = w.reshape(d, n_heads, HEAD_DIM)
    return jnp.concatenate([w, jnp.zeros_like(w)], axis=-1).reshape(d, n_heads * LANES)


def _pack_w_in(w):
    d = w.shape[0]
    o_g = 640
    o_s5 = o_g + NSA_HEADS * 3
    o_sb = o_s5 + S5_WIDTH
    o_mg = o_sb + 3 * SB_HEADS * HEAD_DIM
    hw = SB_HEADS * HEAD_DIM
    gates = jnp.pad(w[:, o_g:o_s5], ((0, 0), (0, LANES - NSA_HEADS * 3)))
    sbk = w[:, o_sb + hw:o_sb + 2 * hw].reshape(d, SB_HEADS, 1, HEAD_DIM)
    sbv = w[:, o_sb + 2 * hw:o_mg].reshape(d, SB_HEADS, 1, HEAD_DIM)
    sbkv = jnp.concatenate([sbk, sbv], axis=2).reshape(d, SB_HEADS * LANES)
    packed = jnp.concatenate([_pad_heads(w[:, 0:256], NSA_HEADS), w[:, 256:o_g], gates, w[:, o_s5:o_sb],
                              _pad_heads(w[:, o_sb:o_sb + hw], SB_HEADS), sbkv, w[:, o_mg:]], axis=1)
    return packed.astype(BF16)


def _pad_rows(w, n_heads):
    d = w.shape[1]
    w = w.reshape(n_heads, HEAD_DIM, d)
    return jnp.concatenate([jnp.zeros_like(w), w], axis=1).reshape(n_heads * LANES, d).astype(BF16)


def _pad_gain(g, fill):
    return jnp.concatenate([g, jnp.full_like(g, fill)], axis=-1)


def _cmp_weights(wk, wv):
    z = jnp.zeros_like(wk)
    wkv = jnp.concatenate([jnp.concatenate([wk, z], axis=2), jnp.concatenate([z, wv], axis=2)], axis=1)
    half = CMP_BLOCK // 2
    wt = wkv[:half].reshape(half * LANES, LANES)
    wb = wkv[half:].reshape(half * LANES, LANES)
    return wt.astype(BF16), wb.astype(BF16)


def _s5_params(a_re, a_im, log_dt, b_re, b_im, c_re, c_im):
    dt = jnp.exp(log_dt.astype(F32))[:, None]
    A = lax.complex(a_re.astype(F32), a_im.astype(F32))
    A_bar = jnp.exp(dt * A)
    B_bar = ((A_bar - 1.0) / A)[..., None] * lax.complex(b_re.astype(F32), b_im.astype(F32))
    eye = jnp.eye(S5_GROUPS, dtype=F32)

    def bdiag_in(m):
        return jnp.einsum('gpc,gh->gchp', m, eye).reshape(S5_WIDTH, S5_X)

    def bdiag_out(m):
        return jnp.einsum('gcp,gh->gphc', m, eye).reshape(S5_X, S5_WIDTH)

    bcat = jnp.concatenate([bdiag_in(jnp.real(B_bar)), bdiag_in(jnp.imag(B_bar))], axis=1).astype(BF16)
    ccat = jnp.concatenate([bdiag_out(c_re.astype(F32)), -bdiag_out(c_im.astype(F32))], axis=0).astype(BF16)
    a_cat = jnp.concatenate([jnp.real(A_bar).reshape(1, S5_X), jnp.imag(A_bar).reshape(1, S5_X)], axis=1)
    return bcat, ccat, a_cat


def _overlap_matrix(n_rows, n_cmp, n_sel):
    cs = np.arange(n_rows)[:, None] * CMP_STRIDE
    ss = np.arange(n_sel)[None, :] * SEL_BLOCK
    ov = np.clip(np.minimum(cs + CMP_BLOCK, ss + SEL_BLOCK) - np.maximum(cs, ss), 0, None) / CMP_BLOCK
    ov[n_cmp:] = 0.0
    return jnp.asarray(ov, dtype=BF16)


def _sb_suffix_matrix(tk):
    u = (np.arange(tk)[:, None] >= np.arange(tk)[None, :]).astype(np.float32)
    return jnp.asarray(-np.concatenate([u, np.ones_like(u)], axis=0), dtype=BF16)


def kernel(x, norm_mix, w_in, nsa_q_gain, nsa_k_gain, cmp_pe, cmp_wk, cmp_wv, s5_a_re, s5_a_im, s5_log_dt,
           s5_b_re, s5_b_im, s5_c_re, s5_c_im, s5_d, s5_w_glu, w_br_nsa, w_br_s5, w_br_sb, w_out, norm_ffn,
           w_up, conv_w, conv_b, w_down):
    B, L, D = x.shape
    depth = w_in.shape[0]
    tq = 128
    tq_cmp = 256
    tl_in = min(512, L)
    tl_ffn = min(512, L)
    sb_tq, sb_tk = 512, 512
    s5_steps = min(256, L)
    assert L % 512 == 0 and L >= WINDOW + tq and D == 1024
    n_cmp = (L - CMP_BLOCK) // CMP_STRIDE + 1
    n_rows = L // CMP_STRIDE
    n_sel = L // SEL_BLOCK
    nq = L // tq
    nwords = (nq + 31) // 32
    ov = _overlap_matrix(n_rows, n_cmp, n_sel)
    u2 = _sb_suffix_matrix(sb_tk // 2)

    for i in range(depth):
        qg = _pad_gain(nsa_q_gain[i][None, :], 0.0)
        kg = _pad_gain(nsa_k_gain[i], 1.0)
        qn, ckv, skv, wkv, gates, u_s5, sbq, sbkv, mg = _inproj(
            x, norm_mix[i][None, :], _pack_w_in(w_in[i]), qg, kg[1:3], tl_in)

        pe = jnp.concatenate([cmp_pe[i], cmp_pe[i]], axis=-1).reshape(2, (CMP_BLOCK // 2) * LANES)
        wt, wb = _cmp_weights(cmp_wk[i, 0], cmp_wv[i, 0])
        kvc = _cmpkv(ckv.reshape(B, n_rows, CMP_STRIDE * LANES), pe, wt, wb, kg[0:1])
        ocmp, idx, flags = _nsa_cmp(qn, kvc, gates, ov, tq_cmp, tq, n_cmp)
        fl = jnp.pad(flags[:, :, 0, :nq], ((0, 0), (0, 0), (0, nwords * 32 - nq))).reshape(B, nq, nwords, 32)
        bits = jnp.sum(fl << jnp.arange(32, dtype=jnp.int32), axis=-1, dtype=jnp.int32).reshape(-1)
        o_a = _nsa_sw(bits, qn, skv, wkv, idx, gates, ocmp, tq, nwords)

        bcat, ccat, a_cat = _s5_params(s5_a_re[i], s5_a_im[i], s5_log_dt[i], s5_b_re[i], s5_b_im[i],
                                       s5_c_re[i], s5_c_im[i])
        o_b = _s5(u_s5, bcat, ccat, a_cat, s5_d[i].reshape(1, S5_WIDTH), s5_w_glu[i].astype(BF16), s5_steps)

        o_c = _sb(sbq, sbkv, u2, sb_tq, sb_tk)

        x = _merge_ffn(x, o_a, o_b, o_c, mg,
                       _pad_rows(w_br_nsa[i], NSA_HEADS), w_br_s5[i].astype(BF16),
                       _pad_rows(w_br_sb[i], SB_HEADS), w_out[i].astype(BF16), norm_ffn[i][None, :],
                       w_up[i].astype(BF16), conv_w[i], conv_b[i][None, :], w_down[i].astype(BF16), tl_ffn)
    return x
```

```python
import functools
import math

import numpy as np
import jax
import jax.numpy as jnp
from jax import lax
from jax.experimental import pallas as pl
from jax.experimental.pallas import tpu as pltpu

HEAD_DIM = 64
NSA_HEADS = 4
CMP_BLOCK = 32
CMP_STRIDE = 16
SEL_BLOCK = 32
SEL_TOPK = 4
WINDOW = 512
FORCE_SCORE = 1.0e4
S5_WIDTH = 256
S5_GROUP = 16
S5_GROUPS = S5_WIDTH // S5_GROUP
S5_STATE = 64
SB_HEADS = 4
NORM_EPS = 1e-6
NEG_INF = -1e30
LANES = 128
VMEM_LIMIT = 56 * 1024 * 1024

F32 = jnp.float32
BF16 = jnp.bfloat16


def _log2(n):
    assert n & (n - 1) == 0
    return n.bit_length() - 1


SEL_SHIFT = _log2(SEL_BLOCK)
LOG2E = math.log2(math.e)


def _dot(a, b):
    return jnp.dot(a, b, preferred_element_type=F32)


def _dot_nt(a, b):
    return lax.dot_general(a, b, (((1,), (1,)), ((), ())), preferred_element_type=F32)


def _const_spec(shape):
    nd = len(shape)
    return pl.BlockSpec(shape, lambda *_: (0,) * nd, pipeline_mode=pl.Buffered(1))


def _params(*sem):
    return pltpu.CompilerParams(dimension_semantics=sem, vmem_limit_bytes=VMEM_LIMIT)


def _gelu(x):
    return 0.5 * x * (1.0 + jnp.tanh(math.sqrt(2.0 / math.pi) * (x + 0.044715 * (x * x * x))))


def _sigmoid(x):
    return 1.0 / (1.0 + jnp.exp(-x))


def _split3(x):
    h1 = x.astype(BF16)
    r1 = x - h1.astype(F32)
    h2 = r1.astype(BF16)
    h3 = (r1 - h2.astype(F32)).astype(BF16)
    return h1, h2, h3


C_QN = 0
C_KV = 512
C_G = 896
C_S5 = 1024
C_SBQ = 1280
C_SBKV = 1792
C_MG = 2304
C_END = C_MG + 3072


def _inproj_kernel(x_ref, gain_ref, w_ref, qg_ref, kg_ref,
                   qn_ref, ckv_ref, skv_ref, wkv_ref, g_ref, s5_ref, sbq_ref, sbkv_ref, mg_ref):
    x = x_ref[0]
    h = x * lax.rsqrt(jnp.mean(x * x, axis=-1, keepdims=True) + NORM_EPS) * gain_ref[...]
    hb = h.astype(BF16)
    lane = lax.broadcasted_iota(jnp.int32, (1, LANES), 1)
    lo = lane < HEAD_DIM

    seg = _dot(hb, w_ref[:, C_QN:C_S5])
    scale = HEAD_DIM ** -0.5
    for hd in range(NSA_HEADS):
        q = seg[:, hd * LANES:(hd + 1) * LANES]
        ms = jnp.sum(q * q, axis=-1, keepdims=True) * (1.0 / HEAD_DIM)
        qn = q * lax.rsqrt(ms + NORM_EPS) * qg_ref[...] * scale
        qn_ref[0, :, hd * LANES:(hd + 1) * LANES] = qn.astype(BF16)
    ckv_ref[0] = seg[:, C_KV:C_KV + LANES].astype(BF16)
    for n, ref in ((1, skv_ref), (2, wkv_ref)):
        kv = seg[:, C_KV + n * LANES:C_KV + (n + 1) * LANES]
        ms = jnp.sum(jnp.where(lo, kv * kv, 0.0), axis=-1, keepdims=True) * (1.0 / HEAD_DIM)
        sc = jnp.where(lo, lax.rsqrt(ms + NORM_EPS) * kg_ref[n - 1:n, :], 1.0)
        ref[0] = (kv * sc).astype(BF16)
    g_ref[0] = _sigmoid(seg[:, C_G:C_G + LANES])

    seg = _dot(hb, w_ref[:, C_S5:C_MG])
    s5_ref[0] = seg[:, 0:S5_WIDTH]
    sbq_ref[0] = (seg[:, C_SBQ - C_S5:C_SBKV - C_S5] * (scale * LOG2E)).astype(BF16)
    sbkv_ref[0] = seg[:, C_SBKV - C_S5:C_MG - C_S5].astype(BF16)

    for c in range(3):
        seg = _dot(hb, w_ref[:, C_MG + c * 1024:C_MG + (c + 1) * 1024])
        mg_ref[0, :, c * 1024:(c + 1) * 1024] = _sigmoid(seg).astype(BF16)


def _inproj(x, gain, w, qg, kg, tl):
    B, L, D = x.shape
    grid = (B, L // tl)
    row = lambda b, l: (b, l, 0)

    def out(n, dt):
        return jax.ShapeDtypeStruct((B, L, n), dt), pl.BlockSpec((1, tl, n), row)

    outs = [out(512, BF16), out(LANES, BF16), out(LANES, BF16), out(LANES, BF16), out(LANES, F32),
            out(S5_WIDTH, F32), out(512, BF16), out(512, BF16), out(3072, BF16)]
    return pl.pallas_call(
        _inproj_kernel,
        grid=grid,
        in_specs=[pl.BlockSpec((1, tl, D), row), _const_spec((1, D)), _const_spec((D, C_END)),
                  _const_spec((1, LANES)), _const_spec((2, LANES))],
        out_specs=[o[1] for o in outs],
        out_shape=[o[0] for o in outs],
        compiler_params=_params("parallel", "parallel"),
        name="inproj",
    )(x, gain, w, qg, kg)


def _cmpkv_kernel(x_ref, pe_ref, wt_ref, wb_ref, kg_ref, o_ref):
    x = x_ref[0].astype(F32)
    a = _dot((x + pe_ref[0:1, :]).astype(BF16), wt_ref[...])
    b = _dot((x + pe_ref[1:2, :]).astype(BF16), wb_ref[...])
    n = x.shape[0]
    kv = a + pltpu.roll(b, n - 1, 0)
    lane = lax.broadcasted_iota(jnp.int32, (1, LANES), 1)
    lo = lane < HEAD_DIM
    ms = jnp.sum(jnp.where(lo, kv * kv, 0.0), axis=-1, keepdims=True) * (1.0 / HEAD_DIM)
    sc = jnp.where(lo, lax.rsqrt(ms + NORM_EPS) * kg_ref[...], 1.0)
    rowi = lax.broadcasted_iota(jnp.int32, (n, 1), 0)
    o_ref[0] = jnp.where(rowi < n - 1, kv * sc, 0.0).astype(BF16)


def _cmpkv(x16, pe2, wt, wb, kg):
    B, n, w = x16.shape
    return pl.pallas_call(
        _cmpkv_kernel,
        grid=(B,),
        in_specs=[pl.BlockSpec((1, n, w), lambda b: (b, 0, 0)), _const_spec((2, w)),
                  _const_spec((w, LANES)), _const_spec((w, LANES)), _const_spec((1, LANES))],
        out_specs=pl.BlockSpec((1, n, LANES), lambda b: (b, 0, 0)),
        out_shape=jax.ShapeDtypeStruct((B, n, LANES), BF16),
        compiler_params=_params("parallel"),
        name="cmpkv",
    )(x16, pe2, wt, wb, kg)


def _slope(hd):
    return 2.0 ** (-8.0 * (hd + 1) / NSA_HEADS)


def _nsa_cmp_kernel(q_ref, kvc_ref, g_ref, ov_ref, ocmp_ref, idx_ref, flag_ref, *, tq, tsel, n_cmp):
    qi = pl.program_id(1)
    nc_all, ns_all = ov_ref.shape
    need = (qi + 1) * (tq // CMP_STRIDE)
    bounds = list(range(LANES, nc_all, LANES)) + [nc_all]
    for v, nc in enumerate(bounds):
        lo = bounds[v - 1] if v else 0
        ns = min(ns_all, max(LANES, nc // (SEL_BLOCK // CMP_STRIDE)))
        pl.when((need > lo) & (need <= nc))(functools.partial(
            _nsa_cmp_body, q_ref, kvc_ref, g_ref, ov_ref, ocmp_ref, idx_ref, flag_ref, qi * tq,
            tq=tq, tsel=tsel, n_cmp=n_cmp, nc=nc, ns=ns))


def _nsa_cmp_body(q_ref, kvc_ref, g_ref, ov_ref, ocmp_ref, idx_ref, flag_ref, t0, *, tq, tsel, n_cmp, nc, ns):
    kvc = kvc_ref[0, 0:nc, :]
    g = g_ref[0]
    col = lax.broadcasted_iota(jnp.int32, (tq, nc), 1)
    t = t0 + lax.broadcasted_iota(jnp.int32, (tq, nc), 0)
    dist = t - (col * CMP_STRIDE + (CMP_BLOCK - 1))
    ok = (dist >= 0) & (col < n_cmp)
    distf = dist.astype(F32)
    psum = jnp.zeros((tq, nc), F32)
    for hd in range(NSA_HEADS):
        q = q_ref[0, :, hd * LANES:(hd + 1) * LANES]
        s = _dot_nt(q, kvc)
        s = jnp.where(ok, s - _slope(hd) * distf, NEG_INF)
        e = jnp.exp(s - jnp.max(s, axis=-1, keepdims=True))
        p = jnp.where(ok, e * (1.0 / jnp.sum(e, axis=-1, keepdims=True)), 0.0)
        psum = psum + p
        o = _dot(p.astype(BF16), kvc)
        ocmp_ref[0, :, hd * LANES:(hd + 1) * LANES] = o * g[:, 3 * hd:3 * hd + 1]

    ov = ov_ref[0:nc, 0:ns]
    h1, h2, h3 = _split3(psum)
    imp = _dot(h1, ov) + _dot(h2, ov) + _dot(h3, ov)
    scol =lax.broadcasted_iota(jnp.int32, (tq, ns), 1)
    scolf = scol.astype(F32)
    trow = t0 + lax.broadcasted_iota(jnp.int32, (tq, ns), 0)
    free = (scol * SEL_BLOCK <= trow) & (scol != 0) & (scol != (trow >> SEL_SHIFT))
    score = jnp.where(free, imp, -1.0)
    lane = lax.broadcasted_iota(jnp.int32, (tq, LANES), 1)
    idx_out = jnp.zeros((tq, LANES), jnp.int32)
    hit = jnp.zeros((tq, LANES), F32)
    for k in range(SEL_TOPK - 2):
        m = jnp.max(score, axis=-1, keepdims=True)
        ikf = jnp.min(jnp.where(score == m, scolf, 1e9), axis=-1, keepdims=True)
        score = jnp.where(scolf == ikf, -3e38, score)
        ik = ikf.astype(jnp.int32)
        idx_out = jnp.where(lane == k, ik, idx_out)
        hit = jnp.where(lane == (ik >> _log2(tsel // SEL_BLOCK)), 1.0, hit)
    idx_ref[0] = idx_out
    for sub in range(tq // tsel):
        flag_ref[0, sub] = jnp.max(hit[sub * tsel:(sub + 1) * tsel], axis=0, keepdims=True).astype(jnp.int32)


def _nsa_cmp(qn, kvc, gates, ov, tq, tsel, n_cmp):
    B, L, _ = qn.shape
    nq = L // tq
    nsub = tq // tsel
    nc = kvc.shape[1]
    row = lambda b, i: (b, i, 0)
    return pl.pallas_call(
        functools.partial(_nsa_cmp_kernel, tq=tq, tsel=tsel, n_cmp=n_cmp),
        grid=(B, nq),
        in_specs=[pl.BlockSpec((1, tq, 512), row), pl.BlockSpec((1, nc, LANES), lambda b, i: (b, 0, 0)),
                  pl.BlockSpec((1, tq, LANES), row), _const_spec(ov.shape)],
        out_specs=[pl.BlockSpec((1, tq, 512), row), pl.BlockSpec((1, tq, LANES), row),
                   pl.BlockSpec((1, nsub, 1, LANES), lambda b, i: (b, i, 0, 0))],
        out_shape=[jax.ShapeDtypeStruct((B, L, 512), F32), jax.ShapeDtypeStruct((B, L, LANES), jnp.int32),
                   jax.ShapeDtypeStruct((B, L // tsel, 1, LANES), jnp.int32)],
        compiler_params=_params("parallel", "parallel"),
        name="nsa_cmp",
    )(qn, kvc, gates, ov)


def _nsa_sw_kernel(bits_ref, q_ref, skv_ref, wkv_ref, idx_ref, g_ref, ocmp_ref, o_ref,
                   m_sc, l_sc, acc_sc, *, tq, nwords):
    b = pl.program_id(0)
    qi = pl.program_id(1)
    nq = pl.num_programs(1)
    t0 = qi * tq
    rows = NSA_HEADS * tq
    qs = jnp.concatenate([q_ref[0, :, hd * LANES:(hd + 1) * LANES] for hd in range(NSA_HEADS)], axis=0)
    rowi = lax.broadcasted_iota(jnp.int32, (rows, LANES), 0)
    ti = t0 + (rowi & (tq - 1))
    tf = ti.astype(F32)
    cur_blk = ti >> SEL_SHIFT
    head = rowi >> _log2(tq)
    slope = jnp.where(head == 0, _slope(0), jnp.where(head == 1, _slope(1),
                      jnp.where(head == 2, _slope(2), _slope(3))))
    idx4 = jnp.concatenate([idx_ref[0]] * NSA_HEADS, axis=0)
    picks = [jnp.broadcast_to(idx4[:, k:k + 1], (rows, LANES)) for k in range(SEL_TOPK - 2)]
    lane_k = lax.broadcasted_iota(jnp.int32, (1, tq), 1)

    def scores(j, kv, forced_blocks, valid):
        kpos = j * tq + lane_k
        blk = kpos >> SEL_SHIFT
        distf = tf - kpos.astype(F32)
        msk = (blk == picks[0]) | (blk == picks[1])
        if forced_blocks:
            min_dist = jnp.where(valid, 0.0, 3e38)
            msk = (msk | (blk == 0) | (blk == cur_blk)) & (distf >= min_dist)
        s = jnp.where(msk, _dot_nt(qs, kv) - slope * distf, NEG_INF)
        return s, msk

    jprev = jnp.maximum(qi - 1, 0)
    tiles = [(0, skv_ref[0, 0:tq, :], qi >= 0), (jprev, skv_ref[0, pl.ds(pl.multiple_of(jprev * tq, tq), tq), :], qi >= 2),
             (qi, skv_ref[0, pl.ds(pl.multiple_of(qi * tq, tq), tq), :], qi >= 1)]
    sm = [scores(j, kv, True, valid) for j, kv, valid in tiles]
    m0 = jnp.maximum(jnp.maximum(jnp.max(sm[0][0], axis=-1, keepdims=True),
                                 jnp.max(sm[1][0], axis=-1, keepdims=True)),
                     jnp.max(sm[2][0], axis=-1, keepdims=True))
    l0 = jnp.zeros((rows, 1), F32)
    acc0 = jnp.zeros((rows, LANES), F32)
    for (s, msk), (_, kv, _) in zip(sm, tiles):
        p = jnp.where(msk, jnp.exp(s - m0), 0.0)
        l0 = l0 + jnp.sum(p, axis=-1, keepdims=True)
        acc0 = acc0 + _dot(p.astype(BF16), kv)
    m_sc[...] = jnp.broadcast_to(m0, (rows, LANES))
    l_sc[...] = jnp.broadcast_to(l0, (rows, LANES))
    acc_sc[...] = acc0

    def sel_step(j, carry):
        word = bits_ref[(b * nq + qi) * nwords + (j >> 5)]
        needed = (lax.shift_right_logical(word, j & 31) & 1) == 1

        @pl.when(needed)
        def _():
            kv = skv_ref[0, pl.ds(pl.multiple_of(j * tq, tq), tq), :]
            s, msk = scores(j, kv, False, None)
            m_old = m_sc[...]
            m_new = jnp.maximum(m_old, jnp.max(s, axis=-1, keepdims=True))
            p = jnp.where(msk, jnp.exp(s - m_new), 0.0)
            alpha = jnp.exp(m_old - m_new)
            l_sc[...] = alpha * l_sc[...] + jnp.sum(p, axis=-1, keepdims=True)
            acc_sc[...] = alpha * acc_sc[...] + _dot(p.astype(BF16), kv)
            m_sc[...] = m_new
        return carry

    lax.fori_loop(1, jnp.maximum(qi - 1, 1), sel_step, 0)
    o_sel = acc_sc[...] / l_sc[...]

    nband = WINDOW + tq
    start = pl.multiple_of(jnp.maximum(t0 - WINDOW, 0), tq)
    band = wkv_ref[0, pl.ds(start, nband), :]
    kposf = (start + lax.broadcasted_iota(jnp.int32, (1, nband), 1)).astype(F32)
    tw = jnp.concatenate([tf] * (nband // LANES), axis=1)
    sw = jnp.concatenate([slope] * (nband // LANES), axis=1)
    distf = tw - kposf
    ok = (distf >= 0.0) & (distf < float(WINDOW))
    s = jnp.where(ok, _dot_nt(qs, band) - sw * distf, NEG_INF)
    e = jnp.exp(s - jnp.max(s, axis=-1, keepdims=True))
    p = e * (1.0 / jnp.sum(e, axis=-1, keepdims=True))
    o_win = _dot(p.astype(BF16), band)

    g = g_ref[0]
    lane = lax.broadcasted_iota(jnp.int32, (1, LANES), 1)
    for hd in range(NSA_HEADS):
        r = slice(hd * tq, (hd + 1) * tq)
        o = (ocmp_ref[0, :, hd * LANES:(hd + 1) * LANES]
             + g[:, 3 * hd + 1:3 * hd + 2] * o_sel[r]
             + g[:, 3 * hd + 2:3 * hd + 3] * o_win[r])
        o_ref[0, :, hd * LANES:(hd + 1) * LANES] = jnp.where(lane >= HEAD_DIM, o, 0.0).astype(BF16)


def _nsa_sw(bits, qn, skv, wkv, idx, gates, ocmp, tq, nwords):
    B, L, _ = qn.shape
    nq = L // tq
    row = lambda b, i, s: (b, i, 0)
    full = lambda b, i, s: (b, 0, 0)
    rows = NSA_HEADS * tq
    return pl.pallas_call(
        functools.partial(_nsa_sw_kernel, tq=tq, nwords=nwords),
        grid_spec=pltpu.PrefetchScalarGridSpec(
            num_scalar_prefetch=1,
            grid=(B, nq),
            in_specs=[pl.BlockSpec((1, tq, 512), row), pl.BlockSpec((1, L, LANES), full),
                      pl.BlockSpec((1, L, LANES), full), pl.BlockSpec((1, tq, LANES), row),
                      pl.BlockSpec((1, tq, LANES), row), pl.BlockSpec((1, tq, 512), row)],
            out_specs=pl.BlockSpec((1, tq, 512), row),
            scratch_shapes=[pltpu.VMEM((rows, LANES), F32), pltpu.VMEM((rows, LANES), F32),
                            pltpu.VMEM((rows, LANES), F32)],
        ),
        out_shape=jax.ShapeDtypeStruct((B, L, 512), BF16),
        compiler_params=_params("parallel", "arbitrary"),
        name="nsa_sw",
    )(bits, qn, skv, wkv, idx, gates, ocmp)


S5_X = S5_GROUPS * S5_STATE
S5_PAD = 8


def _s5_kernel(u_ref, bcat_ref, ccat_ref, a_ref, d_ref, wglu_ref, o_ref, x_sc, st_sc, *, steps, nb, nch):
    @pl.when(pl.program_id(0) == 0)
    def _():
        st_sc[...] = jnp.zeros_like(st_sc)

    stride = steps + S5_PAD
    ng = S5_X // LANES
    for b in range(nb):
        bu = _dot(u_ref[b].astype(BF16), bcat_ref[...])
        for k in range(2 * ng):
            x_sc[k, b * stride:b * stride + steps, :] = bu[:, k * LANES:(k + 1) * LANES]
    gpc = ng // nch
    for c in range(nch):
        ks = list(range(c * gpc, (c + 1) * gpc))
        ar = [jnp.broadcast_to(a_ref[0:1, k * LANES:(k + 1) * LANES], (nb, LANES)) for k in ks]
        ai = [jnp.broadcast_to(a_ref[0:1, S5_X + k * LANES:S5_X + (k + 1) * LANES], (nb, LANES)) for k in ks]

        def body(t, carry):
            xr, xi = carry
            r = pl.ds(t, nb, stride=stride)
            nr, ni = [], []
            for n, k in enumerate(ks):
                vr = ar[n] * xr[n] - ai[n] * xi[n] + x_sc[k, r, :]
                vi = ar[n] * xi[n] + ai[n] * xr[n] + x_sc[ng + k, r, :]
                x_sc[k, r, :] = vr
                x_sc[ng + k, r, :] = vi
                nr.append(vr)
                ni.append(vi)
            return tuple(nr), tuple(ni)

        init = (tuple(st_sc[:, k * LANES:(k + 1) * LANES] for k in ks),
                tuple(st_sc[:, S5_X + k * LANES:S5_X + (k + 1) * LANES] for k in ks))
        xr, xi = lax.fori_loop(0, steps, body, init, unroll=8)
        for n, k in enumerate(ks):
            st_sc[:, k * LANES:(k + 1) * LANES] = xr[n]
            st_sc[:, S5_X + k * LANES:S5_X + (k + 1) * LANES] = xi[n]

    for b in range(nb):
        xs = jnp.concatenate([x_sc[k, b * stride:b * stride + steps, :] for k in range(2 * ng)], axis=-1)
        y = _dot(xs.astype(BF16), ccat_ref[...]) + d_ref[...] * u_ref[b]
        y = _gelu(y)
        o_ref[b] = (y * _sigmoid(_dot(y.astype(BF16), wglu_ref[...]))).astype(BF16)


def _s5(u, bcat, ccat, a_cat, d, wglu, steps):
    nb, L, _ = u.shape
    return pl.pallas_call(
        functools.partial(_s5_kernel, steps=steps, nb=nb, nch=2),
        grid=(L // steps,),
        in_specs=[pl.BlockSpec((nb, steps, S5_WIDTH), lambda i: (0, i, 0)), _const_spec(bcat.shape),
                  _const_spec(ccat.shape), _const_spec(a_cat.shape), _const_spec(d.shape),
                  _const_spec(wglu.shape)],
        out_specs=pl.BlockSpec((nb, steps, S5_WIDTH), lambda i: (0, i, 0)),
        out_shape=jax.ShapeDtypeStruct((nb, L, S5_WIDTH), BF16),
        scratch_shapes=[pltpu.VMEM((2 * S5_X // LANES, nb * (steps + S5_PAD), LANES), F32),
                        pltpu.VMEM((nb, 2 * S5_X), F32)],
        compiler_params=_params("arbitrary"),
        name="s5",
    )(u, bcat, ccat, a_cat, d, wglu)


SIGN_BIT = np.int32(-2 ** 31)
SB_STOP_LOG2 = 170.0


def _sb_kernel(q_ref, kv_ref, tri_ref, o_ref, acc_sc, c_sc, z_sc, w_sc, *, tq, tk):
    i = pl.program_id(2)
    q = q_ref[0]
    half = tk // 2
    acc_sc[...] = jnp.zeros_like(acc_sc)
    c_sc[...] = jnp.zeros_like(c_sc)
    jd = (i * tq) // tk
    t = i * tq + lax.broadcasted_iota(jnp.int32, (tq, tk), 0)
    lanei = lax.broadcasted_iota(jnp.int32, (tq, tk), 1)

    def kv_block(j):
        return kv_ref[0, pl.ds(pl.multiple_of(j * tk, tk), tk), :]

    def step(j, first):
        z = z_sc[...]
        z_next = _dot_nt(q, kv_block(jnp.maximum(j - 1, 0)))
        if not first:
            acc_sc[...] += _dot(w_sc[...], kv_block(j + 1))
        neg_abs = lax.bitcast_convert_type(lax.bitcast_convert_type(z, jnp.int32) | SIGN_BIT, F32)
        sp = jnp.maximum(z, 0.0) + jnp.log2(1.0 + jnp.exp2(neg_abs))
        if first:
            causal = (j * tk + lanei) < t
            sp = jnp.where(causal, sp, 0.0)
        spb = sp.astype(BF16)
        r_lo = _dot(spb, tri_ref[...])
        r_hi = _dot(spb[:, half:], tri_ref[0:half, :])
        c = c_sc[...]
        cw = jnp.concatenate([c] * (half // LANES), axis=-1)
        w = jnp.concatenate([jnp.exp2(jnp.minimum(z[:, :half] + r_lo + cw, 0.0)),
                             jnp.exp2(jnp.minimum(z[:, half:] + r_hi + cw, 0.0))], axis=-1)
        if first:
            w = jnp.where(causal, w, 0.0)
        w_sc[...] = w.astype(BF16)
        z_sc[...] = z_next
        c_sc[...] = c + r_lo[:, 0:1]

    z_sc[...] = _dot_nt(q, kv_block(jd))
    step(jd, True)

    def cond(carry):
        j, cmax = carry
        return (j >= 0) & (cmax > -SB_STOP_LOG2)

    def body(carry):
        j, _ = carry
        step(j, False)
        return j - 1, jnp.max(c_sc[...])

    j_next, _ = lax.while_loop(cond, body, (jd - 1, jnp.max(c_sc[...])))
    acc = acc_sc[...] + _dot(w_sc[...], kv_block(j_next + 1))
    lane = lax.broadcasted_iota(jnp.int32, (1, LANES), 1)
    o_ref[0] = jnp.where(lane >= HEAD_DIM, acc, 0.0).astype(BF16)


def _sb(sbq, sbkv, u2, tq, tk):
    B, L, _ = sbq.shape
    assert tk % tq == 0 and L % tk == 0
    return pl.pallas_call(
        functools.partial(_sb_kernel, tq=tq, tk=tk),
        grid=(B, SB_HEADS, L // tq),
        in_specs=[pl.BlockSpec((1, tq, LANES), lambda b, h, i: (b, i, h)),
                  pl.BlockSpec((1, L, LANES), lambda b, h, i: (b, 0, h)),
                  _const_spec(u2.shape)],
        out_specs=pl.BlockSpec((1, tq, LANES), lambda b, h, i: (b, i, h)),
        out_shape=jax.ShapeDtypeStruct((B, L, SB_HEADS * LANES), BF16),
        scratch_shapes=[pltpu.VMEM((tq, LANES), F32), pltpu.VMEM((tq, LANES), F32),
                        pltpu.VMEM((tq, tk), F32), pltpu.VMEM((tq, tk), BF16)],
        compiler_params=_params("parallel", "parallel", "arbitrary"),
        name="sb_attn",
    )(sbq, sbkv, u2)


FFN_CHUNK = 1024
HALO = 8


def _merge_ffn_kernel(x_ref, oa_ref, ob_ref, oc_ref, mg_ref, wa_ref, wb_ref, wc_ref, wout_ref, gain_ref,
                      wup_ref, cw_ref, cb_ref, wdown_ref, out_ref, ubuf, carry, *, tl, dff):
    @pl.when(pl.program_id(1) == 0)
    def _():
        carry[...] = jnp.zeros_like(carry)

    d = x_ref.shape[2]
    merged = (mg_ref[0, :, 0:d].astype(F32) * _dot(oa_ref[0], wa_ref[...])
              + mg_ref[0, :, d:2 * d].astype(F32) * _dot(ob_ref[0], wb_ref[...])
              + mg_ref[0, :, 2 * d:3 * d].astype(F32) * _dot(oc_ref[0], wc_ref[...]))
    x1 = x_ref[0] + _dot(merged.astype(BF16), wout_ref[...])
    h = x1 * lax.rsqrt(jnp.mean(x1 * x1, axis=-1, keepdims=True) + NORM_EPS) * gain_ref[...]
    hb = h.astype(BF16)

    def conv_cols(off):
        cols = slice(off, off + FFN_CHUNK)
        u = _dot(hb, wup_ref[:, cols])
        ubuf[0:HALO, :] = carry[:, cols]
        ubuf[HALO:HALO + tl, :] = u
        carry[:, cols] = u[tl - HALO:tl, :]
        return (cw_ref[2:3, cols] * u + cw_ref[1:2, cols] * ubuf[HALO - 1:HALO - 1 + tl, :]
                + cw_ref[0:1, cols] * ubuf[HALO - 2:HALO - 2 + tl, :] + cb_ref[:, cols])

    acc = x1
    for j in range(dff // FFN_CHUNK):
        gate = conv_cols(j * FFN_CHUNK)
        val = conv_cols(dff + j * FFN_CHUNK)
        act = (_gelu(gate) * val).astype(BF16)
        acc = acc + _dot(act, wdown_ref[j * FFN_CHUNK:(j + 1) * FFN_CHUNK, :])
    out_ref[0] = acc


def _merge_ffn(x, oa, ob, oc, mg, wa, wb, wc, wout, gain, wup, cw, cb, wdown, tl):
    B, L, D = x.shape
    dff = wdown.shape[0]
    row = lambda b, l: (b, l, 0)
    weights = [wa, wb, wc, wout, gain, wup, cw, cb, wdown]
    return pl.pallas_call(
        functools.partial(_merge_ffn_kernel, tl=tl, dff=dff),
        grid=(B, L // tl),
        in_specs=[pl.BlockSpec((1, tl, D), row), pl.BlockSpec((1, tl, oa.shape[2]), row),
                  pl.BlockSpec((1, tl, ob.shape[2]), row), pl.BlockSpec((1, tl, oc.shape[2]), row),
                  pl.BlockSpec((1, tl, mg.shape[2]), row)] + [_const_spec(w.shape) for w in weights],
        out_specs=pl.BlockSpec((1, tl, D), row),
        out_shape=jax.ShapeDtypeStruct((B, L, D), F32),
        scratch_shapes=[pltpu.VMEM((HALO + tl, FFN_CHUNK), F32), pltpu.VMEM((HALO, 2 * dff), F32)],
        compiler_params=_params("parallel", "arbitrary"),
        name="merge_ffn",
    )(x, oa, ob, oc, mg, *weights)


def _pad_heads(w, n_heads):
    d = w.shape[0]
    w = w.reshape(d, n_heads, HEAD_DIM)
    return jnp.concatenate([w, jnp.zeros_like(w)], axis=-1).reshape(d, n_heads * LANES)


def _pack_w_in(w):
    d = w.shape[0]
    o_g = 640
    o_s5 = o_g + NSA_HEADS * 3
    o_sb = o_s5 + S5_WIDTH
    o_mg = o_sb + 3 * SB_HEADS * HEAD_DIM
    hw = SB_HEADS * HEAD_DIM
    gates = jnp.pad(w[:, o_g:o_s5], ((0, 0), (0, LANES - NSA_HEADS * 3)))
    sbk = w[:, o_sb + hw:o_sb + 2 * hw].reshape(d, SB_HEADS, 1, HEAD_DIM)
    sbv = w[:, o_sb + 2 * hw:o_mg].reshape(d, SB_HEADS, 1, HEAD_DIM)
    sbkv = jnp.concatenate([sbk, sbv], axis=2).reshape(d, SB_HEADS * LANES)
    packed = jnp.concatenate([_pad_heads(w[:, 0:256], NSA_HEADS), w[:, 256:o_g], gates, w[:, o_s5:o_sb],
                              _pad_heads(w[:, o_sb:o_sb + hw], SB_HEADS), sbkv, w[:, o_mg:]], axis=1)
    return packed.astype(BF16)


def _pad_rows(w, n_heads):
    d = w.shape[1]
    w = w.reshape(n_heads, HEAD_DIM, d)
    return jnp.concatenate([jnp.zeros_like(w), w], axis=1).reshape(n_heads * LANES, d).astype(BF16)


def _pad_gain(g, fill):
    return jnp.concatenate([g, jnp.full_like(g, fill)], axis=-1)


def _cmp_weights(wk, wv):
    z = jnp.zeros_like(wk)
    wkv = jnp.concatenate([jnp.concatenate([wk, z], axis=2), jnp.concatenate([z, wv], axis=2)], axis=1)
    half = CMP_BLOCK // 2
    wt = wkv[:half].reshape(half * LANES, LANES)
    wb = wkv[half:].reshape(half * LANES, LANES)
    return wt.astype(BF16), wb.astype(BF16)


def _s5_params(a_re, a_im, log_dt, b_re, b_im, c_re, c_im):
    dt = jnp.exp(log_dt.astype(F32))[:, None]
    A = lax.complex(a_re.astype(F32), a_im.astype(F32))
    A_bar = jnp.exp(dt * A)
    B_bar = ((A_bar - 1.0) / A)[..., None] * lax.complex(b_re.astype(F32), b_im.astype(F32))
    eye = jnp.eye(S5_GROUPS, dtype=F32)

    def bdiag_in(m):
        return jnp.einsum('gpc,gh->gchp', m, eye).reshape(S5_WIDTH, S5_X)

    def bdiag_out(m):
        return jnp.einsum('gcp,gh->gphc', m, eye).reshape(S5_X, S5_WIDTH)

    bcat = jnp.concatenate([bdiag_in(jnp.real(B_bar)), bdiag_in(jnp.imag(B_bar))], axis=1).astype(BF16)
    ccat = jnp.concatenate([bdiag_out(c_re.astype(F32)), -bdiag_out(c_im.astype(F32))], axis=0).astype(BF16)
    a_cat = jnp.concatenate([jnp.real(A_bar).reshape(1, S5_X), jnp.imag(A_bar).reshape(1, S5_X)], axis=1)
    return bcat, ccat, a_cat


def _overlap_matrix(n_rows, n_cmp, n_sel):
    cs = np.arange(n_rows)[:, None] * CMP_STRIDE
    ss = np.arange(n_sel)[None, :] * SEL_BLOCK
    ov = np.clip(np.minimum(cs + CMP_BLOCK, ss + SEL_BLOCK) - np.maximum(cs, ss), 0, None) / CMP_BLOCK
    ov[n_cmp:] = 0.0
    return jnp.asarray(ov, dtype=BF16)


def _sb_suffix_matrix(tk):
    u = (np.arange(tk)[:, None] >= np.arange(tk)[None, :]).astype(np.float32)
    return jnp.asarray(-np.concatenate([u, np.ones_like(u)], axis=0), dtype=BF16)


def kernel(x, norm_mix, w_in, nsa_q_gain, nsa_k_gain, cmp_pe, cmp_wk, cmp_wv, s5_a_re, s5_a_im, s5_log_dt,
           s5_b_re, s5_b_im, s5_c_re, s5_c_im, s5_d, s5_w_glu, w_br_nsa, w_br_s5, w_br_sb, w_out, norm_ffn,
           w_up, conv_w, conv_b, w_down):
    B, L, D = x.shape
    depth = w_in.shape[0]
    tq = 128
    tq_cmp = 256
    tl_in = min(512, L)
    tl_ffn = min(512, L)
    sb_tq, sb_tk = 512, 512
    s5_steps = min(256, L)
    assert L % 512 == 0 and L >= WINDOW + tq and D == 1024
    n_cmp = (L - CMP_BLOCK) // CMP_STRIDE + 1
    n_rows = L // CMP_STRIDE
    n_sel = L // SEL_BLOCK
    nq = L // tq
    nwords = (nq + 31) // 32
    ov = _overlap_matrix(n_rows, n_cmp, n_sel)
    u2 = _sb_suffix_matrix(sb_tk // 2)

    for i in range(depth):
        qg = _pad_gain(nsa_q_gain[i][None, :], 0.0)
        kg = _pad_gain(nsa_k_gain[i], 1.0)
        qn, ckv, skv, wkv, gates, u_s5, sbq, sbkv, mg = _inproj(
            x, norm_mix[i][None, :], _pack_w_in(w_in[i]), qg, kg[1:3], tl_in)

        pe = jnp.concatenate([cmp_pe[i], cmp_pe[i]], axis=-1).reshape(2, (CMP_BLOCK // 2) * LANES)
        wt, wb = _cmp_weights(cmp_wk[i, 0], cmp_wv[i, 0])
        kvc = _cmpkv(ckv.reshape(B, n_rows, CMP_STRIDE * LANES), pe, wt, wb, kg[0:1])
        ocmp, idx, flags = _nsa_cmp(qn, kvc, gates, ov, tq_cmp, tq, n_cmp)
        fl = jnp.pad(flags[:, :, 0, :nq], ((0, 0), (0, 0), (0, nwords * 32 - nq))).reshape(B, nq, nwords, 32)
        bits = jnp.sum(fl << jnp.arange(32, dtype=jnp.int32), axis=-1, dtype=jnp.int32).reshape(-1)
        o_a = _nsa_sw(bits, qn, skv, wkv, idx, gates, ocmp, tq, nwords)

        bcat, ccat, a_cat = _s5_params(s5_a_re[i], s5_a_im[i], s5_log_dt[i], s5_b_re[i], s5_b_im[i],
                                       s5_c_re[i], s5_c_im[i])
        o_b = _s5(u_s5, bcat, ccat, a_cat, s5_d[i].reshape(1, S5_WIDTH), s5_w_glu[i].astype(BF16), s5_steps)

        o_c = _sb(sbq, sbkv, u2, sb_tq, sb_tk)

        x = _merge_ffn(x, o_a, o_b, o_c, mg,
                       _pad_rows(w_br_nsa[i], NSA_HEADS), w_br_s5[i].astype(BF16),
                       _pad_rows(w_br_sb[i], SB_HEADS), w_out[i].astype(BF16), norm_ffn[i][None, :],
                       w_up[i].astype(BF16), conv_w[i], conv_b[i][None, :], w_down[i].astype(BF16), tl_ffn)
    return x
```

```python
import functools
import math

import numpy as np
import jax
import jax.numpy as jnp
from jax import lax
from jax.experimental import pallas as pl
from jax.experimental.pallas import tpu as pltpu

HEAD_DIM = 64
NSA_HEADS = 4
CMP_BLOCK = 32
CMP_STRIDE = 16
SEL_BLOCK = 32
SEL_TOPK = 4
WINDOW = 512
FORCE_SCORE = 1.0e4
S5_WIDTH = 256
S5_GROUP = 16
S5_GROUPS = S5_WIDTH // S5_GROUP
S5_STATE = 64
SB_HEADS = 4
NORM_EPS = 1e-6
NEG_INF = -1e30
LANES = 128
VMEM_LIMIT = 56 * 1024 * 1024

F32 = jnp.float32
BF16 = jnp.bfloat16


def _log2(n):
    assert n & (n - 1) == 0
    return n.bit_length() - 1


SEL_SHIFT = _log2(SEL_BLOCK)
LOG2E = math.log2(math.e)


def _dot(a, b):
    return jnp.dot(a, b, preferred_element_type=F32)


def _dot_nt(a, b):
    return lax.dot_general(a, b, (((1,), (1,)), ((), ())), preferred_element_type=F32)


def _const_spec(shape):
    nd = len(shape)
    return pl.BlockSpec(shape, lambda *_: (0,) * nd, pipeline_mode=pl.Buffered(1))


def _params(*sem):
    return pltpu.CompilerParams(dimension_semantics=sem, vmem_limit_bytes=VMEM_LIMIT)


def _gelu(x):
    return 0.5 * x * (1.0 + jnp.tanh(math.sqrt(2.0 / math.pi) * (x + 0.044715 * (x * x * x))))


def _sigmoid(x):
    return 1.0 / (1.0 + jnp.exp(-x))


def _split3(x):
    h1 = x.astype(BF16)
    r1 = x - h1.astype(F32)
    h2 = r1.astype(BF16)
    h3 = (r1 - h2.astype(F32)).astype(BF16)
    return h1, h2, h3


C_QN = 0
C_KV = 512
C_G = 896
C_S5 = 1024
C_SBQ = 1280
C_SBKV = 1792
C_MG = 2304
C_END = C_MG + 3072


def _inproj_kernel(x_ref, gain_ref, w_ref, qg_ref, kg_ref,
                   qn_ref, ckv_ref, skv_ref, wkv_ref, g_ref, s5_ref, sbq_ref, sbkv_ref, mg_ref):
    x = x_ref[0]
    h = x * lax.rsqrt(jnp.mean(x * x, axis=-1, keepdims=True) + NORM_EPS) * gain_ref[...]
    hb = h.astype(BF16)
    lane = lax.broadcasted_iota(jnp.int32, (1, LANES), 1)
    lo = lane < HEAD_DIM

    seg = _dot(hb, w_ref[:, C_QN:C_S5])
    scale = HEAD_DIM ** -0.5 * LOG2E
    for hd in range(NSA_HEADS):
        q = seg[:, hd * LANES:(hd + 1) * LANES]
        ms = jnp.sum(q * q, axis=-1, keepdims=True) * (1.0 / HEAD_DIM)
        qn = q * lax.rsqrt(ms + NORM_EPS) * qg_ref[...] * scale
        qn_ref[0, :, hd * LANES:(hd + 1) * LANES] = qn.astype(BF16)
    ckv_ref[0] = seg[:, C_KV:C_KV + LANES].astype(BF16)
    for n, ref in ((1, skv_ref), (2, wkv_ref)):
        kv = seg[:, C_KV + n * LANES:C_KV + (n + 1) * LANES]
        ms = jnp.sum(jnp.where(lo, kv * kv, 0.0), axis=-1, keepdims=True) * (1.0 / HEAD_DIM)
        sc = jnp.where(lo, lax.rsqrt(ms + NORM_EPS) * kg_ref[n - 1:n, :], 1.0)
        ref[0] = (kv * sc).astype(BF16)
    g_ref[0] = _sigmoid(seg[:, C_G:C_G + LANES])

    seg = _dot(hb, w_ref[:, C_S5:C_MG])
    s5_ref[0] = seg[:, 0:S5_WIDTH]
    sbq_ref[0] = (seg[:, C_SBQ - C_S5:C_SBKV - C_S5] * scale).astype(BF16)
    sbkv_ref[0] = seg[:, C_SBKV - C_S5:C_MG - C_S5].astype(BF16)

    for c in range(3):
        seg = _dot(hb, w_ref[:, C_MG + c * 1024:C_MG + (c + 1) * 1024])
        mg_ref[0, :, c * 1024:(c + 1) * 1024] = _sigmoid(seg).astype(BF16)


def _inproj(x, gain, w, qg, kg, tl):
    B, L, D = x.shape
    grid = (B, L // tl)
    row = lambda b, l: (b, l, 0)

    def out(n, dt):
        return jax.ShapeDtypeStruct((B, L, n), dt), pl.BlockSpec((1, tl, n), row)

    outs = [out(512, BF16), out(LANES, BF16), out(LANES, BF16), out(LANES, BF16), out(LANES, F32),
            out(S5_WIDTH, F32), out(512, BF16), out(512, BF16), out(3072, BF16)]
    return pl.pallas_call(
        _inproj_kernel,
        grid=grid,
        in_specs=[pl.BlockSpec((1, tl, D), row), _const_spec((1, D)), _const_spec((D, C_END)),
                  _const_spec((1, LANES)), _const_spec((2, LANES))],
        out_specs=[o[1] for o in outs],
        out_shape=[o[0] for o in outs],
        compiler_params=_params("parallel", "parallel"),
        name="inproj",
    )(x, gain, w, qg, kg)


def _cmpkv_kernel(x_ref, pe_ref, wt_ref, wb_ref, kg_ref, o_ref):
    x = x_ref[0].astype(F32)
    a = _dot((x + pe_ref[0:1, :]).astype(BF16), wt_ref[...])
    b = _dot((x + pe_ref[1:2, :]).astype(BF16), wb_ref[...])
    n = x.shape[0]
    kv = a + pltpu.roll(b, n - 1, 0)
    lane = lax.broadcasted_iota(jnp.int32, (1, LANES), 1)
    lo = lane < HEAD_DIM
    ms = jnp.sum(jnp.where(lo, kv * kv, 0.0), axis=-1, keepdims=True) * (1.0 / HEAD_DIM)
    sc = jnp.where(lo, lax.rsqrt(ms + NORM_EPS) * kg_ref[...], 1.0)
    rowi = lax.broadcasted_iota(jnp.int32, (n, 1), 0)
    o_ref[0] = jnp.where(rowi < n - 1, kv * sc, 0.0).astype(BF16)


def _cmpkv(x16, pe2, wt, wb, kg):
    B, n, w = x16.shape
    return pl.pallas_call(
        _cmpkv_kernel,
        grid=(B,),
        in_specs=[pl.BlockSpec((1, n, w), lambda b: (b, 0, 0)), _const_spec((2, w)),
                  _const_spec((w, LANES)), _const_spec((w, LANES)), _const_spec((1, LANES))],
        out_specs=pl.BlockSpec((1, n, LANES), lambda b: (b, 0, 0)),
        out_shape=jax.ShapeDtypeStruct((B, n, LANES), BF16),
        compiler_params=_params("parallel"),
        name="cmpkv",
    )(x16, pe2, wt, wb, kg)


def _slope(hd):
    return 2.0 ** (-8.0 * (hd + 1) / NSA_HEADS) * LOG2E


def _nsa_cmp_kernel(q_ref, kvc_ref, g_ref, ov_ref, ocmp_ref, idx_ref, flag_ref, *, tq, tsel, n_cmp):
    qi = pl.program_id(1)
    nc_all, ns_all = ov_ref.shape
    need = (qi + 1) * (tq // CMP_STRIDE)
    bounds = list(range(LANES, nc_all, LANES)) + [nc_all]
    for v, nc in enumerate(bounds):
        lo = bounds[v - 1] if v else 0
        ns = min(ns_all, max(LANES, nc // (SEL_BLOCK // CMP_STRIDE)))
        pl.when((need > lo) & (need <= nc))(functools.partial(
            _nsa_cmp_body, q_ref, kvc_ref, g_ref, ov_ref, ocmp_ref, idx_ref, flag_ref, qi * tq,
            tq=tq, tsel=tsel, n_cmp=n_cmp, nc=nc, ns=ns))


def _nsa_cmp_body(q_ref, kvc_ref, g_ref, ov_ref, ocmp_ref, idx_ref, flag_ref, t0, *, tq, tsel, n_cmp, nc, ns):
    kvc = kvc_ref[0, 0:nc, :]
    g = g_ref[0]
    col = lax.broadcasted_iota(jnp.int32, (tq, nc), 1)
    t = t0 + lax.broadcasted_iota(jnp.int32, (tq, nc), 0)
    dist = t - (col * CMP_STRIDE + (CMP_BLOCK - 1))
    ok = (dist >= 0) & (col < n_cmp)
    distf = dist.astype(F32)
    psum = jnp.zeros((tq, nc), F32)
    for hd in range(NSA_HEADS):
        q = q_ref[0, :, hd * LANES:(hd + 1) * LANES]
        s = _dot_nt(q, kvc)
        s = jnp.where(ok, s - _slope(hd) * distf, NEG_INF)
        e = jnp.exp2(s - jnp.max(s, axis=-1, keepdims=True))
        p = jnp.where(ok, e * (1.0 / jnp.sum(e, axis=-1, keepdims=True)), 0.0)
        psum = psum + p
        o = _dot(p.astype(BF16), kvc)
        ocmp_ref[0, :, hd * LANES:(hd + 1) * LANES] = o * g[:, 3 * hd:3 * hd + 1]

    ov = ov_ref[0:nc, 0:ns]
    h1, h2, h3 = _split3(psum)
    imp = _dot(h1, ov) + _dot(h2, ov) + _dot(h3, ov)
    scol =lax.broadcasted_iota(jnp.int32, (tq, ns), 1)
    scolf = scol.astype(F32)
    trow = t0 + lax.broadcasted_iota(jnp.int32, (tq, ns), 0)
    free = (scol * SEL_BLOCK <= trow) & (scol != 0) & (scol != (trow >> SEL_SHIFT))
    score = jnp.where(free, imp, -1.0)
    lane = lax.broadcasted_iota(jnp.int32, (tq, LANES), 1)
    idx_out = jnp.zeros((tq, LANES), jnp.int32)
    hit = jnp.zeros((tq, LANES), F32)
    for k in range(SEL_TOPK - 2):
        m = jnp.max(score, axis=-1, keepdims=True)
        ikf = jnp.min(jnp.where(score == m, scolf, 1e9), axis=-1, keepdims=True)
        score = jnp.where(scolf == ikf, -3e38, score)
        ik = ikf.astype(jnp.int32)
        idx_out = jnp.where(lane == k, ik, idx_out)
        hit = jnp.where(lane == (ik >> _log2(tsel // SEL_BLOCK)), 1.0, hit)
    idx_ref[0] = idx_out
    for sub in range(tq // tsel):
        flag_ref[0, sub] = jnp.max(hit[sub * tsel:(sub + 1) * tsel], axis=0, keepdims=True).astype(jnp.int32)


def _nsa_cmp(qn, kvc, gates, ov, tq, tsel, n_cmp):
    B, L, _ = qn.shape
    nq = L // tq
    nsub = tq // tsel
    nc = kvc.shape[1]
    row = lambda b, i: (b, i, 0)
    return pl.pallas_call(
        functools.partial(_nsa_cmp_kernel, tq=tq, tsel=tsel, n_cmp=n_cmp),
        grid=(B, nq),
        in_specs=[pl.BlockSpec((1, tq, 512), row), pl.BlockSpec((1, nc, LANES), lambda b, i: (b, 0, 0)),
                  pl.BlockSpec((1, tq, LANES), row), _const_spec(ov.shape)],
        out_specs=[pl.BlockSpec((1, tq, 512), row), pl.BlockSpec((1, tq, LANES), row),
                   pl.BlockSpec((1, nsub, 1, LANES), lambda b, i: (b, i, 0, 0))],
        out_shape=[jax.ShapeDtypeStruct((B, L, 512), F32), jax.ShapeDtypeStruct((B, L, LANES), jnp.int32),
                   jax.ShapeDtypeStruct((B, L // tsel, 1, LANES), jnp.int32)],
        compiler_params=_params("parallel", "parallel"),
        name="nsa_cmp",
    )(qn, kvc, gates, ov)


def _nsa_sw_kernel(bits_ref, q_ref, skv_ref, wkv_ref, idx_ref, g_ref, ocmp_ref, wb_ref, o_ref,
                   m_sc, l_sc, acc_sc, *, tq, nwords):
    b = pl.program_id(0)
    qi = pl.program_id(1)
    nq = pl.num_programs(1)
    t0 = qi * tq
    rows = NSA_HEADS * tq
    qs = jnp.concatenate([q_ref[0, :, hd * LANES:(hd + 1) * LANES] for hd in range(NSA_HEADS)], axis=0)
    rowi = lax.broadcasted_iota(jnp.int32, (rows, LANES), 0)
    ti = t0 + (rowi & (tq - 1))
    tf = ti.astype(F32)
    cur_blk = ti >> SEL_SHIFT
    head = rowi >> _log2(tq)
    slope = jnp.where(head == 0, _slope(0), jnp.where(head == 1, _slope(1),
                      jnp.where(head == 2, _slope(2), _slope(3))))
    idx4 = jnp.concatenate([idx_ref[0]] * NSA_HEADS, axis=0)
    picks = [jnp.broadcast_to(idx4[:, k:k + 1], (rows, LANES)) for k in range(SEL_TOPK - 2)]
    lane_k = lax.broadcasted_iota(jnp.int32, (1, tq), 1)

    def scores(j, kv, forced_blocks, valid):
        kpos = j * tq + lane_k
        blk = kpos >> SEL_SHIFT
        distf = tf - kpos.astype(F32)
        msk = (blk == picks[0]) | (blk == picks[1])
        if forced_blocks:
            min_dist = jnp.where(valid, 0.0, 3e38)
            msk = (msk | (blk == 0) | (blk == cur_blk)) & (distf >= min_dist)
        s = jnp.where(msk, _dot_nt(qs, kv) - slope * distf, NEG_INF)
        return s, msk

    jprev = jnp.maximum(qi - 1, 0)
    tiles = [(0, skv_ref[0, 0:tq, :], qi >= 0), (jprev, skv_ref[0, pl.ds(pl.multiple_of(jprev * tq, tq), tq), :], qi >= 2),
             (qi, skv_ref[0, pl.ds(pl.multiple_of(qi * tq, tq), tq), :], qi >= 1)]
    sm = [scores(j, kv, True, valid) for j, kv, valid in tiles]
    m0 = jnp.maximum(jnp.maximum(jnp.max(sm[0][0], axis=-1, keepdims=True),
                                 jnp.max(sm[1][0], axis=-1, keepdims=True)),
                     jnp.max(sm[2][0], axis=-1, keepdims=True))
    l0 = jnp.zeros((rows, 1), F32)
    acc0 = jnp.zeros((rows, LANES), F32)
    for (s, msk), (_, kv, _) in zip(sm, tiles):
        p = jnp.where(msk, jnp.exp2(s - m0), 0.0)
        l0 = l0 + jnp.sum(p, axis=-1, keepdims=True)
        acc0 = acc0 + _dot(p.astype(BF16), kv)
    m_sc[...] = jnp.broadcast_to(m0, (rows, LANES))
    l_sc[...] = jnp.broadcast_to(l0, (rows, LANES))
    acc_sc[...] = acc0

    def sel_step(j, carry):
        word = bits_ref[(b * nq + qi) * nwords + (j >> 5)]
        needed = (lax.shift_right_logical(word, j & 31) & 1) == 1

        @pl.when(needed)
        def _():
            kv = skv_ref[0, pl.ds(pl.multiple_of(j * tq, tq), tq), :]
            s, msk = scores(j, kv, False, None)
            m_old = m_sc[...]
            m_new = jnp.maximum(m_old, jnp.max(s, axis=-1, keepdims=True))
            p = jnp.where(msk, jnp.exp2(s - m_new), 0.0)
            alpha = jnp.exp2(m_old - m_new)
            l_sc[...] = alpha * l_sc[...] + jnp.sum(p, axis=-1, keepdims=True)
            acc_sc[...] = alpha * acc_sc[...] + _dot(p.astype(BF16), kv)
            m_sc[...] = m_new
        return carry

    lax.fori_loop(1, jnp.maximum(qi - 1, 1), sel_step, 0)
    o_sel = acc_sc[...] / l_sc[...]

    nband = WINDOW + tq
    start = pl.multiple_of(jnp.maximum(t0 - WINDOW, 0), tq)
    band = wkv_ref[0, pl.ds(start, nband), :]
    shift = (WINDOW - jnp.minimum(t0, WINDOW)) // LANES
    bias = jnp.concatenate([wb_ref[shift + k] for k in range(nband // LANES)], axis=-1)
    s = _dot_nt(qs, band) + bias
    e = jnp.exp2(s - jnp.max(s, axis=-1, keepdims=True))
    o_win = _dot(e.astype(BF16), band) * (1.0 / jnp.sum(e, axis=-1, keepdims=True))

    g = g_ref[0]
    lane = lax.broadcasted_iota(jnp.int32, (1, LANES), 1)
    for hd in range(NSA_HEADS):
        r = slice(hd * tq, (hd + 1) * tq)
        o = (ocmp_ref[0, :, hd * LANES:(hd + 1) * LANES]
             + g[:, 3 * hd + 1:3 * hd + 2] * o_sel[r]
             + g[:, 3 * hd + 2:3 * hd + 3] * o_win[r])
        o_ref[0, :, hd * LANES:(hd + 1) * LANES] = jnp.where(lane >= HEAD_DIM, o, 0.0).astype(BF16)


def _window_bias(tq):
    ncol = WINDOW + tq + WINDOW
    dist = np.arange(tq)[:, None] + WINDOW - np.arange(ncol)[None, :]
    ok = (dist >= 0) & (dist < WINDOW)
    tab = np.concatenate([np.where(ok, -_slope(hd) * dist, NEG_INF) for hd in range(NSA_HEADS)], axis=0)
    return jnp.asarray(tab.reshape(NSA_HEADS * tq, ncol // LANES, LANES).transpose(1, 0, 2), dtype=F32)


def _nsa_sw(bits, qn, skv, wkv, idx, gates, ocmp, wbias, tq, nwords):
    B, L, _ = qn.shape
    nq = L // tq
    row = lambda b, i, s: (b, i, 0)
    full = lambda b, i, s: (b, 0, 0)
    rows = NSA_HEADS * tq
    return pl.pallas_call(
        functools.partial(_nsa_sw_kernel, tq=tq, nwords=nwords),
        grid_spec=pltpu.PrefetchScalarGridSpec(
            num_scalar_prefetch=1,
            grid=(B, nq),
            in_specs=[pl.BlockSpec((1, tq, 512), row), pl.BlockSpec((1, L, LANES), full),
                      pl.BlockSpec((1, L, LANES), full), pl.BlockSpec((1, tq, LANES), row),
                      pl.BlockSpec((1, tq, LANES), row), pl.BlockSpec((1, tq, 512), row),
                      _const_spec(wbias.shape)],
            out_specs=pl.BlockSpec((1, tq, 512), row),
            scratch_shapes=[pltpu.VMEM((rows, LANES), F32), pltpu.VMEM((rows, LANES), F32),
                            pltpu.VMEM((rows, LANES), F32)],
        ),
        out_shape=jax.ShapeDtypeStruct((B, L, 512), BF16),
        compiler_params=_params("parallel", "arbitrary"),
        name="nsa_sw",
    )(bits, qn, skv, wkv, idx, gates, ocmp, wbias)


S5_X = S5_GROUPS * S5_STATE
S5_PAD = 8


def _s5_kernel(u_ref, bcat_ref, ccat_ref, a_ref, d_ref, wglu_ref, o_ref, x_sc, st_sc, *, steps, nb, nch):
    @pl.when(pl.program_id(0) == 0)
    def _():
        st_sc[...] = jnp.zeros_like(st_sc)

    stride = steps + S5_PAD
    ng = S5_X // LANES
    for b in range(nb):
        bu = _dot(u_ref[b].astype(BF16), bcat_ref[...])
        for k in range(2 * ng):
            x_sc[k, b * stride:b * stride + steps, :] = bu[:, k * LANES:(k + 1) * LANES]
    gpc = ng // nch
    for c in range(nch):
        ks = list(range(c * gpc, (c + 1) * gpc))
        ar = [jnp.broadcast_to(a_ref[0:1, k * LANES:(k + 1) * LANES], (nb, LANES)) for k in ks]
        ai = [jnp.broadcast_to(a_ref[0:1, S5_X + k * LANES:S5_X + (k + 1) * LANES], (nb, LANES)) for k in ks]

        def body(t, carry):
            xr, xi = carry
            r = pl.ds(t, nb, stride=stride)
            nr, ni = [], []
            for n, k in enumerate(ks):
                vr = ar[n] * xr[n] - ai[n] * xi[n] + x_sc[k, r, :]
                vi = ar[n] * xi[n] + ai[n] * xr[n] + x_sc[ng + k, r, :]
                x_sc[k, r, :] = vr
                x_sc[ng + k, r, :] = vi
                nr.append(vr)
                ni.append(vi)
            return tuple(nr), tuple(ni)

        init = (tuple(st_sc[:, k * LANES:(k + 1) * LANES] for k in ks),
                tuple(st_sc[:, S5_X + k * LANES:S5_X + (k + 1) * LANES] for k in ks))
        xr, xi = lax.fori_loop(0, steps, body, init, unroll=8)
        for n, k in enumerate(ks):
            st_sc[:, k * LANES:(k + 1) * LANES] = xr[n]
            st_sc[:, S5_X + k * LANES:S5_X + (k + 1) * LANES] = xi[n]

    for b in range(nb):
        xs = jnp.concatenate([x_sc[k, b * stride:b * stride + steps, :] for k in range(2 * ng)], axis=-1)
        y = _dot(xs.astype(BF16), ccat_ref[...]) + d_ref[...] * u_ref[b]
        y = _gelu(y)
        o_ref[b] = (y * _sigmoid(_dot(y.astype(BF16), wglu_ref[...]))).astype(BF16)


def _s5(u, bcat, ccat, a_cat, d, wglu, steps):
    nb, L, _ = u.shape
    return pl.pallas_call(
        functools.partial(_s5_kernel, steps=steps, nb=nb, nch=2),
        grid=(L // steps,),
        in_specs=[pl.BlockSpec((nb, steps, S5_WIDTH), lambda i: (0, i, 0)), _const_spec(bcat.shape),
                  _const_spec(ccat.shape), _const_spec(a_cat.shape), _const_spec(d.shape),
                  _const_spec(wglu.shape)],
        out_specs=pl.BlockSpec((nb, steps, S5_WIDTH), lambda i: (0, i, 0)),
        out_shape=jax.ShapeDtypeStruct((nb, L, S5_WIDTH), BF16),
        scratch_shapes=[pltpu.VMEM((2 * S5_X // LANES, nb * (steps + S5_PAD), LANES), F32),
                        pltpu.VMEM((nb, 2 * S5_X), F32)],
        compiler_params=_params("arbitrary"),
        name="s5",
    )(u, bcat, ccat, a_cat, d, wglu)


SIGN_BIT = np.int32(-2 ** 31)
SB_STOP_LOG2 = 170.0


def _sb_softplus2(z):
    neg_abs = lax.bitcast_convert_type(lax.bitcast_convert_type(z, jnp.int32) | SIGN_BIT, F32)
    return jnp.maximum(z, 0.0) + jnp.log2(1.0 + jnp.exp2(neg_abs))


def _sb_suffix(spb, tri_ref):
    half = tri_ref.shape[1]
    return jnp.concatenate([_dot(spb, tri_ref[...]), _dot(spb[:, half:], tri_ref[0:half, :])], axis=-1)


def _sb_kernel(q_ref, kv_ref, tri1_ref, tri_ref, o_ref, acc_sc, c_sc, z_sc, w_sc, *, tq, tk1, tk):
    i = pl.program_id(2)
    q = q_ref[0]
    start = jnp.maximum((i + 1) * tq - tk1, 0)

    def kv_block(j):
        return kv_ref[0, pl.ds(pl.multiple_of(j * tk, tk), tk), :]

    kv1 = kv_ref[0, pl.ds(pl.multiple_of(start, tk), tk1), :]
    z = _dot_nt(q, kv1)
    t = i * tq + lax.broadcasted_iota(jnp.int32, (tq, tk1), 0)
    causal = (start + lax.broadcasted_iota(jnp.int32, (tq, tk1), 1)) < t
    spb = jnp.where(causal, _sb_softplus2(z), 0.0).astype(BF16)
    r = _sb_suffix(spb, tri1_ref)
    w = jnp.where(causal, jnp.exp2(jnp.minimum(z + r, 0.0)), 0.0)
    acc_sc[...] = _dot(w.astype(BF16), kv1)
    c_sc[...] = jnp.broadcast_to(r[:, 0:1], (tq, LANES))
    j0 = start // tk - 1
    z_sc[...] = _dot_nt(q, kv_block(jnp.maximum(j0, 0)))
    w_sc[...] = jnp.zeros_like(w_sc)

    def step(j):
        z = z_sc[...]
        z_next = _dot_nt(q, kv_block(jnp.maximum(j - 1, 0)))
        acc_sc[...] += _dot(w_sc[...], kv_block(j + 1))
        r = _sb_suffix(_sb_softplus2(z).astype(BF16), tri_ref)
        c = c_sc[...]
        cw = jnp.concatenate([c] * (tk // LANES), axis=-1)
        w_sc[...] = jnp.exp2(jnp.minimum(z + r + cw, 0.0)).astype(BF16)
        z_sc[...] = z_next
        c_sc[...] = c + r[:, 0:1]

    def cond(carry):
        j, cmax = carry
        return (j >= 0) & (cmax > -SB_STOP_LOG2)

    def body(carry):
        j, _ = carry
        step(j)
        return j - 1, jnp.max(c_sc[...])

    j_next, _ = lax.while_loop(cond, body, (j0, jnp.max(c_sc[...])))

    @pl.when(j_next < j0)
    def _():
        acc_sc[...] += _dot(w_sc[...], kv_block(j_next + 1))

    lane = lax.broadcasted_iota(jnp.int32, (1, LANES), 1)
    o_ref[0] = jnp.where(lane >= HEAD_DIM, acc_sc[...], 0.0).astype(BF16)


def _sb(sbq, sbkv, tri1, tri, tq, tk1, tk):
    B, L, _ = sbq.shape
    assert tq % tk == 0 and tk1 % tk == 0 and L >= tk1
    return pl.pallas_call(
        functools.partial(_sb_kernel, tq=tq, tk1=tk1, tk=tk),
        grid=(B, SB_HEADS, L // tq),
        in_specs=[pl.BlockSpec((1, tq, LANES), lambda b, h, i: (b, i, h)),
                  pl.BlockSpec((1, L, LANES), lambda b, h, i: (b, 0, h)),
                  _const_spec(tri1.shape), _const_spec(tri.shape)],
        out_specs=pl.BlockSpec((1, tq, LANES), lambda b, h, i: (b, i, h)),
        out_shape=jax.ShapeDtypeStruct((B, L, SB_HEADS * LANES), BF16),
        scratch_shapes=[pltpu.VMEM((tq, LANES), F32), pltpu.VMEM((tq, LANES), F32),
                        pltpu.VMEM((tq, tk), F32), pltpu.VMEM((tq, tk), BF16)],
        compiler_params=_params("parallel", "parallel", "arbitrary"),
        name="sb_attn",
    )(sbq, sbkv, tri1, tri)


FFN_CHUNK = 1024
HALO = 8


def _merge_ffn_kernel(x_ref, oa_ref, ob_ref, oc_ref, mg_ref, wa_ref, wb_ref, wc_ref, wout_ref, gain_ref,
                      wup_ref, cw_ref, cb_ref, wdown_ref, out_ref, ubuf, carry, *, tl, dff):
    @pl.when(pl.program_id(1) == 0)
    def _():
        carry[...] = jnp.zeros_like(carry)

    d = x_ref.shape[2]
    merged = (mg_ref[0, :, 0:d].astype(F32) * _dot(oa_ref[0], wa_ref[...])
              + mg_ref[0, :, d:2 * d].astype(F32) * _dot(ob_ref[0], wb_ref[...])
              + mg_ref[0, :, 2 * d:3 * d].astype(F32) * _dot(oc_ref[0], wc_ref[...]))
    x1 = x_ref[0] + _dot(merged.astype(BF16), wout_ref[...])
    h = x1 * lax.rsqrt(jnp.mean(x1 * x1, axis=-1, keepdims=True) + NORM_EPS) * gain_ref[...]
    hb = h.astype(BF16)

    def conv_cols(off):
        cols = slice(off, off + FFN_CHUNK)
        u = _dot(hb, wup_ref[:, cols])
        ubuf[0:HALO, :] = carry[:, cols]
        ubuf[HALO:HALO + tl, :] = u
        carry[:, cols] = u[tl - HALO:tl, :]
        return (cw_ref[2:3, cols] * u + cw_ref[1:2, cols] * ubuf[HALO - 1:HALO - 1 + tl, :]
                + cw_ref[0:1, cols] * ubuf[HALO - 2:HALO - 2 + tl, :] + cb_ref[:, cols])

    acc = x1
    for j in range(dff // FFN_CHUNK):
        gate = conv_cols(j * FFN_CHUNK)
        val = conv_cols(dff + j * FFN_CHUNK)
        act = (_gelu(gate) * val).astype(BF16)
        acc = acc + _dot(act, wdown_ref[j * FFN_CHUNK:(j + 1) * FFN_CHUNK, :])
    out_ref[0] = acc


def _merge_ffn(x, oa, ob, oc, mg, wa, wb, wc, wout, gain, wup, cw, cb, wdown, tl):
    B, L, D = x.shape
    dff = wdown.shape[0]
    row = lambda b, l: (b, l, 0)
    weights = [wa, wb, wc, wout, gain, wup, cw, cb, wdown]
    return pl.pallas_call(
        functools.partial(_merge_ffn_kernel, tl=tl, dff=dff),
        grid=(B, L // tl),
        in_specs=[pl.BlockSpec((1, tl, D), row), pl.BlockSpec((1, tl, oa.shape[2]), row),
                  pl.BlockSpec((1, tl, ob.shape[2]), row), pl.BlockSpec((1, tl, oc.shape[2]), row),
                  pl.BlockSpec((1, tl, mg.shape[2]), row)] + [_const_spec(w.shape) for w in weights],
        out_specs=pl.BlockSpec((1, tl, D), row),
        out_shape=jax.ShapeDtypeStruct((B, L, D), F32),
        scratch_shapes=[pltpu.VMEM((HALO + tl, FFN_CHUNK), F32), pltpu.VMEM((HALO, 2 * dff), F32)],
        compiler_params=_params("parallel", "arbitrary"),
        name="merge_ffn",
    )(x, oa, ob, oc, mg, *weights)


def _pad_heads(w, n_heads):
    d = w.shape[0]
    w = w.reshape(d, n_heads, HEAD_DIM)
    return jnp.concatenate([w, jnp.zeros_like(w)], axis=-1).reshape(d, n_heads * LANES)


def _pack_w_in(w):
    d = w.shape[0]
    o_g = 640
    o_s5 = o_g + NSA_HEADS * 3
    o_sb = o_s5 + S5_WIDTH
    o_mg = o_sb + 3 * SB_HEADS * HEAD_DIM
    hw = SB_HEADS * HEAD_DIM
    gates = jnp.pad(w[:, o_g:o_s5], ((0, 0), (0, LANES - NSA_HEADS * 3)))
    sbk = w[:, o_sb + hw:o_sb + 2 * hw].reshape(d, SB_HEADS, 1, HEAD_DIM)
    sbv = w[:, o_sb + 2 * hw:o_mg].reshape(d, SB_HEADS, 1, HEAD_DIM)
    sbkv = jnp.concatenate([sbk, sbv], axis=2).reshape(d, SB_HEADS * LANES)
    packed = jnp.concatenate([_pad_heads(w[:, 0:256], NSA_HEADS), w[:, 256:o_g], gates, w[:, o_s5:o_sb],
                              _pad_heads(w[:, o_sb:o_sb + hw], SB_HEADS), sbkv, w[:, o_mg:]], axis=1)
    return packed.astype(BF16)


def _pad_rows(w, n_heads):
    d = w.shape[1]
    w = w.reshape(n_heads, HEAD_DIM, d)
    return jnp.concatenate([jnp.zeros_like(w), w], axis=1).reshape(n_heads * LANES, d).astype(BF16)


def _pad_gain(g, fill):
    return jnp.concatenate([g, jnp.full_like(g, fill)], axis=-1)


def _cmp_weights(wk, wv):
    z = jnp.zeros_like(wk)
    wkv = jnp.concatenate([jnp.concatenate([wk, z], axis=2), jnp.concatenate([z, wv], axis=2)], axis=1)
    half = CMP_BLOCK // 2
    wt = wkv[:half].reshape(half * LANES, LANES)
    wb = wkv[half:].reshape(half * LANES, LANES)
    return wt.astype(BF16), wb.astype(BF16)


def _s5_params(a_re, a_im, log_dt, b_re, b_im, c_re, c_im):
    dt = jnp.exp(log_dt.astype(F32))[:, None]
    A = lax.complex(a_re.astype(F32), a_im.astype(F32))
    A_bar = jnp.exp(dt * A)
    B_bar = ((A_bar - 1.0) / A)[..., None] * lax.complex(b_re.astype(F32), b_im.astype(F32))
    eye = jnp.eye(S5_GROUPS, dtype=F32)

    def bdiag_in(m):
        return jnp.einsum('gpc,gh->gchp', m, eye).reshape(S5_WIDTH, S5_X)

    def bdiag_out(m):
        return jnp.einsum('gcp,gh->gphc', m, eye).reshape(S5_X, S5_WIDTH)

    bcat = jnp.concatenate([bdiag_in(jnp.real(B_bar)), bdiag_in(jnp.imag(B_bar))], axis=1).astype(BF16)
    ccat = jnp.concatenate([bdiag_out(c_re.astype(F32)), -bdiag_out(c_im.astype(F32))], axis=0).astype(BF16)
    a_cat = jnp.concatenate([jnp.real(A_bar).reshape(1, S5_X), jnp.imag(A_bar).reshape(1, S5_X)], axis=1)
    return bcat, ccat, a_cat


def _overlap_matrix(n_rows, n_cmp, n_sel):
    cs = np.arange(n_rows)[:, None] * CMP_STRIDE
    ss = np.arange(n_sel)[None, :] * SEL_BLOCK
    ov = np.clip(np.minimum(cs + CMP_BLOCK, ss + SEL_BLOCK) - np.maximum(cs, ss), 0, None) / CMP_BLOCK
    ov[n_cmp:] = 0.0
    return jnp.asarray(ov, dtype=BF16)


def _sb_suffix_matrix(tk):
    u = (np.arange(tk)[:, None] >= np.arange(tk)[None, :]).astype(np.float32)
    return jnp.asarray(-np.concatenate([u, np.ones_like(u)], axis=0), dtype=BF16)


def kernel(x, norm_mix, w_in, nsa_q_gain, nsa_k_gain, cmp_pe, cmp_wk, cmp_wv, s5_a_re, s5_a_im, s5_log_dt,
           s5_b_re, s5_b_im, s5_c_re, s5_c_im, s5_d, s5_w_glu, w_br_nsa, w_br_s5, w_br_sb, w_out, norm_ffn,
           w_up, conv_w, conv_b, w_down):
    B, L, D = x.shape
    depth = w_in.shape[0]
    tq = 128
    tq_cmp = 256
    tl_in = min(512, L)
    tl_ffn = min(512, L)
    sb_tq, sb_tk1, sb_tk = 256, 512, 256
    s5_steps = min(256, L)
    assert L % 512 == 0 and L >= WINDOW + tq and D == 1024
    n_cmp = (L - CMP_BLOCK) // CMP_STRIDE + 1
    n_rows = L // CMP_STRIDE
    n_sel = L // SEL_BLOCK
    nq = L // tq
    nwords = (nq + 31) // 32
    ov = _overlap_matrix(n_rows, n_cmp, n_sel)
    tri1, tri = _sb_suffix_matrix(sb_tk1 // 2), _sb_suffix_matrix(sb_tk // 2)
    wbias = _window_bias(tq)

    for i in range(depth):
        qg = _pad_gain(nsa_q_gain[i][None, :], 0.0)
        kg = _pad_gain(nsa_k_gain[i], 1.0)
        qn, ckv, skv, wkv, gates, u_s5, sbq, sbkv, mg = _inproj(
            x, norm_mix[i][None, :], _pack_w_in(w_in[i]), qg, kg[1:3], tl_in)

        pe = jnp.concatenate([cmp_pe[i], cmp_pe[i]], axis=-1).reshape(2, (CMP_BLOCK // 2) * LANES)
        wt, wb = _cmp_weights(cmp_wk[i, 0], cmp_wv[i, 0])
        kvc = _cmpkv(ckv.reshape(B, n_rows, CMP_STRIDE * LANES), pe, wt, wb, kg[0:1])
        ocmp, idx, flags = _nsa_cmp(qn, kvc, gates, ov, tq_cmp, tq, n_cmp)
        fl = jnp.pad(flags[:, :, 0, :nq], ((0, 0), (0, 0), (0, nwords * 32 - nq))).reshape(B, nq, nwords, 32)
        bits = jnp.sum(fl << jnp.arange(32, dtype=jnp.int32), axis=-1, dtype=jnp.int32).reshape(-1)
        o_a = _nsa_sw(bits, qn, skv, wkv, idx, gates, ocmp, wbias, tq, nwords)

        bcat, ccat, a_cat = _s5_params(s5_a_re[i], s5_a_im[i], s5_log_dt[i], s5_b_re[i], s5_b_im[i],
                                       s5_c_re[i], s5_c_im[i])
        o_b = _s5(u_s5, bcat, ccat, a_cat, s5_d[i].reshape(1, S5_WIDTH), s5_w_glu[i].astype(BF16), s5_steps)

        o_c = _sb(sbq, sbkv, tri1, tri, sb_tq, sb_tk1, sb_tk)

        x = _merge_ffn(x, o_a, o_b, o_c, mg,
                       _pad_rows(w_br_nsa[i], NSA_HEADS), w_br_s5[i].astype(BF16),
                       _pad_rows(w_br_sb[i], SB_HEADS), w_out[i].astype(BF16), norm_ffn[i][None, :],
                       w_up[i].astype(BF16), conv_w[i], conv_b[i][None, :], w_down[i].astype(BF16), tl_ffn)
    return x
```

```python
import functools
import math

import numpy as np
import jax
import jax.numpy as jnp
from jax import lax
from jax.experimental import pallas as pl
from jax.experimental.pallas import tpu as pltpu

HEAD_DIM = 64
NSA_HEADS = 4
CMP_BLOCK = 32
CMP_STRIDE = 16
SEL_BLOCK = 32
SEL_TOPK = 4
WINDOW = 512
FORCE_SCORE = 1.0e4
S5_WIDTH = 256
S5_GROUP = 16
S5_GROUPS = S5_WIDTH // S5_GROUP
S5_STATE = 64
SB_HEADS = 4
NORM_EPS = 1e-6
NEG_INF = -1e30
LANES = 128
VMEM_LIMIT = 56 * 1024 * 1024

F32 = jnp.float32
BF16 = jnp.bfloat16


def _log2(n):
    assert n & (n - 1) == 0
    return n.bit_length() - 1


SEL_SHIFT = _log2(SEL_BLOCK)
LOG2E = math.log2(math.e)


def _dot(a, b):
    return jnp.dot(a, b, preferred_element_type=F32)


def _dot_nt(a, b):
    return lax.dot_general(a, b, (((1,), (1,)), ((), ())), preferred_element_type=F32)


def _const_spec(shape):
    nd = len(shape)
    return pl.BlockSpec(shape, lambda *_: (0,) * nd, pipeline_mode=pl.Buffered(1))


def _params(*sem):
    return pltpu.CompilerParams(dimension_semantics=sem, vmem_limit_bytes=VMEM_LIMIT)


def _gelu(x):
    return 0.5 * x * (1.0 + jnp.tanh(math.sqrt(2.0 / math.pi) * (x + 0.044715 * (x * x * x))))


def _sigmoid(x):
    return 1.0 / (1.0 + jnp.exp(-x))


def _split3(x):
    h1 = x.astype(BF16)
    r1 = x - h1.astype(F32)
    h2 = r1.astype(BF16)
    h3 = (r1 - h2.astype(F32)).astype(BF16)
    return h1, h2, h3


C_QN = 0
C_KV = 512
C_G = 896
C_S5 = 1024
C_SBQ = 1280
C_SBKV = 1792
C_MG = 2304
C_END = C_MG + 3072


def _inproj_kernel(x_ref, gain_ref, w_ref, qg_ref, kg_ref,
                   qn_ref, ckv_ref, skv_ref, wkv_ref, g_ref, s5_ref, sbq_ref, sbkv_ref, mg_ref):
    x = x_ref[0]
    h = x * lax.rsqrt(jnp.mean(x * x, axis=-1, keepdims=True) + NORM_EPS) * gain_ref[...]
    hb = h.astype(BF16)
    lane = lax.broadcasted_iota(jnp.int32, (1, LANES), 1)
    lo = lane < HEAD_DIM

    seg = _dot(hb, w_ref[:, C_QN:C_S5])
    scale = HEAD_DIM ** -0.5 * LOG2E
    for hd in range(NSA_HEADS):
        q = seg[:, hd * LANES:(hd + 1) * LANES]
        ms = jnp.sum(q * q, axis=-1, keepdims=True) * (1.0 / HEAD_DIM)
        qn = q * lax.rsqrt(ms + NORM_EPS) * qg_ref[...] * scale
        qn_ref[0, :, hd * LANES:(hd + 1) * LANES] = qn.astype(BF16)
    ckv_ref[0] = seg[:, C_KV:C_KV + LANES].astype(BF16)
    for n, ref in ((1, skv_ref), (2, wkv_ref)):
        kv = seg[:, C_KV + n * LANES:C_KV + (n + 1) * LANES]
        ms = jnp.sum(jnp.where(lo, kv * kv, 0.0), axis=-1, keepdims=True) * (1.0 / HEAD_DIM)
        sc = jnp.where(lo, lax.rsqrt(ms + NORM_EPS) * kg_ref[n - 1:n, :], 1.0)
        ref[0] = (kv * sc).astype(BF16)
    g_ref[0] = _sigmoid(seg[:, C_G:C_G + LANES])

    seg = _dot(hb, w_ref[:, C_S5:C_MG])
    s5_ref[0] = seg[:, 0:S5_WIDTH]
    sbq_ref[0] = (seg[:, C_SBQ - C_S5:C_SBKV - C_S5] * scale).astype(BF16)
    sbkv_ref[0] = seg[:, C_SBKV - C_S5:C_MG - C_S5].astype(BF16)

    for c in range(3):
        seg = _dot(hb, w_ref[:, C_MG + c * 1024:C_MG + (c + 1) * 1024])
        mg_ref[0, :, c * 1024:(c + 1) * 1024] = _sigmoid(seg).astype(BF16)


def _inproj(x, gain, w, qg, kg, tl):
    B, L, D = x.shape
    grid = (B, L // tl)
    row = lambda b, l: (b, l, 0)

    def out(n, dt):
        return jax.ShapeDtypeStruct((B, L, n), dt), pl.BlockSpec((1, tl, n), row)

    outs = [out(512, BF16), out(LANES, BF16), out(LANES, BF16), out(LANES, BF16), out(LANES, F32),
            out(S5_WIDTH, F32), out(512, BF16), out(512, BF16), out(3072, BF16)]
    return pl.pallas_call(
        _inproj_kernel,
        grid=grid,
        in_specs=[pl.BlockSpec((1, tl, D), row), _const_spec((1, D)), _const_spec((D, C_END)),
                  _const_spec((1, LANES)), _const_spec((2, LANES))],
        out_specs=[o[1] for o in outs],
        out_shape=[o[0] for o in outs],
        compiler_params=_params("parallel", "parallel"),
        name="inproj",
    )(x, gain, w, qg, kg)


def _cmpkv_kernel(x_ref, pe_ref, wt_ref, wb_ref, kg_ref, o_ref):
    x = x_ref[0].astype(F32)
    a = _dot((x + pe_ref[0:1, :]).astype(BF16), wt_ref[...])
    b = _dot((x + pe_ref[1:2, :]).astype(BF16), wb_ref[...])
    n = x.shape[0]
    kv = a + pltpu.roll(b, n - 1, 0)
    lane = lax.broadcasted_iota(jnp.int32, (1, LANES), 1)
    lo = lane < HEAD_DIM
    ms = jnp.sum(jnp.where(lo, kv * kv, 0.0), axis=-1, keepdims=True) * (1.0 / HEAD_DIM)
    sc = jnp.where(lo, lax.rsqrt(ms + NORM_EPS) * kg_ref[...], 1.0)
    rowi = lax.broadcasted_iota(jnp.int32, (n, 1), 0)
    o_ref[0] = jnp.where(rowi < n - 1, kv * sc, 0.0).astype(BF16)


def _cmpkv(x16, pe2, wt, wb, kg):
    B, n, w = x16.shape
    return pl.pallas_call(
        _cmpkv_kernel,
        grid=(B,),
        in_specs=[pl.BlockSpec((1, n, w), lambda b: (b, 0, 0)), _const_spec((2, w)),
                  _const_spec((w, LANES)), _const_spec((w, LANES)), _const_spec((1, LANES))],
        out_specs=pl.BlockSpec((1, n, LANES), lambda b: (b, 0, 0)),
        out_shape=jax.ShapeDtypeStruct((B, n, LANES), BF16),
        compiler_params=_params("parallel"),
        name="cmpkv",
    )(x16, pe2, wt, wb, kg)


def _slope(hd):
    return 2.0 ** (-8.0 * (hd + 1) / NSA_HEADS) * LOG2E


def _nsa_cmp_kernel(q_ref, kvc_ref, g_ref, ov_ref, ocmp_ref, idx_ref, flag_ref, *, tq, tsel, n_cmp):
    qi = pl.program_id(1)
    nc_all, ns_all = ov_ref.shape
    need = (qi + 1) * (tq // CMP_STRIDE)
    bounds = list(range(LANES, nc_all, LANES)) + [nc_all]
    for v, nc in enumerate(bounds):
        lo = bounds[v - 1] if v else 0
        ns = min(ns_all, max(LANES, nc // (SEL_BLOCK // CMP_STRIDE)))
        pl.when((need > lo) & (need <= nc))(functools.partial(
            _nsa_cmp_body, q_ref, kvc_ref, g_ref, ov_ref, ocmp_ref, idx_ref, flag_ref, qi * tq,
            tq=tq, tsel=tsel, n_cmp=n_cmp, nc=nc, ns=ns))


def _nsa_cmp_body(q_ref, kvc_ref, g_ref, ov_ref, ocmp_ref, idx_ref, flag_ref, t0, *, tq, tsel, n_cmp, nc, ns):
    kvc = kvc_ref[0, 0:nc, :]
    g = g_ref[0]
    col = lax.broadcasted_iota(jnp.int32, (tq, nc), 1)
    t = t0 + lax.broadcasted_iota(jnp.int32, (tq, nc), 0)
    dist = t - (col * CMP_STRIDE + (CMP_BLOCK - 1))
    ok = (dist >= 0) & (col < n_cmp)
    distf = dist.astype(F32)
    psum = jnp.zeros((tq, nc), F32)
    for hd in range(NSA_HEADS):
        q = q_ref[0, :, hd * LANES:(hd + 1) * LANES]
        s = _dot_nt(q, kvc)
        s = jnp.where(ok, s - _slope(hd) * distf, NEG_INF)
        e = jnp.exp2(s - jnp.max(s, axis=-1, keepdims=True))
        p = jnp.where(ok, e * (1.0 / jnp.sum(e, axis=-1, keepdims=True)), 0.0)
        psum = psum + p
        o = _dot(p.astype(BF16), kvc)
        ocmp_ref[0, :, hd * LANES:(hd + 1) * LANES] = o * g[:, 3 * hd:3 * hd + 1]

    ov = ov_ref[0:nc, 0:ns]
    h1, h2, h3 = _split3(psum)
    imp = _dot(h1, ov) + _dot(h2, ov) + _dot(h3, ov)
    scol =lax.broadcasted_iota(jnp.int32, (tq, ns), 1)
    scolf = scol.astype(F32)
    trow = t0 + lax.broadcasted_iota(jnp.int32, (tq, ns), 0)
    free = (scol * SEL_BLOCK <= trow) & (scol != 0) & (scol != (trow >> SEL_SHIFT))
    score = jnp.where(free, imp, -1.0)
    lane = lax.broadcasted_iota(jnp.int32, (tq, LANES), 1)
    idx_out = jnp.zeros((tq, LANES), jnp.int32)
    hit = jnp.zeros((tq, LANES), F32)
    for k in range(SEL_TOPK - 2):
        m = jnp.max(score, axis=-1, keepdims=True)
        ikf = jnp.min(jnp.where(score == m, scolf, 1e9), axis=-1, keepdims=True)
        score = jnp.where(scolf == ikf, -3e38, score)
        ik = ikf.astype(jnp.int32)
        idx_out = jnp.where(lane == k, ik, idx_out)
        hit = jnp.where(lane == (ik >> _log2(tsel // SEL_BLOCK)), 1.0, hit)
    idx_ref[0] = idx_out
    for sub in range(tq // tsel):
        flag_ref[0, sub] = jnp.max(hit[sub * tsel:(sub + 1) * tsel], axis=0, keepdims=True).astype(jnp.int32)


def _nsa_cmp(qn, kvc, gates, ov, tq, tsel, n_cmp):
    B, L, _ = qn.shape
    nq = L // tq
    nsub = tq // tsel
    nc = kvc.shape[1]
    row = lambda b, i: (b, i, 0)
    return pl.pallas_call(
        functools.partial(_nsa_cmp_kernel, tq=tq, tsel=tsel, n_cmp=n_cmp),
        grid=(B, nq),
        in_specs=[pl.BlockSpec((1, tq, 512), row), pl.BlockSpec((1, nc, LANES), lambda b, i: (b, 0, 0)),
                  pl.BlockSpec((1, tq, LANES), row), _const_spec(ov.shape)],
        out_specs=[pl.BlockSpec((1, tq, 512), row), pl.BlockSpec((1, tq, LANES), row),
                   pl.BlockSpec((1, nsub, 1, LANES), lambda b, i: (b, i, 0, 0))],
        out_shape=[jax.ShapeDtypeStruct((B, L, 512), F32), jax.ShapeDtypeStruct((B, L, LANES), jnp.int32),
                   jax.ShapeDtypeStruct((B, L // tsel, 1, LANES), jnp.int32)],
        compiler_params=_params("parallel", "parallel"),
        name="nsa_cmp",
    )(qn, kvc, gates, ov)


def _nsa_sw_kernel(bits_ref, q_ref, skv_ref, wkv_ref, idx_ref, g_ref, ocmp_ref, wb_ref, o_ref,
                   m_sc, l_sc, acc_sc, *, tq, nwords):
    b = pl.program_id(0)
    qi = pl.program_id(1)
    nq = pl.num_programs(1)
    t0 = qi * tq
    rows = NSA_HEADS * tq
    qs = jnp.concatenate([q_ref[0, :, hd * LANES:(hd + 1) * LANES] for hd in range(NSA_HEADS)], axis=0)
    rowi = lax.broadcasted_iota(jnp.int32, (rows, LANES), 0)
    ti = t0 + (rowi & (tq - 1))
    tf = ti.astype(F32)
    cur_blk = ti >> SEL_SHIFT
    head = rowi >> _log2(tq)
    slope = jnp.where(head == 0, _slope(0), jnp.where(head == 1, _slope(1),
                      jnp.where(head == 2, _slope(2), _slope(3))))
    idx4 = jnp.concatenate([idx_ref[0]] * NSA_HEADS, axis=0)
    picks = [jnp.broadcast_to(idx4[:, k:k + 1], (rows, LANES)) for k in range(SEL_TOPK - 2)]
    lane_k = lax.broadcasted_iota(jnp.int32, (1, tq), 1)

    def scores(j, kv, forced_blocks, valid):
        kpos = j * tq + lane_k
        blk = kpos >> SEL_SHIFT
        distf = tf - kpos.astype(F32)
        msk = (blk == picks[0]) | (blk == picks[1])
        if forced_blocks:
            min_dist = jnp.where(valid, 0.0, 3e38)
            msk = (msk | (blk == 0) | (blk == cur_blk)) & (distf >= min_dist)
        s = jnp.where(msk, _dot_nt(qs, kv) - slope * distf, NEG_INF)
        return s, msk

    jprev = jnp.maximum(qi - 1, 0)
    tiles = [(0, skv_ref[0, 0:tq, :], qi >= 0), (jprev, skv_ref[0, pl.ds(pl.multiple_of(jprev * tq, tq), tq), :], qi >= 2),
             (qi, skv_ref[0, pl.ds(pl.multiple_of(qi * tq, tq), tq), :], qi >= 1)]
    sm = [scores(j, kv, True, valid) for j, kv, valid in tiles]
    m0 = jnp.maximum(jnp.maximum(jnp.max(sm[0][0], axis=-1, keepdims=True),
                                 jnp.max(sm[1][0], axis=-1, keepdims=True)),
                     jnp.max(sm[2][0], axis=-1, keepdims=True))
    l0 = jnp.zeros((rows, 1), F32)
    acc0 = jnp.zeros((rows, LANES), F32)
    for (s, msk), (_, kv, _) in zip(sm, tiles):
        p = jnp.where(msk, jnp.exp2(s - m0), 0.0)
        l0 = l0 + jnp.sum(p, axis=-1, keepdims=True)
        acc0 = acc0 + _dot(p.astype(BF16), kv)
    m_sc[...] = jnp.broadcast_to(m0, (rows, LANES))
    l_sc[...] = jnp.broadcast_to(l0, (rows, LANES))
    acc_sc[...] = acc0

    def sel_step(j, carry):
        word = bits_ref[(b * nq + qi) * nwords + (j >> 5)]
        needed = (lax.shift_right_logical(word, j & 31) & 1) == 1

        @pl.when(needed)
        def _():
            kv = skv_ref[0, pl.ds(pl.multiple_of(j * tq, tq), tq), :]
            s, msk = scores(j, kv, False, None)
            m_old = m_sc[...]
            m_new = jnp.maximum(m_old, jnp.max(s, axis=-1, keepdims=True))
            p = jnp.where(msk, jnp.exp2(s - m_new), 0.0)
            alpha = jnp.exp2(m_old - m_new)
            l_sc[...] = alpha * l_sc[...] + jnp.sum(p, axis=-1, keepdims=True)
            acc_sc[...] = alpha * acc_sc[...] + _dot(p.astype(BF16), kv)
            m_sc[...] = m_new
        return carry

    lax.fori_loop(1, jnp.maximum(qi - 1, 1), sel_step, 0)
    o_sel = acc_sc[...] / l_sc[...]

    nband = WINDOW + tq
    start = pl.multiple_of(jnp.maximum(t0 - WINDOW, 0), tq)
    band = wkv_ref[0, pl.ds(start, nband), :]
    shift = (WINDOW - jnp.minimum(t0, WINDOW)) // LANES
    bias = jnp.concatenate([wb_ref[shift + k] for k in range(nband // LANES)], axis=-1)
    s = _dot_nt(qs, band) + bias
    e = jnp.exp2(s - jnp.max(s, axis=-1, keepdims=True))
    o_win = _dot(e.astype(BF16), band) * (1.0 / jnp.sum(e, axis=-1, keepdims=True))

    g = g_ref[0]
    lane = lax.broadcasted_iota(jnp.int32, (1, LANES), 1)
    for hd in range(NSA_HEADS):
        r = slice(hd * tq, (hd + 1) * tq)
        o = (ocmp_ref[0, :, hd * LANES:(hd + 1) * LANES]
             + g[:, 3 * hd + 1:3 * hd + 2] * o_sel[r]
             + g[:, 3 * hd + 2:3 * hd + 3] * o_win[r])
        o_ref[0, :, hd * LANES:(hd + 1) * LANES] = jnp.where(lane >= HEAD_DIM, o, 0.0).astype(BF16)


def _window_bias(tq):
    ncol = WINDOW + tq + WINDOW
    dist = np.arange(tq)[:, None] + WINDOW - np.arange(ncol)[None, :]
    ok = (dist >= 0) & (dist < WINDOW)
    tab = np.concatenate([np.where(ok, -_slope(hd) * dist, NEG_INF) for hd in range(NSA_HEADS)], axis=0)
    return jnp.asarray(tab.reshape(NSA_HEADS * tq, ncol // LANES, LANES).transpose(1, 0, 2), dtype=F32)


def _nsa_sw(bits, qn, skv, wkv, idx, gates, ocmp, wbias, tq, nwords):
    B, L, _ = qn.shape
    nq = L // tq
    row = lambda b, i, s: (b, i, 0)
    full = lambda b, i, s: (b, 0, 0)
    rows = NSA_HEADS * tq
    return pl.pallas_call(
        functools.partial(_nsa_sw_kernel, tq=tq, nwords=nwords),
        grid_spec=pltpu.PrefetchScalarGridSpec(
            num_scalar_prefetch=1,
            grid=(B, nq),
            in_specs=[pl.BlockSpec((1, tq, 512), row), pl.BlockSpec((1, L, LANES), full),
                      pl.BlockSpec((1, L, LANES), full), pl.BlockSpec((1, tq, LANES), row),
                      pl.BlockSpec((1, tq, LANES), row), pl.BlockSpec((1, tq, 512), row),
                      _const_spec(wbias.shape)],
            out_specs=pl.BlockSpec((1, tq, 512), row),
            scratch_shapes=[pltpu.VMEM((rows, LANES), F32), pltpu.VMEM((rows, LANES), F32),
                            pltpu.VMEM((rows, LANES), F32)],
        ),
        out_shape=jax.ShapeDtypeStruct((B, L, 512), BF16),
        compiler_params=_params("parallel", "arbitrary"),
        name="nsa_sw",
    )(bits, qn, skv, wkv, idx, gates, ocmp, wbias)


S5_X = S5_GROUPS * S5_STATE
S5_PAD = 8


def _s5_kernel(u_ref, bcat_ref, ccat_ref, a_ref, d_ref, wglu_ref, o_ref, x_sc, st_sc, *, steps, nb, nch):
    @pl.when(pl.program_id(0) == 0)
    def _():
        st_sc[...] = jnp.zeros_like(st_sc)

    stride = steps + S5_PAD
    ng = S5_X // LANES
    for b in range(nb):
        bu = _dot(u_ref[b].astype(BF16), bcat_ref[...])
        for k in range(2 * ng):
            x_sc[k, b * stride:b * stride + steps, :] = bu[:, k * LANES:(k + 1) * LANES]
    gpc = ng // nch
    for c in range(nch):
        ks = list(range(c * gpc, (c + 1) * gpc))
        ar = [jnp.broadcast_to(a_ref[0:1, k * LANES:(k + 1) * LANES], (nb, LANES)) for k in ks]
        ai = [jnp.broadcast_to(a_ref[0:1, S5_X + k * LANES:S5_X + (k + 1) * LANES], (nb, LANES)) for k in ks]

        def body(t, carry):
            xr, xi = carry
            r = pl.ds(t, nb, stride=stride)
            nr, ni = [], []
            for n, k in enumerate(ks):
                vr = ar[n] * xr[n] - ai[n] * xi[n] + x_sc[k, r, :]
                vi = ar[n] * xi[n] + ai[n] * xr[n] + x_sc[ng + k, r, :]
                x_sc[k, r, :] = vr
                x_sc[ng + k, r, :] = vi
                nr.append(vr)
                ni.append(vi)
            return tuple(nr), tuple(ni)

        init = (tuple(st_sc[:, k * LANES:(k + 1) * LANES] for k in ks),
                tuple(st_sc[:, S5_X + k * LANES:S5_X + (k + 1) * LANES] for k in ks))
        xr, xi = lax.fori_loop(0, steps, body, init, unroll=8)
        for n, k in enumerate(ks):
            st_sc[:, k * LANES:(k + 1) * LANES] = xr[n]
            st_sc[:, S5_X + k * LANES:S5_X + (k + 1) * LANES] = xi[n]

    for b in range(nb):
        xs = jnp.concatenate([x_sc[k, b * stride:b * stride + steps, :] for k in range(2 * ng)], axis=-1)
        y = _dot(xs.astype(BF16), ccat_ref[...]) + d_ref[...] * u_ref[b]
        y = _gelu(y)
        o_ref[b] = (y * _sigmoid(_dot(y.astype(BF16), wglu_ref[...]))).astype(BF16)


def _s5(u, bcat, ccat, a_cat, d, wglu, steps):
    nb, L, _ = u.shape
    return pl.pallas_call(
        functools.partial(_s5_kernel, steps=steps, nb=nb, nch=2),
        grid=(L // steps,),
        in_specs=[pl.BlockSpec((nb, steps, S5_WIDTH), lambda i: (0, i, 0)), _const_spec(bcat.shape),
                  _const_spec(ccat.shape), _const_spec(a_cat.shape), _const_spec(d.shape),
                  _const_spec(wglu.shape)],
        out_specs=pl.BlockSpec((nb, steps, S5_WIDTH), lambda i: (0, i, 0)),
        out_shape=jax.ShapeDtypeStruct((nb, L, S5_WIDTH), BF16),
        scratch_shapes=[pltpu.VMEM((2 * S5_X // LANES, nb * (steps + S5_PAD), LANES), F32),
                        pltpu.VMEM((nb, 2 * S5_X), F32)],
        compiler_params=_params("arbitrary"),
        name="s5",
    )(u, bcat, ccat, a_cat, d, wglu)


SIGN_BIT = np.int32(-2 ** 31)
SB_STOP_LOG2 = 170.0


def _sb_softplus2(z):
    neg_abs = lax.bitcast_convert_type(lax.bitcast_convert_type(z, jnp.int32) | SIGN_BIT, F32)
    return jnp.maximum(z, 0.0) + jnp.log2(1.0 + jnp.exp2(neg_abs))


def _sb_suffix(spb, tri_ref):
    half = tri_ref.shape[1]
    return jnp.concatenate([_dot(spb, tri_ref[...]), _dot(spb[:, half:], tri_ref[0:half, :])], axis=-1)


def _sb_kernel(q_ref, kv_ref, tri1_ref, tri_ref, o_ref, acc_sc, c_sc, z_sc, w_sc, *, tq, tk1, tk):
    i = pl.program_id(2)
    q = q_ref[0]
    start = jnp.maximum((i + 1) * tq - tk1, 0)

    def kv_block(j):
        return kv_ref[0, pl.ds(pl.multiple_of(j * tk, tk), tk), :]

    kv1 = kv_ref[0, pl.ds(pl.multiple_of(start, tk), tk1), :]
    z = _dot_nt(q, kv1)
    t = i * tq + lax.broadcasted_iota(jnp.int32, (tq, tk1), 0)
    causal = (start + lax.broadcasted_iota(jnp.int32, (tq, tk1), 1)) < t
    spb = jnp.where(causal, _sb_softplus2(z), 0.0).astype(BF16)
    r = _sb_suffix(spb, tri1_ref)
    w = jnp.where(causal, jnp.exp2(jnp.minimum(z + r, 0.0)), 0.0)
    acc_sc[...] = _dot(w.astype(BF16), kv1)
    c_sc[...] = jnp.broadcast_to(r[:, 0:1], (tq, LANES))
    j0 = start // tk - 1
    z_sc[...] = _dot_nt(q, kv_block(jnp.maximum(j0, 0)))
    w_sc[...] = jnp.zeros_like(w_sc)

    def step(j):
        z = z_sc[...]
        z_next = _dot_nt(q, kv_block(jnp.maximum(j - 1, 0)))
        acc_sc[...] += _dot(w_sc[...], kv_block(j + 1))
        r = _sb_suffix(_sb_softplus2(z).astype(BF16), tri_ref)
        c = c_sc[...]
        cw = jnp.concatenate([c] * (tk // LANES), axis=-1)
        w_sc[...] = jnp.exp2(jnp.minimum(z + r + cw, 0.0)).astype(BF16)
        z_sc[...] = z_next
        c_sc[...] = c + r[:, 0:1]

    def cond(carry):
        j, cmax = carry
        return (j >= 0) & (cmax > -SB_STOP_LOG2)

    def body(carry):
        j, _ = carry
        step(j)
        return j - 1, jnp.max(c_sc[...])

    j_next, _ = lax.while_loop(cond, body, (j0, jnp.max(c_sc[...])))

    @pl.when(j_next < j0)
    def _():
        acc_sc[...] += _dot(w_sc[...], kv_block(j_next + 1))

    lane = lax.broadcasted_iota(jnp.int32, (1, LANES), 1)
    o_ref[0] = jnp.where(lane >= HEAD_DIM, acc_sc[...], 0.0).astype(BF16)


def _sb(sbq, sbkv, tri1, tri, tq, tk1, tk):
    B, L, _ = sbq.shape
    assert tq % tk == 0 and tk1 % tk == 0 and L >= tk1
    return pl.pallas_call(
        functools.partial(_sb_kernel, tq=tq, tk1=tk1, tk=tk),
        grid=(B, SB_HEADS, L // tq),
        in_specs=[pl.BlockSpec((1, tq, LANES), lambda b, h, i: (b, i, h)),
                  pl.BlockSpec((1, L, LANES), lambda b, h, i: (b, 0, h)),
                  _const_spec(tri1.shape), _const_spec(tri.shape)],
        out_specs=pl.BlockSpec((1, tq, LANES), lambda b, h, i: (b, i, h)),
        out_shape=jax.ShapeDtypeStruct((B, L, SB_HEADS * LANES), BF16),
        scratch_shapes=[pltpu.VMEM((tq, LANES), F32), pltpu.VMEM((tq, LANES), F32),
                        pltpu.VMEM((tq, tk), F32), pltpu.VMEM((tq, tk), BF16)],
        compiler_params=_params("parallel", "parallel", "arbitrary"),
        name="sb_attn",
    )(sbq, sbkv, tri1, tri)


FFN_CHUNK = 1024
HALO = 8


def _merge_ffn_kernel(x_ref, oa_ref, ob_ref, oc_ref, mg_ref, wa_ref, wb_ref, wc_ref, wout_ref, gain_ref,
                      wup_ref, cw_ref, cb_ref, wdown_ref, out_ref, ubuf, carry, *, tl, dff):
    @pl.when(pl.program_id(1) == 0)
    def _():
        carry[...] = jnp.zeros_like(carry)

    d = x_ref.shape[2]
    merged = (mg_ref[0, :, 0:d].astype(F32) * _dot(oa_ref[0], wa_ref[...])
              + mg_ref[0, :, d:2 * d].astype(F32) * _dot(ob_ref[0], wb_ref[...])
              + mg_ref[0, :, 2 * d:3 * d].astype(F32) * _dot(oc_ref[0], wc_ref[...]))
    x1 = x_ref[0] + _dot(merged.astype(BF16), wout_ref[...])
    h = x1 * lax.rsqrt(jnp.mean(x1 * x1, axis=-1, keepdims=True) + NORM_EPS) * gain_ref[...]
    hb = h.astype(BF16)

    def conv_cols(off):
        cols = slice(off, off + FFN_CHUNK)
        u = _dot(hb, wup_ref[:, cols])
        ubuf[0:HALO, :] = carry[:, cols]
        ubuf[HALO:HALO + tl, :] = u
        carry[:, cols] = u[tl - HALO:tl, :]
        return (cw_ref[2:3, cols] * u + cw_ref[1:2, cols] * ubuf[HALO - 1:HALO - 1 + tl, :]
                + cw_ref[0:1, cols] * ubuf[HALO - 2:HALO - 2 + tl, :] + cb_ref[:, cols])

    acc = x1
    for j in range(dff // FFN_CHUNK):
        gate = conv_cols(j * FFN_CHUNK)
        val = conv_cols(dff + j * FFN_CHUNK)
        act = (_gelu(gate) * val).astype(BF16)
        acc = acc + _dot(act, wdown_ref[j * FFN_CHUNK:(j + 1) * FFN_CHUNK, :])
    out_ref[0] = acc


def _merge_ffn(x, oa, ob, oc, mg, wa, wb, wc, wout, gain, wup, cw, cb, wdown, tl):
    B, L, D = x.shape
    dff = wdown.shape[0]
    row = lambda b, l: (b, l, 0)
    weights = [wa, wb, wc, wout, gain, wup, cw, cb, wdown]
    return pl.pallas_call(
        functools.partial(_merge_ffn_kernel, tl=tl, dff=dff),
        grid=(B, L // tl),
        in_specs=[pl.BlockSpec((1, tl, D), row), pl.BlockSpec((1, tl, oa.shape[2]), row),
                  pl.BlockSpec((1, tl, ob.shape[2]), row), pl.BlockSpec((1, tl, oc.shape[2]), row),
                  pl.BlockSpec((1, tl, mg.shape[2]), row)] + [_const_spec(w.shape) for w in weights],
        out_specs=pl.BlockSpec((1, tl, D), row),
        out_shape=jax.ShapeDtypeStruct((B, L, D), F32),
        scratch_shapes=[pltpu.VMEM((HALO + tl, FFN_CHUNK), F32), pltpu.VMEM((HALO, 2 * dff), F32)],
        compiler_params=_params("parallel", "arbitrary"),
        name="merge_ffn",
    )(x, oa, ob, oc, mg, *weights)


def _pad_heads(w, n_heads):
    d = w.shape[0]
    w = w.reshape(d, n_heads, HEAD_DIM)
    return jnp.concatenate([w, jnp.zeros_like(w)], axis=-1).reshape(d, n_heads * LANES)


def _pack_w_in(w):
    d = w.shape[0]
    o_g = 640
    o_s5 = o_g + NSA_HEADS * 3
    o_sb = o_s5 + S5_WIDTH
    o_mg = o_sb + 3 * SB_HEADS * HEAD_DIM
    hw = SB_HEADS * HEAD_DIM
    gates = jnp.pad(w[:, o_g:o_s5], ((0, 0), (0, LANES - NSA_HEADS * 3)))
    sbk = w[:, o_sb + hw:o_sb + 2 * hw].reshape(d, SB_HEADS, 1, HEAD_DIM)
    sbv = w[:, o_sb + 2 * hw:o_mg].reshape(d, SB_HEADS, 1, HEAD_DIM)
    sbkv = jnp.concatenate([sbk, sbv], axis=2).reshape(d, SB_HEADS * LANES)
    packed = jnp.concatenate([_pad_heads(w[:, 0:256], NSA_HEADS), w[:, 256:o_g], gates, w[:, o_s5:o_sb],
                              _pad_heads(w[:, o_sb:o_sb + hw], SB_HEADS), sbkv, w[:, o_mg:]], axis=1)
    return packed.astype(BF16)


def _pad_rows(w, n_heads):
    d = w.shape[1]
    w = w.reshape(n_heads, HEAD_DIM, d)
    return jnp.concatenate([jnp.zeros_like(w), w], axis=1).reshape(n_heads * LANES, d).astype(BF16)


def _pad_gain(g, fill):
    return jnp.concatenate([g, jnp.full_like(g, fill)], axis=-1)


def _cmp_weights(wk, wv):
    z = jnp.zeros_like(wk)
    wkv = jnp.concatenate([jnp.concatenate([wk, z], axis=2), jnp.concatenate([z, wv], axis=2)], axis=1)
    half = CMP_BLOCK // 2
    wt = wkv[:half].reshape(half * LANES, LANES)
    wb = wkv[half:].reshape(half * LANES, LANES)
    return wt.astype(BF16), wb.astype(BF16)


def _s5_params(a_re, a_im, log_dt, b_re, b_im, c_re, c_im):
    dt = jnp.exp(log_dt.astype(F32))[:, None]
    A = lax.complex(a_re.astype(F32), a_im.astype(F32))
    A_bar = jnp.exp(dt * A)
    B_bar = ((A_bar - 1.0) / A)[..., None] * lax.complex(b_re.astype(F32), b_im.astype(F32))
    eye = jnp.eye(S5_GROUPS, dtype=F32)

    def bdiag_in(m):
        return jnp.einsum('gpc,gh->gchp', m, eye).reshape(S5_WIDTH, S5_X)

    def bdiag_out(m):
        return jnp.einsum('gcp,gh->gphc', m, eye).reshape(S5_X, S5_WIDTH)

    bcat = jnp.concatenate([bdiag_in(jnp.real(B_bar)), bdiag_in(jnp.imag(B_bar))], axis=1).astype(BF16)
    ccat = jnp.concatenate([bdiag_out(c_re.astype(F32)), -bdiag_out(c_im.astype(F32))], axis=0).astype(BF16)
    a_cat = jnp.concatenate([jnp.real(A_bar).reshape(1, S5_X), jnp.imag(A_bar).reshape(1, S5_X)], axis=1)
    return bcat, ccat, a_cat


def _overlap_matrix(n_rows, n_cmp, n_sel):
    cs = np.arange(n_rows)[:, None] * CMP_STRIDE
    ss = np.arange(n_sel)[None, :] * SEL_BLOCK
    ov = np.clip(np.minimum(cs + CMP_BLOCK, ss + SEL_BLOCK) - np.maximum(cs, ss), 0, None) / CMP_BLOCK
    ov[n_cmp:] = 0.0
    return jnp.asarray(ov, dtype=BF16)


def _sb_suffix_matrix(tk):
    u = (np.arange(tk)[:, None] >= np.arange(tk)[None, :]).astype(np.float32)
    return jnp.asarray(-np.concatenate([u, np.ones_like(u)], axis=0), dtype=BF16)


def kernel(x, norm_mix, w_in, nsa_q_gain, nsa_k_gain, cmp_pe, cmp_wk, cmp_wv, s5_a_re, s5_a_im, s5_log_dt,
           s5_b_re, s5_b_im, s5_c_re, s5_c_im, s5_d, s5_w_glu, w_br_nsa, w_br_s5, w_br_sb, w_out, norm_ffn,
           w_up, conv_w, conv_b, w_down):
    B, L, D = x.shape
    depth = w_in.shape[0]
    tq = 128
    tq_cmp = 256
    tl_in = min(512, L)
    tl_ffn = min(512, L)
    sb_tq, sb_tk1, sb_tk = 256, 512, 256
    s5_steps = min(256, L)
    assert L % 512 == 0 and L >= WINDOW + tq and D == 1024
    n_cmp = (L - CMP_BLOCK) // CMP_STRIDE + 1
    n_rows = L // CMP_STRIDE
    n_sel = L // SEL_BLOCK
    nq = L // tq
    nwords = (nq + 31) // 32
    ov = _overlap_matrix(n_rows, n_cmp, n_sel)
    tri1, tri = _sb_suffix_matrix(sb_tk1 // 2), _sb_suffix_matrix(sb_tk // 2)
    wbias = _window_bias(tq)

    def pack_layer(p):
        wt, wb = _cmp_weights(p['cmp_wk'][0], p['cmp_wv'][0])
        bcat, ccat, a_cat = _s5_params(p['s5_a_re'], p['s5_a_im'], p['s5_log_dt'], p['s5_b_re'], p['s5_b_im'],
                                       p['s5_c_re'], p['s5_c_im'])
        return dict(
            norm_mix=p['norm_mix'][None, :], w_in=_pack_w_in(p['w_in']),
            qg=_pad_gain(p['nsa_q_gain'][None, :], 0.0), kg=_pad_gain(p['nsa_k_gain'], 1.0),
            pe=jnp.concatenate([p['cmp_pe'], p['cmp_pe']], axis=-1).reshape(2, (CMP_BLOCK // 2) * LANES),
            wt=wt, wb=wb, bcat=bcat, ccat=ccat, a_cat=a_cat, d=p['s5_d'].reshape(1, S5_WIDTH),
            wglu=p['s5_w_glu'].astype(BF16), wa=_pad_rows(p['w_br_nsa'], NSA_HEADS),
            wb5=p['w_br_s5'].astype(BF16), wc=_pad_rows(p['w_br_sb'], SB_HEADS), wout=p['w_out'].astype(BF16),
            norm_ffn=p['norm_ffn'][None, :], wup=p['w_up'].astype(BF16), cw=p['conv_w'],
            cb=p['conv_b'][None, :], wdown=p['w_down'].astype(BF16))

    packed = jax.vmap(pack_layer)(dict(
        norm_mix=norm_mix, w_in=w_in, nsa_q_gain=nsa_q_gain, nsa_k_gain=nsa_k_gain, cmp_pe=cmp_pe, cmp_wk=cmp_wk,
        cmp_wv=cmp_wv, s5_a_re=s5_a_re, s5_a_im=s5_a_im, s5_log_dt=s5_log_dt, s5_b_re=s5_b_re, s5_b_im=s5_b_im,
        s5_c_re=s5_c_re, s5_c_im=s5_c_im, s5_d=s5_d, s5_w_glu=s5_w_glu, w_br_nsa=w_br_nsa, w_br_s5=w_br_s5,
        w_br_sb=w_br_sb, w_out=w_out, norm_ffn=norm_ffn, w_up=w_up, conv_w=conv_w, conv_b=conv_b, w_down=w_down))

    for i in range(depth):
        p = {k: v[i] for k, v in packed.items()}
        qn, ckv, skv, wkv, gates, u_s5, sbq, sbkv, mg = _inproj(
            x, p['norm_mix'], p['w_in'], p['qg'], p['kg'][1:3], tl_in)

        kvc = _cmpkv(ckv.reshape(B, n_rows, CMP_STRIDE * LANES), p['pe'], p['wt'], p['wb'], p['kg'][0:1])
        ocmp, idx, flags = _nsa_cmp(qn, kvc, gates, ov, tq_cmp, tq, n_cmp)
        fl = jnp.pad(flags[:, :, 0, :nq], ((0, 0), (0, 0), (0, nwords * 32 - nq))).reshape(B, nq, nwords, 32)
        bits = jnp.sum(fl << jnp.arange(32, dtype=jnp.int32), axis=-1, dtype=jnp.int32).reshape(-1)
        o_a = _nsa_sw(bits, qn, skv, wkv, idx, gates, ocmp, wbias, tq, nwords)

        o_b = _s5(u_s5, p['bcat'], p['ccat'], p['a_cat'], p['d'], p['wglu'], s5_steps)

        o_c = _sb(sbq, sbkv, tri1, tri, sb_tq, sb_tk1, sb_tk)

        x = _merge_ffn(x, o_a, o_b, o_c, mg, p['wa'], p['wb5'], p['wc'], p['wout'], p['norm_ffn'],
                       p['wup'], p['cw'], p['cb'], p['wdown'], tl_ffn)
    return x
```

```python
import functools
import math

import numpy as np
import jax
import jax.numpy as jnp
from jax import lax
from jax.experimental import pallas as pl
from jax.experimental.pallas import tpu as pltpu

HEAD_DIM = 64
NSA_HEADS = 4
CMP_BLOCK = 32
CMP_STRIDE = 16
SEL_BLOCK = 32
SEL_TOPK = 4
WINDOW = 512
FORCE_SCORE = 1.0e4
S5_WIDTH = 256
S5_GROUP = 16
S5_GROUPS = S5_WIDTH // S5_GROUP
S5_STATE = 64
SB_HEADS = 4
NORM_EPS = 1e-6
NEG_INF = -1e30
LANES = 128
VMEM_LIMIT = 56 * 1024 * 1024

F32 = jnp.float32
BF16 = jnp.bfloat16


def _log2(n):
    assert n & (n - 1) == 0
    return n.bit_length() - 1


SEL_SHIFT = _log2(SEL_BLOCK)
LOG2E = math.log2(math.e)


def _dot(a, b):
    return jnp.dot(a, b, preferred_element_type=F32)


def _dot_nt(a, b):
    return lax.dot_general(a, b, (((1,), (1,)), ((), ())), preferred_element_type=F32)


def _const_spec(shape):
    nd = len(shape)
    return pl.BlockSpec(shape, lambda *_: (0,) * nd, pipeline_mode=pl.Buffered(1))


def _params(*sem):
    return pltpu.CompilerParams(dimension_semantics=sem, vmem_limit_bytes=VMEM_LIMIT)


def _gelu(x):
    return 0.5 * x * (1.0 + jnp.tanh(math.sqrt(2.0 / math.pi) * (x + 0.044715 * (x * x * x))))


def _sigmoid(x):
    return 1.0 / (1.0 + jnp.exp(-x))


def _split3(x):
    h1 = x.astype(BF16)
    r1 = x - h1.astype(F32)
    h2 = r1.astype(BF16)
    h3 = (r1 - h2.astype(F32)).astype(BF16)
    return h1, h2, h3


D_MODEL = 1024
N_BRANCHES = 3
HEADS_W = NSA_HEADS * LANES
C_QN = 0
C_KV = C_QN + HEADS_W
C_G = C_KV + 3 * LANES
C_S5 = C_G + LANES
C_SBQ = C_S5 + S5_WIDTH
C_SBKV = C_SBQ + HEADS_W
C_MG = C_SBKV + HEADS_W
C_END = C_MG + N_BRANCHES * D_MODEL


def _inproj_kernel(x_ref, gain_ref, w_ref, qg_ref, kg_ref,
                   qn_ref, ckv_ref, skv_ref, wkv_ref, g_ref, s5_ref, sbq_ref, sbkv_ref, mg_ref):
    x = x_ref[0]
    h = x * lax.rsqrt(jnp.mean(x * x, axis=-1, keepdims=True) + NORM_EPS) * gain_ref[...]
    hb = h.astype(BF16)
    lane = lax.broadcasted_iota(jnp.int32, (1, LANES), 1)
    lo = lane < HEAD_DIM

    seg = _dot(hb, w_ref[:, C_QN:C_S5])
    scale = HEAD_DIM ** -0.5 * LOG2E
    for hd in range(NSA_HEADS):
        q = seg[:, hd * LANES:(hd + 1) * LANES]
        ms = jnp.sum(q * q, axis=-1, keepdims=True) * (1.0 / HEAD_DIM)
        qn = q * lax.rsqrt(ms + NORM_EPS) * qg_ref[...] * scale
        qn_ref[0, :, hd * LANES:(hd + 1) * LANES] = qn.astype(BF16)
    ckv_ref[0] = seg[:, C_KV:C_KV + LANES].astype(BF16)
    for n, ref in ((1, skv_ref), (2, wkv_ref)):
        kv = seg[:, C_KV + n * LANES:C_KV + (n + 1) * LANES]
        ms = jnp.sum(jnp.where(lo, kv * kv, 0.0), axis=-1, keepdims=True) * (1.0 / HEAD_DIM)
        sc = jnp.where(lo, lax.rsqrt(ms + NORM_EPS) * kg_ref[n - 1:n, :], 1.0)
        ref[0] = (kv * sc).astype(BF16)
    g_ref[0] = _sigmoid(seg[:, C_G:C_G + LANES])

    seg = _dot(hb, w_ref[:, C_S5:C_MG])
    s5_ref[0] = seg[:, 0:S5_WIDTH]
    sbq_ref[0] = (seg[:, C_SBQ - C_S5:C_SBKV - C_S5] * scale).astype(BF16)
    sbkv_ref[0] = seg[:, C_SBKV - C_S5:C_MG - C_S5].astype(BF16)

    for c in range(3):
        seg = _dot(hb, w_ref[:, C_MG + c * 1024:C_MG + (c + 1) * 1024])
        mg_ref[0, :, c * 1024:(c + 1) * 1024] = _sigmoid(seg).astype(BF16)


def _inproj(x, gain, w, qg, kg, tl):
    B, L, D = x.shape
    grid = (B, L // tl)
    row = lambda b, l: (b, l, 0)

    def out(n, dt):
        return jax.ShapeDtypeStruct((B, L, n), dt), pl.BlockSpec((1, tl, n), row)

    outs = [out(HEADS_W, BF16), out(LANES, BF16), out(LANES, BF16), out(LANES, BF16), out(LANES, F32),
            out(S5_WIDTH, F32), out(HEADS_W, BF16), out(HEADS_W, BF16), out(C_END - C_MG, BF16)]
    return pl.pallas_call(
        _inproj_kernel,
        grid=grid,
        in_specs=[pl.BlockSpec((1, tl, D), row), _const_spec((1, D)), _const_spec((D, C_END)),
                  _const_spec((1, LANES)), _const_spec((2, LANES))],
        out_specs=[o[1] for o in outs],
        out_shape=[o[0] for o in outs],
        compiler_params=_params("parallel", "parallel"),
        name="inproj",
    )(x, gain, w, qg, kg)


def _cmpkv_kernel(x_ref, pe_ref, wt_ref, wb_ref, kg_ref, o_ref):
    x = x_ref[0].astype(F32)
    a = _dot((x + pe_ref[0:1, :]).astype(BF16), wt_ref[...])
    b = _dot((x + pe_ref[1:2, :]).astype(BF16), wb_ref[...])
    n = x.shape[0]
    kv = a + pltpu.roll(b, n - 1, 0)
    lane = lax.broadcasted_iota(jnp.int32, (1, LANES), 1)
    lo = lane < HEAD_DIM
    ms = jnp.sum(jnp.where(lo, kv * kv, 0.0), axis=-1, keepdims=True) * (1.0 / HEAD_DIM)
    sc = jnp.where(lo, lax.rsqrt(ms + NORM_EPS) * kg_ref[...], 1.0)
    rowi = lax.broadcasted_iota(jnp.int32, (n, 1), 0)
    o_ref[0] = jnp.where(rowi < n - 1, kv * sc, 0.0).astype(BF16)


def _cmpkv(x16, pe2, wt, wb, kg):
    B, n, w = x16.shape
    return pl.pallas_call(
        _cmpkv_kernel,
        grid=(B,),
        in_specs=[pl.BlockSpec((1, n, w), lambda b: (b, 0, 0)), _const_spec((2, w)),
                  _const_spec((w, LANES)), _const_spec((w, LANES)), _const_spec((1, LANES))],
        out_specs=pl.BlockSpec((1, n, LANES), lambda b: (b, 0, 0)),
        out_shape=jax.ShapeDtypeStruct((B, n, LANES), BF16),
        compiler_params=_params("parallel"),
        name="cmpkv",
    )(x16, pe2, wt, wb, kg)


def _slope(hd):
    return 2.0 ** (-8.0 * (hd + 1) / NSA_HEADS) * LOG2E


def _nsa_cmp_kernel(q_ref, kvc_ref, g_ref, ov_ref, ocmp_ref, idx_ref, flag_ref, *, tq, tsel, n_cmp):
    qi = pl.program_id(1)
    nc_all, ns_all = ov_ref.shape
    need = (qi + 1) * (tq // CMP_STRIDE)
    bounds = list(range(LANES, nc_all, LANES)) + [nc_all]
    for v, nc in enumerate(bounds):
        lo = bounds[v - 1] if v else 0
        ns = min(ns_all, max(LANES, nc // (SEL_BLOCK // CMP_STRIDE)))
        pl.when((need > lo) & (need <= nc))(functools.partial(
            _nsa_cmp_body, q_ref, kvc_ref, g_ref, ov_ref, ocmp_ref, idx_ref, flag_ref, qi * tq,
            tq=tq, tsel=tsel, n_cmp=n_cmp, nc=nc, ns=ns))


def _nsa_cmp_body(q_ref, kvc_ref, g_ref, ov_ref, ocmp_ref, idx_ref, flag_ref, t0, *, tq, tsel, n_cmp, nc, ns):
    kvc = kvc_ref[0, 0:nc, :]
    g = g_ref[0]
    col = lax.broadcasted_iota(jnp.int32, (tq, nc), 1)
    t = t0 + lax.broadcasted_iota(jnp.int32, (tq, nc), 0)
    dist = t - (col * CMP_STRIDE + (CMP_BLOCK - 1))
    ok = (dist >= 0) & (col < n_cmp)
    distf = dist.astype(F32)
    psum = jnp.zeros((tq, nc), F32)
    for hd in range(NSA_HEADS):
        q = q_ref[0, :, hd * LANES:(hd + 1) * LANES]
        s = _dot_nt(q, kvc)
        s = jnp.where(ok, s - _slope(hd) * distf, NEG_INF)
        e = jnp.exp2(s - jnp.max(s, axis=-1, keepdims=True))
        p = jnp.where(ok, e * (1.0 / jnp.sum(e, axis=-1, keepdims=True)), 0.0)
        psum = psum + p
        o = _dot(p.astype(BF16), kvc)
        ocmp_ref[0, :, hd * LANES:(hd + 1) * LANES] = o * g[:, 3 * hd:3 * hd + 1]

    ov = ov_ref[0:nc, 0:ns]
    h1, h2, h3 = _split3(psum)
    imp = _dot(h1, ov) + _dot(h2, ov) + _dot(h3, ov)
    scol =lax.broadcasted_iota(jnp.int32, (tq, ns), 1)
    scolf = scol.astype(F32)
    trow = t0 + lax.broadcasted_iota(jnp.int32, (tq, ns), 0)
    free = (scol * SEL_BLOCK <= trow) & (scol != 0) & (scol != (trow >> SEL_SHIFT))
    score = jnp.where(free, imp, -1.0)
    lane = lax.broadcasted_iota(jnp.int32, (tq, LANES), 1)
    idx_out = jnp.zeros((tq, LANES), jnp.int32)
    hit = jnp.zeros((tq, LANES), F32)
    for k in range(SEL_TOPK - 2):
        m = jnp.max(score, axis=-1, keepdims=True)
        ikf = jnp.min(jnp.where(score == m, scolf, 1e9), axis=-1, keepdims=True)
        score = jnp.where(scolf == ikf, -3e38, score)
        ik = ikf.astype(jnp.int32)
        idx_out = jnp.where(lane == k, ik, idx_out)
        hit = jnp.where(lane == (ik >> _log2(tsel // SEL_BLOCK)), 1.0, hit)
    idx_ref[0] = idx_out
    for sub in range(tq // tsel):
        flag_ref[0, sub] = jnp.max(hit[sub * tsel:(sub + 1) * tsel], axis=0, keepdims=True).astype(jnp.int32)


def _nsa_cmp(qn, kvc, gates, ov, tq, tsel, n_cmp):
    B, L, _ = qn.shape
    nq = L // tq
    nsub = tq // tsel
    nc = kvc.shape[1]
    row = lambda b, i: (b, i, 0)
    return pl.pallas_call(
        functools.partial(_nsa_cmp_kernel, tq=tq, tsel=tsel, n_cmp=n_cmp),
        grid=(B, nq),
        in_specs=[pl.BlockSpec((1, tq, HEADS_W), row), pl.BlockSpec((1, nc, LANES), lambda b, i: (b, 0, 0)),
                  pl.BlockSpec((1, tq, LANES), row), _const_spec(ov.shape)],
        out_specs=[pl.BlockSpec((1, tq, HEADS_W), row), pl.BlockSpec((1, tq, LANES), row),
                   pl.BlockSpec((1, nsub, 1, LANES), lambda b, i: (b, i, 0, 0))],
        out_shape=[jax.ShapeDtypeStruct((B, L, HEADS_W), F32), jax.ShapeDtypeStruct((B, L, LANES), jnp.int32),
                   jax.ShapeDtypeStruct((B, L // tsel, 1, LANES), jnp.int32)],
        compiler_params=_params("parallel", "parallel"),
        name="nsa_cmp",
    )(qn, kvc, gates, ov)


def _nsa_sw_kernel(bits_ref, q_ref, skv_ref, wkv_ref, idx_ref, g_ref, ocmp_ref, wb_ref, o_ref,
                   m_sc, l_sc, acc_sc, *, tq, nwords):
    b = pl.program_id(0)
    qi = pl.program_id(1)
    nq = pl.num_programs(1)
    t0 = qi * tq
    rows = NSA_HEADS * tq
    qs = jnp.concatenate([q_ref[0, :, hd * LANES:(hd + 1) * LANES] for hd in range(NSA_HEADS)], axis=0)
    rowi = lax.broadcasted_iota(jnp.int32, (rows, LANES), 0)
    ti = t0 + (rowi & (tq - 1))
    tf = ti.astype(F32)
    cur_blk = ti >> SEL_SHIFT
    head = rowi >> _log2(tq)
    slope = jnp.where(head == 0, _slope(0), jnp.where(head == 1, _slope(1),
                      jnp.where(head == 2, _slope(2), _slope(3))))
    idx4 = jnp.concatenate([idx_ref[0]] * NSA_HEADS, axis=0)
    picks = [jnp.broadcast_to(idx4[:, k:k + 1], (rows, LANES)) for k in range(SEL_TOPK - 2)]
    lane_k = lax.broadcasted_iota(jnp.int32, (1, tq), 1)

    def scores(j, kv, forced_blocks, valid):
        kpos = j * tq + lane_k
        blk = kpos >> SEL_SHIFT
        distf = tf - kpos.astype(F32)
        msk = (blk == picks[0]) | (blk == picks[1])
        if forced_blocks:
            min_dist = jnp.where(valid, 0.0, 3e38)
            msk = (msk | (blk == 0) | (blk == cur_blk)) & (distf >= min_dist)
        s = jnp.where(msk, _dot_nt(qs, kv) - slope * distf, NEG_INF)
        return s, msk

    jprev = jnp.maximum(qi - 1, 0)
    tiles = [(0, skv_ref[0, 0:tq, :], qi >= 0), (jprev, skv_ref[0, pl.ds(pl.multiple_of(jprev * tq, tq), tq), :], qi >= 2),
             (qi, skv_ref[0, pl.ds(pl.multiple_of(qi * tq, tq), tq), :], qi >= 1)]
    sm = [scores(j, kv, True, valid) for j, kv, valid in tiles]
    m0 = jnp.maximum(jnp.maximum(jnp.max(sm[0][0], axis=-1, keepdims=True),
                                 jnp.max(sm[1][0], axis=-1, keepdims=True)),
                     jnp.max(sm[2][0], axis=-1, keepdims=True))
    l0 = jnp.zeros((rows, 1), F32)
    acc0 = jnp.zeros((rows, LANES), F32)
    for (s, msk), (_, kv, _) in zip(sm, tiles):
        p = jnp.where(msk, jnp.exp2(s - m0), 0.0)
        l0 = l0 + jnp.sum(p, axis=-1, keepdims=True)
        acc0 = acc0 + _dot(p.astype(BF16), kv)
    m_sc[...] = jnp.broadcast_to(m0, (rows, LANES))
    l_sc[...] = jnp.broadcast_to(l0, (rows, LANES))
    acc_sc[...] = acc0

    def sel_step(j, carry):
        word = bits_ref[(b * nq + qi) * nwords + (j >> 5)]
        needed = (lax.shift_right_logical(word, j & 31) & 1) == 1

        @pl.when(needed)
        def _():
            kv = skv_ref[0, pl.ds(pl.multiple_of(j * tq, tq), tq), :]
            s, msk = scores(j, kv, False, None)
            m_old = m_sc[...]
            m_new = jnp.maximum(m_old, jnp.max(s, axis=-1, keepdims=True))
            p = jnp.where(msk, jnp.exp2(s - m_new), 0.0)
            alpha = jnp.exp2(m_old - m_new)
            l_sc[...] = alpha * l_sc[...] + jnp.sum(p, axis=-1, keepdims=True)
            acc_sc[...] = alpha * acc_sc[...] + _dot(p.astype(BF16), kv)
            m_sc[...] = m_new
        return carry

    lax.fori_loop(1, jnp.maximum(qi - 1, 1), sel_step, 0)
    o_sel = acc_sc[...] / l_sc[...]

    nband = WINDOW + tq
    start = pl.multiple_of(jnp.maximum(t0 - WINDOW, 0), tq)
    band = wkv_ref[0, pl.ds(start, nband), :]
    shift = (WINDOW - jnp.minimum(t0, WINDOW)) // LANES
    bias = jnp.concatenate([wb_ref[shift + k] for k in range(nband // LANES)], axis=-1)
    s = _dot_nt(qs, band) + bias
    e = jnp.exp2(s - jnp.max(s, axis=-1, keepdims=True))
    o_win = _dot(e.astype(BF16), band) * (1.0 / jnp.sum(e, axis=-1, keepdims=True))

    g = g_ref[0]
    lane = lax.broadcasted_iota(jnp.int32, (1, LANES), 1)
    for hd in range(NSA_HEADS):
        r = slice(hd * tq, (hd + 1) * tq)
        o = (ocmp_ref[0, :, hd * LANES:(hd + 1) * LANES]
             + g[:, 3 * hd + 1:3 * hd + 2] * o_sel[r]
             + g[:, 3 * hd + 2:3 * hd + 3] * o_win[r])
        o_ref[0, :, hd * LANES:(hd + 1) * LANES] = jnp.where(lane >= HEAD_DIM, o, 0.0).astype(BF16)


def _window_bias(tq):
    ncol = WINDOW + tq + WINDOW
    dist = np.arange(tq)[:, None] + WINDOW - np.arange(ncol)[None, :]
    ok = (dist >= 0) & (dist < WINDOW)
    tab = np.concatenate([np.where(ok, -_slope(hd) * dist, NEG_INF) for hd in range(NSA_HEADS)], axis=0)
    return jnp.asarray(tab.reshape(NSA_HEADS * tq, ncol // LANES, LANES).transpose(1, 0, 2), dtype=F32)


def _nsa_sw(bits, qn, skv, wkv, idx, gates, ocmp, wbias, tq, nwords):
    B, L, _ = qn.shape
    nq = L // tq
    row = lambda b, i, s: (b, i, 0)
    full = lambda b, i, s: (b, 0, 0)
    rows = NSA_HEADS * tq
    return pl.pallas_call(
        functools.partial(_nsa_sw_kernel, tq=tq, nwords=nwords),
        grid_spec=pltpu.PrefetchScalarGridSpec(
            num_scalar_prefetch=1,
            grid=(B, nq),
            in_specs=[pl.BlockSpec((1, tq, HEADS_W), row), pl.BlockSpec((1, L, LANES), full),
                      pl.BlockSpec((1, L, LANES), full), pl.BlockSpec((1, tq, LANES), row),
                      pl.BlockSpec((1, tq, LANES), row), pl.BlockSpec((1, tq, HEADS_W), row),
                      _const_spec(wbias.shape)],
            out_specs=pl.BlockSpec((1, tq, HEADS_W), row),
            scratch_shapes=[pltpu.VMEM((rows, LANES), F32), pltpu.VMEM((rows, LANES), F32),
                            pltpu.VMEM((rows, LANES), F32)],
        ),
        out_shape=jax.ShapeDtypeStruct((B, L, HEADS_W), BF16),
        compiler_params=_params("parallel", "arbitrary"),
        name="nsa_sw",
    )(bits, qn, skv, wkv, idx, gates, ocmp, wbias)


S5_X = S5_GROUPS * S5_STATE
S5_PAD = 8


def _s5_kernel(u_ref, bcat_ref, ccat_ref, a_ref, d_ref, wglu_ref, o_ref, x_sc, st_sc, *, steps, nb, nch):
    @pl.when(pl.program_id(0) == 0)
    def _():
        st_sc[...] = jnp.zeros_like(st_sc)

    stride = steps + S5_PAD
    ng = S5_X // LANES
    for b in range(nb):
        bu = _dot(u_ref[b].astype(BF16), bcat_ref[...])
        for k in range(2 * ng):
            x_sc[k, b * stride:b * stride + steps, :] = bu[:, k * LANES:(k + 1) * LANES]
    gpc = ng // nch
    for c in range(nch):
        ks = list(range(c * gpc, (c + 1) * gpc))
        ar = [jnp.broadcast_to(a_ref[0:1, k * LANES:(k + 1) * LANES], (nb, LANES)) for k in ks]
        ai = [jnp.broadcast_to(a_ref[0:1, S5_X + k * LANES:S5_X + (k + 1) * LANES], (nb, LANES)) for k in ks]

        def body(t, carry):
            xr, xi = carry
            r = pl.ds(t, nb, stride=stride)
            nr, ni = [], []
            for n, k in enumerate(ks):
                vr = ar[n] * xr[n] - ai[n] * xi[n] + x_sc[k, r, :]
                vi = ar[n] * xi[n] + ai[n] * xr[n] + x_sc[ng + k, r, :]
                x_sc[k, r, :] = vr
                x_sc[ng + k, r, :] = vi
                nr.append(vr)
                ni.append(vi)
            return tuple(nr), tuple(ni)

        init = (tuple(st_sc[:, k * LANES:(k + 1) * LANES] for k in ks),
                tuple(st_sc[:, S5_X + k * LANES:S5_X + (k + 1) * LANES] for k in ks))
        xr, xi = lax.fori_loop(0, steps, body, init, unroll=8)
        for n, k in enumerate(ks):
            st_sc[:, k * LANES:(k + 1) * LANES] = xr[n]
            st_sc[:, S5_X + k * LANES:S5_X + (k + 1) * LANES] = xi[n]

    for b in range(nb):
        xs = jnp.concatenate([x_sc[k, b * stride:b * stride + steps, :] for k in range(2 * ng)], axis=-1)
        y = _dot(xs.astype(BF16), ccat_ref[...]) + d_ref[...] * u_ref[b]
        y = _gelu(y)
        o_ref[b] = (y * _sigmoid(_dot(y.astype(BF16), wglu_ref[...]))).astype(BF16)


def _s5(u, bcat, ccat, a_cat, d, wglu, steps):
    nb, L, _ = u.shape
    return pl.pallas_call(
        functools.partial(_s5_kernel, steps=steps, nb=nb, nch=2),
        grid=(L // steps,),
        in_specs=[pl.BlockSpec((nb, steps, S5_WIDTH), lambda i: (0, i, 0)), _const_spec(bcat.shape),
                  _const_spec(ccat.shape), _const_spec(a_cat.shape), _const_spec(d.shape),
                  _const_spec(wglu.shape)],
        out_specs=pl.BlockSpec((nb, steps, S5_WIDTH), lambda i: (0, i, 0)),
        out_shape=jax.ShapeDtypeStruct((nb, L, S5_WIDTH), BF16),
        scratch_shapes=[pltpu.VMEM((2 * S5_X // LANES, nb * (steps + S5_PAD), LANES), F32),
                        pltpu.VMEM((nb, 2 * S5_X), F32)],
        compiler_params=_params("arbitrary"),
        name="s5",
    )(u, bcat, ccat, a_cat, d, wglu)


SIGN_BIT = np.int32(-2 ** 31)
SB_STOP_LOG2 = 170.0


def _sb_softplus2(z):
    neg_abs = lax.bitcast_convert_type(lax.bitcast_convert_type(z, jnp.int32) | SIGN_BIT, F32)
    return jnp.maximum(z, 0.0) + jnp.log2(1.0 + jnp.exp2(neg_abs))


def _sb_suffix(spb, tri_ref):
    half = tri_ref.shape[1]
    return jnp.concatenate([_dot(spb, tri_ref[...]), _dot(spb[:, half:], tri_ref[0:half, :])], axis=-1)


def _sb_kernel(q_ref, kv_ref, tri1_ref, tri_ref, o_ref, acc_sc, c_sc, z_sc, w_sc, *, tq, tk1, tk, nt):
    g = pl.program_id(2)

    def kv_block(j):
        return kv_ref[0, pl.ds(pl.multiple_of(j * tk, tk), tk), :]

    def first_step(h):
        i = g * nt + h
        q = q_ref[0, h * tq:(h + 1) * tq, :]
        start = jnp.maximum((i + 1) * tq - tk1, 0)
        kv1 = kv_ref[0, pl.ds(pl.multiple_of(start, tk), tk1), :]
        z = _dot_nt(q, kv1)
        t = i * tq + lax.broadcasted_iota(jnp.int32, (tq, tk1), 0)
        causal = (start + lax.broadcasted_iota(jnp.int32, (tq, tk1), 1)) < t
        spb = jnp.where(causal, _sb_softplus2(z), 0.0).astype(BF16)
        r = _sb_suffix(spb, tri1_ref)
        w = jnp.where(causal, jnp.exp2(jnp.minimum(z + r, 0.0)), 0.0)
        acc_sc[h] = _dot(w.astype(BF16), kv1)
        c_sc[h] = jnp.broadcast_to(r[:, 0:1], (tq, LANES))
        j0 = start // tk - 1
        z_sc[h] = _dot_nt(q, kv_block(jnp.maximum(j0, 0)))
        w_sc[h] = jnp.zeros((tq, tk), BF16)
        return j0

    j0s = [first_step(h) for h in range(nt)]
    cmaxs = [jnp.max(c_sc[h]) for h in range(nt)]

    def cond(carry):
        j, cmax = carry
        return (j >= 0) & (cmax > -SB_STOP_LOG2)

    lane = lax.broadcasted_iota(jnp.int32, (1, LANES), 1)
    for h in range(nt):
        q = q_ref[0, h * tq:(h + 1) * tq, :]

        def body(carry, h=h, q=q):
            j, _ = carry
            z = z_sc[h]
            z_next = _dot_nt(q, kv_block(jnp.maximum(j - 1, 0)))
            acc_sc[h] += _dot(w_sc[h], kv_block(j + 1))
            r = _sb_suffix(_sb_softplus2(z).astype(BF16), tri_ref)
            c = c_sc[h]
            cw = jnp.concatenate([c] * (tk // LANES), axis=-1)
            w_sc[h] = jnp.exp2(jnp.minimum(z + r + cw, 0.0)).astype(BF16)
            z_sc[h] = z_next
            c_sc[h] = c + r[:, 0:1]
            return j - 1, jnp.max(c_sc[h])

        j_next, _ = lax.while_loop(cond, body, (j0s[h], cmaxs[h]))

        @pl.when(j_next < j0s[h])
        def _(h=h, j_next=j_next):
            acc_sc[h] += _dot(w_sc[h], kv_block(j_next + 1))

        o_ref[0, h * tq:(h + 1) * tq, :] = jnp.where(lane >= HEAD_DIM, acc_sc[h], 0.0).astype(BF16)


def _sb(sbq, sbkv, tri1, tri, tq, tk1, tk, nt):
    B, L, _ = sbq.shape
    assert tq % tk == 0 and tk1 % tk == 0 and L >= tk1
    return pl.pallas_call(
        functools.partial(_sb_kernel, tq=tq, tk1=tk1, tk=tk, nt=nt),
        grid=(B, SB_HEADS, L // (nt * tq)),
        in_specs=[pl.BlockSpec((1, nt * tq, LANES), lambda b, h, i: (b, i, h)),
                  pl.BlockSpec((1, L, LANES), lambda b, h, i: (b, 0, h)),
                  _const_spec(tri1.shape), _const_spec(tri.shape)],
        out_specs=pl.BlockSpec((1, nt * tq, LANES), lambda b, h, i: (b, i, h)),
        out_shape=jax.ShapeDtypeStruct((B, L, SB_HEADS * LANES), BF16),
        scratch_shapes=[pltpu.VMEM((nt, tq, LANES), F32), pltpu.VMEM((nt, tq, LANES), F32),
                        pltpu.VMEM((nt, tq, tk), F32), pltpu.VMEM((nt, tq, tk), BF16)],
        compiler_params=_params("parallel", "parallel", "arbitrary"),
        name="sb_attn",
    )(sbq, sbkv, tri1, tri)


FFN_CHUNK = 1024
HALO = 8


def _merge_ffn_kernel(x_ref, oa_ref, ob_ref, oc_ref, mg_ref, wa_ref, wb_ref, wc_ref, wout_ref, gain_ref,
                      wup_ref, cw_ref, cb_ref, wdown_ref, out_ref, ubuf, carry, *, tl, dff):
    @pl.when(pl.program_id(1) == 0)
    def _():
        carry[...] = jnp.zeros_like(carry)

    d = x_ref.shape[2]
    merged = (mg_ref[0, :, 0:d].astype(F32) * _dot(oa_ref[0], wa_ref[...])
              + mg_ref[0, :, d:2 * d].astype(F32) * _dot(ob_ref[0], wb_ref[...])
              + mg_ref[0, :, 2 * d:3 * d].astype(F32) * _dot(oc_ref[0], wc_ref[...]))
    x1 = x_ref[0] + _dot(merged.astype(BF16), wout_ref[...])
    h = x1 * lax.rsqrt(jnp.mean(x1 * x1, axis=-1, keepdims=True) + NORM_EPS) * gain_ref[...]
    hb = h.astype(BF16)

    def conv_cols(off):
        cols = slice(off, off + FFN_CHUNK)
        u = _dot(hb, wup_ref[:, cols])
        ubuf[0:HALO, :] = carry[:, cols]
        ubuf[HALO:HALO + tl, :] = u
        carry[:, cols] = u[tl - HALO:tl, :]
        return (cw_ref[2:3, cols] * u + cw_ref[1:2, cols] * ubuf[HALO - 1:HALO - 1 + tl, :]
                + cw_ref[0:1, cols] * ubuf[HALO - 2:HALO - 2 + tl, :] + cb_ref[:, cols])

    acc = x1
    for j in range(dff // FFN_CHUNK):
        gate = conv_cols(j * FFN_CHUNK)
        val = conv_cols(dff + j * FFN_CHUNK)
        act = (_gelu(gate) * val).astype(BF16)
        acc = acc + _dot(act, wdown_ref[j * FFN_CHUNK:(j + 1) * FFN_CHUNK, :])
    out_ref[0] = acc


def _merge_ffn(x, oa, ob, oc, mg, wa, wb, wc, wout, gain, wup, cw, cb, wdown, tl):
    B, L, D = x.shape
    dff = wdown.shape[0]
    row = lambda b, l: (b, l, 0)
    weights = [wa, wb, wc, wout, gain, wup, cw, cb, wdown]
    return pl.pallas_call(
        functools.partial(_merge_ffn_kernel, tl=tl, dff=dff),
        grid=(B, L // tl),
        in_specs=[pl.BlockSpec((1, tl, D), row), pl.BlockSpec((1, tl, oa.shape[2]), row),
                  pl.BlockSpec((1, tl, ob.shape[2]), row), pl.BlockSpec((1, tl, oc.shape[2]), row),
                  pl.BlockSpec((1, tl, mg.shape[2]), row)] + [_const_spec(w.shape) for w in weights],
        out_specs=pl.BlockSpec((1, tl, D), row),
        out_shape=jax.ShapeDtypeStruct((B, L, D), F32),
        scratch_shapes=[pltpu.VMEM((HALO + tl, FFN_CHUNK), F32), pltpu.VMEM((HALO, 2 * dff), F32)],
        compiler_params=_params("parallel", "arbitrary"),
        name="merge_ffn",
    )(x, oa, ob, oc, mg, *weights)


def _pad_heads(w, n_heads):
    d = w.shape[0]
    w = w.reshape(d, n_heads, HEAD_DIM)
    return jnp.concatenate([w, jnp.zeros_like(w)], axis=-1).reshape(d, n_heads * LANES)


def _pack_w_in(w):
    d = w.shape[0]
    o_g = 640
    o_s5 = o_g + NSA_HEADS * 3
    o_sb = o_s5 + S5_WIDTH
    o_mg = o_sb + 3 * SB_HEADS * HEAD_DIM
    hw = SB_HEADS * HEAD_DIM
    gates = jnp.pad(w[:, o_g:o_s5], ((0, 0), (0, LANES - NSA_HEADS * 3)))
    sbk = w[:, o_sb + hw:o_sb + 2 * hw].reshape(d, SB_HEADS, 1, HEAD_DIM)
    sbv = w[:, o_sb + 2 * hw:o_mg].reshape(d, SB_HEADS, 1, HEAD_DIM)
    sbkv = jnp.concatenate([sbk, sbv], axis=2).reshape(d, SB_HEADS * LANES)
    packed = jnp.concatenate([_pad_heads(w[:, 0:256], NSA_HEADS), w[:, 256:o_g], gates, w[:, o_s5:o_sb],
                              _pad_heads(w[:, o_sb:o_sb + hw], SB_HEADS), sbkv, w[:, o_mg:]], axis=1)
    return packed.astype(BF16)


def _pad_rows(w, n_heads):
    d = w.shape[1]
    w = w.reshape(n_heads, HEAD_DIM, d)
    return jnp.concatenate([jnp.zeros_like(w), w], axis=1).reshape(n_heads * LANES, d).astype(BF16)


def _pad_gain(g, fill):
    return jnp.concatenate([g, jnp.full_like(g, fill)], axis=-1)


def _cmp_weights(wk, wv):
    z = jnp.zeros_like(wk)
    wkv = jnp.concatenate([jnp.concatenate([wk, z], axis=2), jnp.concatenate([z, wv], axis=2)], axis=1)
    half = CMP_BLOCK // 2
    wt = wkv[:half].reshape(half * LANES, LANES)
    wb = wkv[half:].reshape(half * LANES, LANES)
    return wt.astype(BF16), wb.astype(BF16)


def _s5_params(a_re, a_im, log_dt, b_re, b_im, c_re, c_im):
    dt = jnp.exp(log_dt.astype(F32))[:, None]
    A = lax.complex(a_re.astype(F32), a_im.astype(F32))
    A_bar = jnp.exp(dt * A)
    B_bar = ((A_bar - 1.0) / A)[..., None] * lax.complex(b_re.astype(F32), b_im.astype(F32))
    eye = jnp.eye(S5_GROUPS, dtype=F32)

    def bdiag_in(m):
        return jnp.einsum('gpc,gh->gchp', m, eye).reshape(S5_WIDTH, S5_X)

    def bdiag_out(m):
        return jnp.einsum('gcp,gh->gphc', m, eye).reshape(S5_X, S5_WIDTH)

    bcat = jnp.concatenate([bdiag_in(jnp.real(B_bar)), bdiag_in(jnp.imag(B_bar))], axis=1).astype(BF16)
    ccat = jnp.concatenate([bdiag_out(c_re.astype(F32)), -bdiag_out(c_im.astype(F32))], axis=0).astype(BF16)
    a_cat = jnp.concatenate([jnp.real(A_bar).reshape(1, S5_X), jnp.imag(A_bar).reshape(1, S5_X)], axis=1)
    return bcat, ccat, a_cat


def _overlap_matrix(n_rows, n_cmp, n_sel):
    cs = np.arange(n_rows)[:, None] * CMP_STRIDE
    ss = np.arange(n_sel)[None, :] * SEL_BLOCK
    ov = np.clip(np.minimum(cs + CMP_BLOCK, ss + SEL_BLOCK) - np.maximum(cs, ss), 0, None) / CMP_BLOCK
    ov[n_cmp:] = 0.0
    return jnp.asarray(ov, dtype=BF16)


def _sb_suffix_matrix(tk):
    u = (np.arange(tk)[:, None] >= np.arange(tk)[None, :]).astype(np.float32)
    return jnp.asarray(-np.concatenate([u, np.ones_like(u)], axis=0), dtype=BF16)


def kernel(x, norm_mix, w_in, nsa_q_gain, nsa_k_gain, cmp_pe, cmp_wk, cmp_wv, s5_a_re, s5_a_im, s5_log_dt,
           s5_b_re, s5_b_im, s5_c_re, s5_c_im, s5_d, s5_w_glu, w_br_nsa, w_br_s5, w_br_sb, w_out, norm_ffn,
           w_up, conv_w, conv_b, w_down):
    B, L, D = x.shape
    depth = w_in.shape[0]
    tq = 128
    tq_cmp = 256
    tl_in = min(512, L)
    tl_ffn = min(512, L)
    sb_tq, sb_tk1, sb_tk, sb_tiles = 256, 512, 256, 2
    s5_steps = min(256, L)
    assert L % tl_in == 0 and L % (sb_tq * sb_tiles) == 0 and L >= WINDOW + tq and D == D_MODEL
    n_cmp = (L - CMP_BLOCK) // CMP_STRIDE + 1
    n_rows = L // CMP_STRIDE
    n_sel = L // SEL_BLOCK
    nq = L // tq
    nwords = (nq + 31) // 32
    ov = _overlap_matrix(n_rows, n_cmp, n_sel)
    tri1, tri = _sb_suffix_matrix(sb_tk1 // 2), _sb_suffix_matrix(sb_tk // 2)
    wbias = _window_bias(tq)

    def pack_layer(p):
        wt, wb = _cmp_weights(p['cmp_wk'][0], p['cmp_wv'][0])
        bcat, ccat, a_cat = _s5_params(p['s5_a_re'], p['s5_a_im'], p['s5_log_dt'], p['s5_b_re'], p['s5_b_im'],
                                       p['s5_c_re'], p['s5_c_im'])
        return dict(
            norm_mix=p['norm_mix'][None, :], w_in=_pack_w_in(p['w_in']),
            qg=_pad_gain(p['nsa_q_gain'][None, :], 0.0), kg=_pad_gain(p['nsa_k_gain'], 1.0),
            pe=jnp.concatenate([p['cmp_pe'], p['cmp_pe']], axis=-1).reshape(2, (CMP_BLOCK // 2) * LANES),
            wt=wt, wb=wb, bcat=bcat, ccat=ccat, a_cat=a_cat, d=p['s5_d'].reshape(1, S5_WIDTH),
            wglu=p['s5_w_glu'].astype(BF16), wa=_pad_rows(p['w_br_nsa'], NSA_HEADS),
            wb5=p['w_br_s5'].astype(BF16), wc=_pad_rows(p['w_br_sb'], SB_HEADS), wout=p['w_out'].astype(BF16),
            norm_ffn=p['norm_ffn'][None, :], wup=p['w_up'].astype(BF16), cw=p['conv_w'],
            cb=p['conv_b'][None, :], wdown=p['w_down'].astype(BF16))

    packed = jax.vmap(pack_layer)(dict(
        norm_mix=norm_mix, w_in=w_in, nsa_q_gain=nsa_q_gain, nsa_k_gain=nsa_k_gain, cmp_pe=cmp_pe, cmp_wk=cmp_wk,
        cmp_wv=cmp_wv, s5_a_re=s5_a_re, s5_a_im=s5_a_im, s5_log_dt=s5_log_dt, s5_b_re=s5_b_re, s5_b_im=s5_b_im,
        s5_c_re=s5_c_re, s5_c_im=s5_c_im, s5_d=s5_d, s5_w_glu=s5_w_glu, w_br_nsa=w_br_nsa, w_br_s5=w_br_s5,
        w_br_sb=w_br_sb, w_out=w_out, norm_ffn=norm_ffn, w_up=w_up, conv_w=conv_w, conv_b=conv_b, w_down=w_down))

    for i in range(depth):
        p = {k: v[i] for k, v in packed.items()}
        qn, ckv, skv, wkv, gates, u_s5, sbq, sbkv, mg = _inproj(
            x, p['norm_mix'], p['w_in'], p['qg'], p['kg'][1:3], tl_in)

        kvc = _cmpkv(ckv.reshape(B, n_rows, CMP_STRIDE * LANES), p['pe'], p['wt'], p['wb'], p['kg'][0:1])
        ocmp, idx, flags = _nsa_cmp(qn, kvc, gates, ov, tq_cmp, tq, n_cmp)
        fl = jnp.pad(flags[:, :, 0, :nq], ((0, 0), (0, 0), (0, nwords * 32 - nq))).reshape(B, nq, nwords, 32)
        bits = jnp.sum(fl << jnp.arange(32, dtype=jnp.int32), axis=-1, dtype=jnp.int32).reshape(-1)
        o_a = _nsa_sw(bits, qn, skv, wkv, idx, gates, ocmp, wbias, tq, nwords)

        o_b = _s5(u_s5, p['bcat'], p['ccat'], p['a_cat'], p['d'], p['wglu'], s5_steps)

        o_c = _sb(sbq, sbkv, tri1, tri, sb_tq, sb_tk1, sb_tk, sb_tiles)

        x = _merge_ffn(x, o_a, o_b, o_c, mg, p['wa'], p['wb5'], p['wc'], p['wout'], p['norm_ffn'],
                       p['wup'], p['cw'], p['cb'], p['wdown'], tl_ffn)
    return x
```

```python
import functools
import math

import numpy as np
import jax
import jax.numpy as jnp
from jax import lax
from jax.experimental import pallas as pl
from jax.experimental.pallas import tpu as pltpu

HEAD_DIM = 64
NSA_HEADS = 4
CMP_BLOCK = 32
CMP_STRIDE = 16
SEL_BLOCK = 32
SEL_TOPK = 4
WINDOW = 512
FORCE_SCORE = 1.0e4
S5_WIDTH = 256
S5_GROUP = 16
S5_GROUPS = S5_WIDTH // S5_GROUP
S5_STATE = 64
SB_HEADS = 4
NORM_EPS = 1e-6
NEG_INF = -1e30
LANES = 128
VMEM_LIMIT = 56 * 1024 * 1024

F32 = jnp.float32
BF16 = jnp.bfloat16


def _log2(n):
    assert n & (n - 1) == 0
    return n.bit_length() - 1


SEL_SHIFT = _log2(SEL_BLOCK)
LOG2E = math.log2(math.e)


def _dot(a, b):
    return jnp.dot(a, b, preferred_element_type=F32)


def _dot_nt(a, b):
    return lax.dot_general(a, b, (((1,), (1,)), ((), ())), preferred_element_type=F32)


def _const_spec(shape):
    nd = len(shape)
    return pl.BlockSpec(shape, lambda *_: (0,) * nd, pipeline_mode=pl.Buffered(1))


def _params(*sem):
    return pltpu.CompilerParams(dimension_semantics=sem, vmem_limit_bytes=VMEM_LIMIT)


def _gelu(x):
    return 0.5 * x * (1.0 + jnp.tanh(math.sqrt(2.0 / math.pi) * (x + 0.044715 * (x * x * x))))


def _sigmoid(x):
    return 1.0 / (1.0 + jnp.exp(-x))


def _split3(x):
    h1 = x.astype(BF16)
    r1 = x - h1.astype(F32)
    h2 = r1.astype(BF16)
    h3 = (r1 - h2.astype(F32)).astype(BF16)
    return h1, h2, h3


D_MODEL = 1024
N_BRANCHES = 3
HEADS_W = NSA_HEADS * LANES
C_QN = 0
C_KV = C_QN + HEADS_W
C_G = C_KV + 3 * LANES
C_S5 = C_G + LANES
C_SBQ = C_S5 + S5_WIDTH
C_SBKV = C_SBQ + HEADS_W
C_MG = C_SBKV + HEADS_W
C_END = C_MG + N_BRANCHES * D_MODEL


def _inproj_kernel(x_ref, gain_ref, w_ref, qg_ref, kg_ref,
                   qn_ref, ckv_ref, skv_ref, wkv_ref, g_ref, s5_ref, sbq_ref, sbkv_ref, mg_ref):
    x = x_ref[0]
    h = x * lax.rsqrt(jnp.mean(x * x, axis=-1, keepdims=True) + NORM_EPS) * gain_ref[...]
    hb = h.astype(BF16)
    lane = lax.broadcasted_iota(jnp.int32, (1, LANES), 1)
    lo = lane < HEAD_DIM

    seg = _dot(hb, w_ref[:, C_QN:C_S5])
    scale = HEAD_DIM ** -0.5 * LOG2E
    for hd in range(NSA_HEADS):
        q = seg[:, hd * LANES:(hd + 1) * LANES]
        ms = jnp.sum(q * q, axis=-1, keepdims=True) * (1.0 / HEAD_DIM)
        qn = q * lax.rsqrt(ms + NORM_EPS) * qg_ref[...] * scale
        qn_ref[0, :, hd * LANES:(hd + 1) * LANES] = qn.astype(BF16)
    ckv_ref[0] = seg[:, C_KV:C_KV + LANES].astype(BF16)
    for n, ref in ((1, skv_ref), (2, wkv_ref)):
        kv = seg[:, C_KV + n * LANES:C_KV + (n + 1) * LANES]
        ms = jnp.sum(jnp.where(lo, kv * kv, 0.0), axis=-1, keepdims=True) * (1.0 / HEAD_DIM)
        sc = jnp.where(lo, lax.rsqrt(ms + NORM_EPS) * kg_ref[n - 1:n, :], 1.0)
        ref[0] = (kv * sc).astype(BF16)
    g_ref[0] = _sigmoid(seg[:, C_G:C_G + LANES])

    seg = _dot(hb, w_ref[:, C_S5:C_MG])
    s5_ref[0] = seg[:, 0:S5_WIDTH]
    sbq_ref[0] = (seg[:, C_SBQ - C_S5:C_SBKV - C_S5] * scale).astype(BF16)
    sbkv_ref[0] = seg[:, C_SBKV - C_S5:C_MG - C_S5].astype(BF16)

    for c in range(3):
        seg = _dot(hb, w_ref[:, C_MG + c * 1024:C_MG + (c + 1) * 1024])
        mg_ref[0, :, c * 1024:(c + 1) * 1024] = _sigmoid(seg).astype(BF16)


def _inproj(x, gain, w, qg, kg, tl):
    B, L, D = x.shape
    grid = (B, L // tl)
    row = lambda b, l: (b, l, 0)

    def out(n, dt):
        return jax.ShapeDtypeStruct((B, L, n), dt), pl.BlockSpec((1, tl, n), row)

    outs = [out(HEADS_W, BF16), out(LANES, BF16), out(LANES, BF16), out(LANES, BF16), out(LANES, F32),
            out(S5_WIDTH, F32), out(HEADS_W, BF16), out(HEADS_W, BF16), out(C_END - C_MG, BF16)]
    return pl.pallas_call(
        _inproj_kernel,
        grid=grid,
        in_specs=[pl.BlockSpec((1, tl, D), row), _const_spec((1, D)), _const_spec((D, C_END)),
                  _const_spec((1, LANES)), _const_spec((2, LANES))],
        out_specs=[o[1] for o in outs],
        out_shape=[o[0] for o in outs],
        compiler_params=_params("parallel", "parallel"),
        name="inproj",
    )(x, gain, w, qg, kg)


def _cmpkv_kernel(x_ref, pe_ref, wt_ref, wb_ref, kg_ref, o_ref):
    x = x_ref[0].astype(F32)
    a = _dot((x + pe_ref[0:1, :]).astype(BF16), wt_ref[...])
    b = _dot((x + pe_ref[1:2, :]).astype(BF16), wb_ref[...])
    n = x.shape[0]
    kv = a + pltpu.roll(b, n - 1, 0)
    lane = lax.broadcasted_iota(jnp.int32, (1, LANES), 1)
    lo = lane < HEAD_DIM
    ms = jnp.sum(jnp.where(lo, kv * kv, 0.0), axis=-1, keepdims=True) * (1.0 / HEAD_DIM)
    sc = jnp.where(lo, lax.rsqrt(ms + NORM_EPS) * kg_ref[...], 1.0)
    rowi = lax.broadcasted_iota(jnp.int32, (n, 1), 0)
    o_ref[0] = jnp.where(rowi < n - 1, kv * sc, 0.0).astype(BF16)


def _cmpkv(x16, pe2, wt, wb, kg):
    B, n, w = x16.shape
    return pl.pallas_call(
        _cmpkv_kernel,
        grid=(B,),
        in_specs=[pl.BlockSpec((1, n, w), lambda b: (b, 0, 0)), _const_spec((2, w)),
                  _const_spec((w, LANES)), _const_spec((w, LANES)), _const_spec((1, LANES))],
        out_specs=pl.BlockSpec((1, n, LANES), lambda b: (b, 0, 0)),
        out_shape=jax.ShapeDtypeStruct((B, n, LANES), BF16),
        compiler_params=_params("parallel"),
        name="cmpkv",
    )(x16, pe2, wt, wb, kg)


def _slope(hd):
    return 2.0 ** (-8.0 * (hd + 1) / NSA_HEADS) * LOG2E


def _nsa_cmp_kernel(q_ref, kvc_ref, g_ref, ov_ref, ocmp_ref, idx_ref, flag_ref, *, tq, tsel, n_cmp):
    qi = pl.program_id(1)
    nc_all, ns_all = ov_ref.shape
    need = (qi + 1) * (tq // CMP_STRIDE)
    bounds = list(range(LANES, nc_all, LANES)) + [nc_all]
    for v, nc in enumerate(bounds):
        lo = bounds[v - 1] if v else 0
        ns = min(ns_all, max(LANES, nc // (SEL_BLOCK // CMP_STRIDE)))
        pl.when((need > lo) & (need <= nc))(functools.partial(
            _nsa_cmp_body, q_ref, kvc_ref, g_ref, ov_ref, ocmp_ref, idx_ref, flag_ref, qi * tq,
            tq=tq, tsel=tsel, n_cmp=n_cmp, nc=nc, ns=ns))


def _nsa_cmp_body(q_ref, kvc_ref, g_ref, ov_ref, ocmp_ref, idx_ref, flag_ref, t0, *, tq, tsel, n_cmp, nc, ns):
    kvc = kvc_ref[0, 0:nc, :]
    g = g_ref[0]
    col = lax.broadcasted_iota(jnp.int32, (tq, nc), 1)
    t = t0 + lax.broadcasted_iota(jnp.int32, (tq, nc), 0)
    dist = t - (col * CMP_STRIDE + (CMP_BLOCK - 1))
    ok = (dist >= 0) & (col < n_cmp)
    distf = dist.astype(F32)
    psum = jnp.zeros((tq, nc), F32)
    for hd in range(NSA_HEADS):
        q = q_ref[0, :, hd * LANES:(hd + 1) * LANES]
        s = _dot_nt(q, kvc)
        s = jnp.where(ok, s - _slope(hd) * distf, NEG_INF)
        e = jnp.exp2(s - jnp.max(s, axis=-1, keepdims=True))
        p = jnp.where(ok, e * (1.0 / jnp.sum(e, axis=-1, keepdims=True)), 0.0)
        psum = psum + p
        o = _dot(p.astype(BF16), kvc)
        ocmp_ref[0, :, hd * LANES:(hd + 1) * LANES] = o * g[:, 3 * hd:3 * hd + 1]

    ov = ov_ref[0:nc, 0:ns]
    h1, h2, h3 = _split3(psum)
    imp = _dot(h1, ov) + _dot(h2, ov) + _dot(h3, ov)
    scol =lax.broadcasted_iota(jnp.int32, (tq, ns), 1)
    scolf = scol.astype(F32)
    trow = t0 + lax.broadcasted_iota(jnp.int32, (tq, ns), 0)
    free = (scol * SEL_BLOCK <= trow) & (scol != 0) & (scol != (trow >> SEL_SHIFT))
    score = jnp.where(free, imp, -1.0)
    lane = lax.broadcasted_iota(jnp.int32, (tq, LANES), 1)
    idx_out = jnp.zeros((tq, LANES), jnp.int32)
    hit = jnp.zeros((tq, LANES), F32)
    for k in range(SEL_TOPK - 2):
        m = jnp.max(score, axis=-1, keepdims=True)
        ikf = jnp.min(jnp.where(score == m, scolf, 1e9), axis=-1, keepdims=True)
        score = jnp.where(scolf == ikf, -3e38, score)
        ik = ikf.astype(jnp.int32)
        idx_out = jnp.where(lane == k, ik, idx_out)
        hit = jnp.where(lane == (ik >> _log2(tsel // SEL_BLOCK)), 1.0, hit)
    idx_ref[0] = idx_out
    for sub in range(tq // tsel):
        flag_ref[0, sub] = jnp.max(hit[sub * tsel:(sub + 1) * tsel], axis=0, keepdims=True).astype(jnp.int32)


def _nsa_cmp(qn, kvc, gates, ov, tq, tsel, n_cmp):
    B, L, _ = qn.shape
    nq = L // tq
    nsub = tq // tsel
    nc = kvc.shape[1]
    row = lambda b, i: (b, i, 0)
    return pl.pallas_call(
        functools.partial(_nsa_cmp_kernel, tq=tq, tsel=tsel, n_cmp=n_cmp),
        grid=(B, nq),
        in_specs=[pl.BlockSpec((1, tq, HEADS_W), row), pl.BlockSpec((1, nc, LANES), lambda b, i: (b, 0, 0)),
                  pl.BlockSpec((1, tq, LANES), row), _const_spec(ov.shape)],
        out_specs=[pl.BlockSpec((1, tq, HEADS_W), row), pl.BlockSpec((1, tq, LANES), row),
                   pl.BlockSpec((1, nsub, 1, LANES), lambda b, i: (b, i, 0, 0))],
        out_shape=[jax.ShapeDtypeStruct((B, L, HEADS_W), F32), jax.ShapeDtypeStruct((B, L, LANES), jnp.int32),
                   jax.ShapeDtypeStruct((B, L // tsel, 1, LANES), jnp.int32)],
        compiler_params=_params("parallel", "parallel"),
        name="nsa_cmp",
    )(qn, kvc, gates, ov)


def _nsa_sw_kernel(bits_ref, q_ref, skv_ref, wkv_ref, idx_ref, g_ref, ocmp_ref, wb_ref, o_ref,
                   m_sc, l_sc, acc_sc, win_sc, *, tq, nwords, nt):
    b = pl.program_id(0)
    nq = pl.num_programs(1) * nt
    rows = NSA_HEADS * tq
    lane_k = lax.broadcasted_iota(jnp.int32, (1, tq), 1)

    def tile_ctx(h):
        qi = pl.program_id(1) * nt + h
        r0 = h * tq
        qs = jnp.concatenate([q_ref[0, r0:r0 + tq, hd * LANES:(hd + 1) * LANES] for hd in range(NSA_HEADS)],
                             axis=0)
        rowi = lax.broadcasted_iota(jnp.int32, (rows, LANES), 0)
        ti = qi * tq + (rowi & (tq - 1))
        head = rowi >> _log2(tq)
        slope = jnp.where(head == 0, _slope(0), jnp.where(head == 1, _slope(1),
                          jnp.where(head == 2, _slope(2), _slope(3))))
        idx4 = jnp.concatenate([idx_ref[0, r0:r0 + tq, :]] * NSA_HEADS, axis=0)
        picks = [jnp.broadcast_to(idx4[:, k:k + 1], (rows, LANES)) for k in range(SEL_TOPK - 2)]
        return qi, qs, ti.astype(F32), ti >> SEL_SHIFT, slope, picks

    def scores(ctx, j, kv, forced_blocks, valid):
        _, qs, tf, cur_blk, slope, picks = ctx
        kpos = j * tq + lane_k
        blk = kpos >> SEL_SHIFT
        distf = tf - kpos.astype(F32)
        msk = (blk == picks[0]) | (blk == picks[1])
        if forced_blocks:
            min_dist = jnp.where(valid, 0.0, 3e38)
            msk = (msk | (blk == 0) | (blk == cur_blk)) & (distf >= min_dist)
        s = jnp.where(msk, _dot_nt(qs, kv) - slope * distf, NEG_INF)
        return s, msk

    def static_part(h):
        ctx = tile_ctx(h)
        qi, qs = ctx[0], ctx[1]
        jprev = jnp.maximum(qi - 1, 0)
        tiles = [(0, skv_ref[0, 0:tq, :], qi >= 0),
                 (jprev, skv_ref[0, pl.ds(pl.multiple_of(jprev * tq, tq), tq), :], qi >= 2),
                 (qi, skv_ref[0, pl.ds(pl.multiple_of(qi * tq, tq), tq), :], qi >= 1)]
        sm = [scores(ctx, j, kv, True, valid) for j, kv, valid in tiles]
        m0 = jnp.maximum(jnp.maximum(jnp.max(sm[0][0], axis=-1, keepdims=True),
                                     jnp.max(sm[1][0], axis=-1, keepdims=True)),
                         jnp.max(sm[2][0], axis=-1, keepdims=True))
        l0 = jnp.zeros((rows, 1), F32)
        acc0 = jnp.zeros((rows, LANES), F32)
        for (s, msk), (_, kv, _) in zip(sm, tiles):
            p = jnp.where(msk, jnp.exp2(s - m0), 0.0)
            l0 = l0 + jnp.sum(p, axis=-1, keepdims=True)
            acc0 = acc0 + _dot(p.astype(BF16), kv)
        m_sc[h] = jnp.broadcast_to(m0, (rows, LANES))
        l_sc[h] = jnp.broadcast_to(l0, (rows, LANES))
        acc_sc[h] = acc0

        t0 = qi * tq
        nband = WINDOW + tq
        start = pl.multiple_of(jnp.maximum(t0 - WINDOW, 0), tq)
        band = wkv_ref[0, pl.ds(start, nband), :]
        shift = (WINDOW - jnp.minimum(t0, WINDOW)) // LANES
        bias = jnp.concatenate([wb_ref[shift + k] for k in range(nband // LANES)], axis=-1)
        s = _dot_nt(qs, band) + bias
        e = jnp.exp2(s - jnp.max(s, axis=-1, keepdims=True))
        win_sc[h] = _dot(e.astype(BF16), band) * (1.0 / jnp.sum(e, axis=-1, keepdims=True))

    for h in range(nt):
        static_part(h)

    lane = lax.broadcasted_iota(jnp.int32, (1, LANES), 1)
    for h in range(nt):
        ctx = tile_ctx(h)
        qi = ctx[0]

        def sel_step(j, carry, h=h, ctx=ctx, qi=qi):
            word = bits_ref[(b * nq + qi) * nwords + (j >> 5)]
            needed = (lax.shift_right_logical(word, j & 31) & 1) == 1

            @pl.when(needed)
            def _():
                kv = skv_ref[0, pl.ds(pl.multiple_of(j * tq, tq), tq), :]
                s, msk = scores(ctx, j, kv, False, None)
                m_old = m_sc[h]
                m_new = jnp.maximum(m_old, jnp.max(s, axis=-1, keepdims=True))
                p = jnp.where(msk, jnp.exp2(s - m_new), 0.0)
                alpha = jnp.exp2(m_old - m_new)
                l_sc[h] = alpha * l_sc[h] + jnp.sum(p, axis=-1, keepdims=True)
                acc_sc[h] = alpha * acc_sc[h] + _dot(p.astype(BF16), kv)
                m_sc[h] = m_new
            return carry

        lax.fori_loop(1, jnp.maximum(qi - 1, 1), sel_step, 0)
        o_sel = acc_sc[h] / l_sc[h]
        o_win = win_sc[h]
        r0 = h * tq
        g = g_ref[0, r0:r0 + tq, :]
        for hd in range(NSA_HEADS):
            r = slice(hd * tq, (hd + 1) * tq)
            o = (ocmp_ref[0, r0:r0 + tq, hd * LANES:(hd + 1) * LANES]
                 + g[:, 3 * hd + 1:3 * hd + 2] * o_sel[r]
                 + g[:, 3 * hd + 2:3 * hd + 3] * o_win[r])
            o_ref[0, r0:r0 + tq, hd * LANES:(hd + 1) * LANES] = jnp.where(lane >= HEAD_DIM, o, 0.0).astype(BF16)


def _window_bias(tq):
    ncol = WINDOW + tq + WINDOW
    dist = np.arange(tq)[:, None] + WINDOW - np.arange(ncol)[None, :]
    ok = (dist >= 0) & (dist < WINDOW)
    tab = np.concatenate([np.where(ok, -_slope(hd) * dist, NEG_INF) for hd in range(NSA_HEADS)], axis=0)
    return jnp.asarray(tab.reshape(NSA_HEADS * tq, ncol // LANES, LANES).transpose(1, 0, 2), dtype=F32)


def _nsa_sw(bits, qn, skv, wkv, idx, gates, ocmp, wbias, tq, nwords, nt):
    B, L, _ = qn.shape
    row = lambda b, i, s: (b, i, 0)
    full = lambda b, i, s: (b, 0, 0)
    rows = NSA_HEADS * tq
    tr = nt * tq
    return pl.pallas_call(
        functools.partial(_nsa_sw_kernel, tq=tq, nwords=nwords, nt=nt),
        grid_spec=pltpu.PrefetchScalarGridSpec(
            num_scalar_prefetch=1,
            grid=(B, L // tr),
            in_specs=[pl.BlockSpec((1, tr, HEADS_W), row), pl.BlockSpec((1, L, LANES), full),
                      pl.BlockSpec((1, L, LANES), full), pl.BlockSpec((1, tr, LANES), row),
                      pl.BlockSpec((1, tr, LANES), row), pl.BlockSpec((1, tr, HEADS_W), row),
                      _const_spec(wbias.shape)],
            out_specs=pl.BlockSpec((1, tr, HEADS_W), row),
            scratch_shapes=[pltpu.VMEM((nt, rows, LANES), F32) for _ in range(4)],
        ),
        out_shape=jax.ShapeDtypeStruct((B, L, HEADS_W), BF16),
        compiler_params=_params("parallel", "arbitrary"),
        name="nsa_sw",
    )(bits, qn, skv, wkv, idx, gates, ocmp, wbias)


S5_X = S5_GROUPS * S5_STATE
S5_PAD = 8


def _s5_kernel(u_ref, bcat_ref, ccat_ref, a_ref, d_ref, wglu_ref, o_ref, x_sc, st_sc, *, steps, nb, nch):
    @pl.when(pl.program_id(0) == 0)
    def _():
        st_sc[...] = jnp.zeros_like(st_sc)

    stride = steps + S5_PAD
    ng = S5_X // LANES
    for b in range(nb):
        bu = _dot(u_ref[b].astype(BF16), bcat_ref[...])
        for k in range(2 * ng):
            x_sc[k, b * stride:b * stride + steps, :] = bu[:, k * LANES:(k + 1) * LANES]
    gpc = ng // nch
    for c in range(nch):
        ks = list(range(c * gpc, (c + 1) * gpc))
        ar = [jnp.broadcast_to(a_ref[0:1, k * LANES:(k + 1) * LANES], (nb, LANES)) for k in ks]
        ai = [jnp.broadcast_to(a_ref[0:1, S5_X + k * LANES:S5_X + (k + 1) * LANES], (nb, LANES)) for k in ks]

        def body(t, carry):
            xr, xi = carry
            r = pl.ds(t, nb, stride=stride)
            nr, ni = [], []
            for n, k in enumerate(ks):
                vr = ar[n] * xr[n] - ai[n] * xi[n] + x_sc[k, r, :]
                vi = ar[n] * xi[n] + ai[n] * xr[n] + x_sc[ng + k, r, :]
                x_sc[k, r, :] = vr
                x_sc[ng + k, r, :] = vi
                nr.append(vr)
                ni.append(vi)
            return tuple(nr), tuple(ni)

        init = (tuple(st_sc[:, k * LANES:(k + 1) * LANES] for k in ks),
                tuple(st_sc[:, S5_X + k * LANES:S5_X + (k + 1) * LANES] for k in ks))
        xr, xi = lax.fori_loop(0, steps, body, init, unroll=8)
        for n, k in enumerate(ks):
            st_sc[:, k * LANES:(k + 1) * LANES] = xr[n]
            st_sc[:, S5_X + k * LANES:S5_X + (k + 1) * LANES] = xi[n]

    for b in range(nb):
        xs = jnp.concatenate([x_sc[k, b * stride:b * stride + steps, :] for k in range(2 * ng)], axis=-1)
        y = _dot(xs.astype(BF16), ccat_ref[...]) + d_ref[...] * u_ref[b]
        y = _gelu(y)
        o_ref[b] = (y * _sigmoid(_dot(y.astype(BF16), wglu_ref[...]))).astype(BF16)


def _s5(u, bcat, ccat, a_cat, d, wglu, steps):
    nb, L, _ = u.shape
    return pl.pallas_call(
        functools.partial(_s5_kernel, steps=steps, nb=nb, nch=2),
        grid=(L // steps,),
        in_specs=[pl.BlockSpec((nb, steps, S5_WIDTH), lambda i: (0, i, 0)), _const_spec(bcat.shape),
                  _const_spec(ccat.shape), _const_spec(a_cat.shape), _const_spec(d.shape),
                  _const_spec(wglu.shape)],
        out_specs=pl.BlockSpec((nb, steps, S5_WIDTH), lambda i: (0, i, 0)),
        out_shape=jax.ShapeDtypeStruct((nb, L, S5_WIDTH), BF16),
        scratch_shapes=[pltpu.VMEM((2 * S5_X // LANES, nb * (steps + S5_PAD), LANES), F32),
                        pltpu.VMEM((nb, 2 * S5_X), F32)],
        compiler_params=_params("arbitrary"),
        name="s5",
    )(u, bcat, ccat, a_cat, d, wglu)


SIGN_BIT = np.int32(-2 ** 31)
SB_STOP_LOG2 = 170.0


def _sb_softplus2(z):
    neg_abs = lax.bitcast_convert_type(lax.bitcast_convert_type(z, jnp.int32) | SIGN_BIT, F32)
    return jnp.maximum(z, 0.0) + jnp.log2(1.0 + jnp.exp2(neg_abs))


def _sb_suffix(spb, tri_ref):
    half = tri_ref.shape[1]
    return jnp.concatenate([_dot(spb, tri_ref[...]), _dot(spb[:, half:], tri_ref[0:half, :])], axis=-1)


def _sb_kernel(q_ref, kv_ref, tri1_ref, tri_ref, o_ref, acc_sc, c_sc, z_sc, w_sc, *, tq, tk1, tk, nt):
    g = pl.program_id(2)

    def kv_block(j):
        return kv_ref[0, pl.ds(pl.multiple_of(j * tk, tk), tk), :]

    def first_step(h):
        i = g * nt + h
        q = q_ref[0, h * tq:(h + 1) * tq, :]
        start = jnp.maximum((i + 1) * tq - tk1, 0)
        kv1 = kv_ref[0, pl.ds(pl.multiple_of(start, tk), tk1), :]
        z = _dot_nt(q, kv1)
        t = i * tq + lax.broadcasted_iota(jnp.int32, (tq, tk1), 0)
        causal = (start + lax.broadcasted_iota(jnp.int32, (tq, tk1), 1)) < t
        spb = jnp.where(causal, _sb_softplus2(z), 0.0).astype(BF16)
        r = _sb_suffix(spb, tri1_ref)
        w = jnp.where(causal, jnp.exp2(jnp.minimum(z + r, 0.0)), 0.0)
        acc_sc[h] = _dot(w.astype(BF16), kv1)
        c_sc[h] = jnp.broadcast_to(r[:, 0:1], (tq, LANES))
        j0 = start // tk - 1
        z_sc[h] = _dot_nt(q, kv_block(jnp.maximum(j0, 0)))
        w_sc[h] = jnp.zeros((tq, tk), BF16)
        return j0

    j0s = [first_step(h) for h in range(nt)]
    cmaxs = [jnp.max(c_sc[h]) for h in range(nt)]

    def cond(carry):
        j, cmax = carry
        return (j >= 0) & (cmax > -SB_STOP_LOG2)

    lane = lax.broadcasted_iota(jnp.int32, (1, LANES), 1)
    for h in range(nt):
        q = q_ref[0, h * tq:(h + 1) * tq, :]

        def body(carry, h=h, q=q):
            j, _ = carry
            z = z_sc[h]
            z_next = _dot_nt(q, kv_block(jnp.maximum(j - 1, 0)))
            acc_sc[h] += _dot(w_sc[h], kv_block(j + 1))
            r = _sb_suffix(_sb_softplus2(z).astype(BF16), tri_ref)
            c = c_sc[h]
            cw = jnp.concatenate([c] * (tk // LANES), axis=-1)
            w_sc[h] = jnp.exp2(jnp.minimum(z + r + cw, 0.0)).astype(BF16)
            z_sc[h] = z_next
            c_sc[h] = c + r[:, 0:1]
            return j - 1, jnp.max(c_sc[h])

        j_next, _ = lax.while_loop(cond, body, (j0s[h], cmaxs[h]))

        @pl.when(j_next < j0s[h])
        def _(h=h, j_next=j_next):
            acc_sc[h] += _dot(w_sc[h], kv_block(j_next + 1))

        o_ref[0, h * tq:(h + 1) * tq, :] = jnp.where(lane >= HEAD_DIM, acc_sc[h], 0.0).astype(BF16)


def _sb(sbq, sbkv, tri1, tri, tq, tk1, tk, nt):
    B, L, _ = sbq.shape
    assert tq % tk == 0 and tk1 % tk == 0 and L >= tk1
    return pl.pallas_call(
        functools.partial(_sb_kernel, tq=tq, tk1=tk1, tk=tk, nt=nt),
        grid=(B, SB_HEADS, L // (nt * tq)),
        in_specs=[pl.BlockSpec((1, nt * tq, LANES), lambda b, h, i: (b, i, h)),
                  pl.BlockSpec((1, L, LANES), lambda b, h, i: (b, 0, h)),
                  _const_spec(tri1.shape), _const_spec(tri.shape)],
        out_specs=pl.BlockSpec((1, nt * tq, LANES), lambda b, h, i: (b, i, h)),
        out_shape=jax.ShapeDtypeStruct((B, L, SB_HEADS * LANES), BF16),
        scratch_shapes=[pltpu.VMEM((nt, tq, LANES), F32), pltpu.VMEM((nt, tq, LANES), F32),
                        pltpu.VMEM((nt, tq, tk), F32), pltpu.VMEM((nt, tq, tk), BF16)],
        compiler_params=_params("parallel", "parallel", "arbitrary"),
        name="sb_attn",
    )(sbq, sbkv, tri1, tri)


FFN_CHUNK = 1024
HALO = 8


def _merge_ffn_kernel(x_ref, oa_ref, ob_ref, oc_ref, mg_ref, wa_ref, wb_ref, wc_ref, wout_ref, gain_ref,
                      wup_ref, cw_ref, cb_ref, wdown_ref, out_ref, ubuf, carry, *, tl, dff):
    @pl.when(pl.program_id(1) == 0)
    def _():
        carry[...] = jnp.zeros_like(carry)

    d = x_ref.shape[2]
    merged = (mg_ref[0, :, 0:d].astype(F32) * _dot(oa_ref[0], wa_ref[...])
              + mg_ref[0, :, d:2 * d].astype(F32) * _dot(ob_ref[0], wb_ref[...])
              + mg_ref[0, :, 2 * d:3 * d].astype(F32) * _dot(oc_ref[0], wc_ref[...]))
    x1 = x_ref[0] + _dot(merged.astype(BF16), wout_ref[...])
    h = x1 * lax.rsqrt(jnp.mean(x1 * x1, axis=-1, keepdims=True) + NORM_EPS) * gain_ref[...]
    hb = h.astype(BF16)

    def conv_cols(off):
        cols = slice(off, off + FFN_CHUNK)
        u = _dot(hb, wup_ref[:, cols])
        ubuf[0:HALO, :] = carry[:, cols]
        ubuf[HALO:HALO + tl, :] = u
        carry[:, cols] = u[tl - HALO:tl, :]
        return (cw_ref[2:3, cols] * u + cw_ref[1:2, cols] * ubuf[HALO - 1:HALO - 1 + tl, :]
                + cw_ref[0:1, cols] * ubuf[HALO - 2:HALO - 2 + tl, :] + cb_ref[:, cols])

    acc = x1
    for j in range(dff // FFN_CHUNK):
        gate = conv_cols(j * FFN_CHUNK)
        val = conv_cols(dff + j * FFN_CHUNK)
        act = (_gelu(gate) * val).astype(BF16)
        acc = acc + _dot(act, wdown_ref[j * FFN_CHUNK:(j + 1) * FFN_CHUNK, :])
    out_ref[0] = acc


def _merge_ffn(x, oa, ob, oc, mg, wa, wb, wc, wout, gain, wup, cw, cb, wdown, tl):
    B, L, D = x.shape
    dff = wdown.shape[0]
    row = lambda b, l: (b, l, 0)
    weights = [wa, wb, wc, wout, gain, wup, cw, cb, wdown]
    return pl.pallas_call(
        functools.partial(_merge_ffn_kernel, tl=tl, dff=dff),
        grid=(B, L // tl),
        in_specs=[pl.BlockSpec((1, tl, D), row), pl.BlockSpec((1, tl, oa.shape[2]), row),
                  pl.BlockSpec((1, tl, ob.shape[2]), row), pl.BlockSpec((1, tl, oc.shape[2]), row),
                  pl.BlockSpec((1, tl, mg.shape[2]), row)] + [_const_spec(w.shape) for w in weights],
        out_specs=pl.BlockSpec((1, tl, D), row),
        out_shape=jax.ShapeDtypeStruct((B, L, D), F32),
        scratch_shapes=[pltpu.VMEM((HALO + tl, FFN_CHUNK), F32), pltpu.VMEM((HALO, 2 * dff), F32)],
        compiler_params=_params("parallel", "arbitrary"),
        name="merge_ffn",
    )(x, oa, ob, oc, mg, *weights)


def _pad_heads(w, n_heads):
    d = w.shape[0]
    w = w.reshape(d, n_heads, HEAD_DIM)
    return jnp.concatenate([w, jnp.zeros_like(w)], axis=-1).reshape(d, n_heads * LANES)


def _pack_w_in(w):
    d = w.shape[0]
    o_g = 640
    o_s5 = o_g + NSA_HEADS * 3
    o_sb = o_s5 + S5_WIDTH
    o_mg = o_sb + 3 * SB_HEADS * HEAD_DIM
    hw = SB_HEADS * HEAD_DIM
    gates = jnp.pad(w[:, o_g:o_s5], ((0, 0), (0, LANES - NSA_HEADS * 3)))
    sbk = w[:, o_sb + hw:o_sb + 2 * hw].reshape(d, SB_HEADS, 1, HEAD_DIM)
    sbv = w[:, o_sb + 2 * hw:o_mg].reshape(d, SB_HEADS, 1, HEAD_DIM)
    sbkv = jnp.concatenate([sbk, sbv], axis=2).reshape(d, SB_HEADS * LANES)
    packed = jnp.concatenate([_pad_heads(w[:, 0:256], NSA_HEADS), w[:, 256:o_g], gates, w[:, o_s5:o_sb],
                              _pad_heads(w[:, o_sb:o_sb + hw], SB_HEADS), sbkv, w[:, o_mg:]], axis=1)
    return packed.astype(BF16)


def _pad_rows(w, n_heads):
    d = w.shape[1]
    w = w.reshape(n_heads, HEAD_DIM, d)
    return jnp.concatenate([jnp.zeros_like(w), w], axis=1).reshape(n_heads * LANES, d).astype(BF16)


def _pad_gain(g, fill):
    return jnp.concatenate([g, jnp.full_like(g, fill)], axis=-1)


def _cmp_weights(wk, wv):
    z = jnp.zeros_like(wk)
    wkv = jnp.concatenate([jnp.concatenate([wk, z], axis=2), jnp.concatenate([z, wv], axis=2)], axis=1)
    half = CMP_BLOCK // 2
    wt = wkv[:half].reshape(half * LANES, LANES)
    wb = wkv[half:].reshape(half * LANES, LANES)
    return wt.astype(BF16), wb.astype(BF16)


def _s5_params(a_re, a_im, log_dt, b_re, b_im, c_re, c_im):
    dt = jnp.exp(log_dt.astype(F32))[:, None]
    A = lax.complex(a_re.astype(F32), a_im.astype(F32))
    A_bar = jnp.exp(dt * A)
    B_bar = ((A_bar - 1.0) / A)[..., None] * lax.complex(b_re.astype(F32), b_im.astype(F32))
    eye = jnp.eye(S5_GROUPS, dtype=F32)

    def bdiag_in(m):
        return jnp.einsum('gpc,gh->gchp', m, eye).reshape(S5_WIDTH, S5_X)

    def bdiag_out(m):
        return jnp.einsum('gcp,gh->gphc', m, eye).reshape(S5_X, S5_WIDTH)

    bcat = jnp.concatenate([bdiag_in(jnp.real(B_bar)), bdiag_in(jnp.imag(B_bar))], axis=1).astype(BF16)
    ccat = jnp.concatenate([bdiag_out(c_re.astype(F32)), -bdiag_out(c_im.astype(F32))], axis=0).astype(BF16)
    a_cat = jnp.concatenate([jnp.real(A_bar).reshape(1, S5_X), jnp.imag(A_bar).reshape(1, S5_X)], axis=1)
    return bcat, ccat, a_cat


def _overlap_matrix(n_rows, n_cmp, n_sel):
    cs = np.arange(n_rows)[:, None] * CMP_STRIDE
    ss = np.arange(n_sel)[None, :] * SEL_BLOCK
    ov = np.clip(np.minimum(cs + CMP_BLOCK, ss + SEL_BLOCK) - np.maximum(cs, ss), 0, None) / CMP_BLOCK
    ov[n_cmp:] = 0.0
    return jnp.asarray(ov, dtype=BF16)


def _sb_suffix_matrix(tk):
    u = (np.arange(tk)[:, None] >= np.arange(tk)[None, :]).astype(np.float32)
    return jnp.asarray(-np.concatenate([u, np.ones_like(u)], axis=0), dtype=BF16)


def kernel(x, norm_mix, w_in, nsa_q_gain, nsa_k_gain, cmp_pe, cmp_wk, cmp_wv, s5_a_re, s5_a_im, s5_log_dt,
           s5_b_re, s5_b_im, s5_c_re, s5_c_im, s5_d, s5_w_glu, w_br_nsa, w_br_s5, w_br_sb, w_out, norm_ffn,
           w_up, conv_w, conv_b, w_down):
    B, L, D = x.shape
    depth = w_in.shape[0]
    tq = 128
    tq_cmp = 256
    tl_in = min(512, L)
    tl_ffn = min(512, L)
    sb_tq, sb_tk1, sb_tk, sb_tiles = 256, 512, 256, 4
    s5_steps = min(256, L)
    assert L % tl_in == 0 and L % (sb_tq * sb_tiles) == 0 and L >= WINDOW + tq and D == D_MODEL
    n_cmp = (L - CMP_BLOCK) // CMP_STRIDE + 1
    n_rows = L // CMP_STRIDE
    n_sel = L // SEL_BLOCK
    nq = L // tq
    nwords = (nq + 31) // 32
    ov = _overlap_matrix(n_rows, n_cmp, n_sel)
    tri1, tri = _sb_suffix_matrix(sb_tk1 // 2), _sb_suffix_matrix(sb_tk // 2)
    wbias = _window_bias(tq)

    def pack_layer(p):
        wt, wb = _cmp_weights(p['cmp_wk'][0], p['cmp_wv'][0])
        bcat, ccat, a_cat = _s5_params(p['s5_a_re'], p['s5_a_im'], p['s5_log_dt'], p['s5_b_re'], p['s5_b_im'],
                                       p['s5_c_re'], p['s5_c_im'])
        return dict(
            norm_mix=p['norm_mix'][None, :], w_in=_pack_w_in(p['w_in']),
            qg=_pad_gain(p['nsa_q_gain'][None, :], 0.0), kg=_pad_gain(p['nsa_k_gain'], 1.0),
            pe=jnp.concatenate([p['cmp_pe'], p['cmp_pe']], axis=-1).reshape(2, (CMP_BLOCK // 2) * LANES),
            wt=wt, wb=wb, bcat=bcat, ccat=ccat, a_cat=a_cat, d=p['s5_d'].reshape(1, S5_WIDTH),
            wglu=p['s5_w_glu'].astype(BF16), wa=_pad_rows(p['w_br_nsa'], NSA_HEADS),
            wb5=p['w_br_s5'].astype(BF16), wc=_pad_rows(p['w_br_sb'], SB_HEADS), wout=p['w_out'].astype(BF16),
            norm_ffn=p['norm_ffn'][None, :], wup=p['w_up'].astype(BF16), cw=p['conv_w'],
            cb=p['conv_b'][None, :], wdown=p['w_down'].astype(BF16))

    packed = jax.vmap(pack_layer)(dict(
        norm_mix=norm_mix, w_in=w_in, nsa_q_gain=nsa_q_gain, nsa_k_gain=nsa_k_gain, cmp_pe=cmp_pe, cmp_wk=cmp_wk,
        cmp_wv=cmp_wv, s5_a_re=s5_a_re, s5_a_im=s5_a_im, s5_log_dt=s5_log_dt, s5_b_re=s5_b_re, s5_b_im=s5_b_im,
        s5_c_re=s5_c_re, s5_c_im=s5_c_im, s5_d=s5_d, s5_w_glu=s5_w_glu, w_br_nsa=w_br_nsa, w_br_s5=w_br_s5,
        w_br_sb=w_br_sb, w_out=w_out, norm_ffn=norm_ffn, w_up=w_up, conv_w=conv_w, conv_b=conv_b, w_down=w_down))

    for i in range(depth):
        p = {k: v[i] for k, v in packed.items()}
        qn, ckv, skv, wkv, gates, u_s5, sbq, sbkv, mg = _inproj(
            x, p['norm_mix'], p['w_in'], p['qg'], p['kg'][1:3], tl_in)

        kvc = _cmpkv(ckv.reshape(B, n_rows, CMP_STRIDE * LANES), p['pe'], p['wt'], p['wb'], p['kg'][0:1])
        ocmp, idx, flags = _nsa_cmp(qn, kvc, gates, ov, tq_cmp, tq, n_cmp)
        fl = jnp.pad(flags[:, :, 0, :nq], ((0, 0), (0, 0), (0, nwords * 32 - nq))).reshape(B, nq, nwords, 32)
        bits = jnp.sum(fl << jnp.arange(32, dtype=jnp.int32), axis=-1, dtype=jnp.int32).reshape(-1)
        o_a = _nsa_sw(bits, qn, skv, wkv, idx, gates, ocmp, wbias, tq, nwords, 2)

        o_b = _s5(u_s5, p['bcat'], p['ccat'], p['a_cat'], p['d'], p['wglu'], s5_steps)

        o_c = _sb(sbq, sbkv, tri1, tri, sb_tq, sb_tk1, sb_tk, sb_tiles)

        x = _merge_ffn(x, o_a, o_b, o_c, mg, p['wa'], p['wb5'], p['wc'], p['wout'], p['norm_ffn'],
                       p['wup'], p['cw'], p['cb'], p['wdown'], tl_ffn)
    return x
```

```python
import functools
import math

import numpy as np
import jax
import jax.numpy as jnp
from jax import lax
from jax.experimental import pallas as pl
from jax.experimental.pallas import tpu as pltpu

HEAD_DIM = 64
NSA_HEADS = 4
CMP_BLOCK = 32
CMP_STRIDE = 16
SEL_BLOCK = 32
SEL_TOPK = 4
WINDOW = 512
FORCE_SCORE = 1.0e4
S5_WIDTH = 256
S5_GROUP = 16
S5_GROUPS = S5_WIDTH // S5_GROUP
S5_STATE = 64
SB_HEADS = 4
NORM_EPS = 1e-6
NEG_INF = -1e30
LANES = 128
VMEM_LIMIT = 56 * 1024 * 1024

F32 = jnp.float32
BF16 = jnp.bfloat16


def _log2(n):
    assert n & (n - 1) == 0
    return n.bit_length() - 1


SEL_SHIFT = _log2(SEL_BLOCK)
LOG2E = math.log2(math.e)


def _dot(a, b):
    return jnp.dot(a, b, preferred_element_type=F32)


def _dot_nt(a, b):
    return lax.dot_general(a, b, (((1,), (1,)), ((), ())), preferred_element_type=F32)


def _const_spec(shape):
    nd = len(shape)
    return pl.BlockSpec(shape, lambda *_: (0,) * nd, pipeline_mode=pl.Buffered(1))


def _params(*sem):
    return pltpu.CompilerParams(dimension_semantics=sem, vmem_limit_bytes=VMEM_LIMIT)


def _gelu(x):
    return 0.5 * x * (1.0 + jnp.tanh(math.sqrt(2.0 / math.pi) * (x + 0.044715 * (x * x * x))))


def _sigmoid(x):
    return 1.0 / (1.0 + jnp.exp(-x))


def _split3(x):
    h1 = x.astype(BF16)
    r1 = x - h1.astype(F32)
    h2 = r1.astype(BF16)
    h3 = (r1 - h2.astype(F32)).astype(BF16)
    return h1, h2, h3


D_MODEL = 1024
N_BRANCHES = 3
HEADS_W = NSA_HEADS * LANES
C_QN = 0
C_KV = C_QN + HEADS_W
C_G = C_KV + 3 * LANES
C_S5 = C_G + LANES
C_SBQ = C_S5 + S5_WIDTH
C_SBKV = C_SBQ + HEADS_W
C_MG = C_SBKV + HEADS_W
C_END = C_MG + N_BRANCHES * D_MODEL


def _inproj_kernel(x_ref, gain_ref, w_ref, qg_ref, kg_ref,
                   qn_ref, ckv_ref, skv_ref, wkv_ref, g_ref, s5_ref, sbq_ref, sbkv_ref, mg_ref):
    x = x_ref[0]
    h = x * lax.rsqrt(jnp.mean(x * x, axis=-1, keepdims=True) + NORM_EPS) * gain_ref[...]
    hb = h.astype(BF16)
    lane = lax.broadcasted_iota(jnp.int32, (1, LANES), 1)
    lo = lane < HEAD_DIM

    seg = _dot(hb, w_ref[:, C_QN:C_S5])
    scale = HEAD_DIM ** -0.5 * LOG2E
    for hd in range(NSA_HEADS):
        q = seg[:, hd * LANES:(hd + 1) * LANES]
        ms = jnp.sum(q * q, axis=-1, keepdims=True) * (1.0 / HEAD_DIM)
        qn = q * lax.rsqrt(ms + NORM_EPS) * qg_ref[...] * scale
        qn_ref[0, :, hd * LANES:(hd + 1) * LANES] = qn.astype(BF16)
    ckv_ref[0] = seg[:, C_KV:C_KV + LANES].astype(BF16)
    for n, ref in ((1, skv_ref), (2, wkv_ref)):
        kv = seg[:, C_KV + n * LANES:C_KV + (n + 1) * LANES]
        ms = jnp.sum(jnp.where(lo, kv * kv, 0.0), axis=-1, keepdims=True) * (1.0 / HEAD_DIM)
        sc = jnp.where(lo, lax.rsqrt(ms + NORM_EPS) * kg_ref[n - 1:n, :], 1.0)
        ref[0] = (kv * sc).astype(BF16)
    g_ref[0] = _sigmoid(seg[:, C_G:C_G + LANES])

    seg = _dot(hb, w_ref[:, C_S5:C_MG])
    s5_ref[0] = seg[:, 0:S5_WIDTH]
    sbq_ref[0] = (seg[:, C_SBQ - C_S5:C_SBKV - C_S5] * scale).astype(BF16)
    sbkv_ref[0] = seg[:, C_SBKV - C_S5:C_MG - C_S5].astype(BF16)

    for c in range(3):
        seg = _dot(hb, w_ref[:, C_MG + c * 1024:C_MG + (c + 1) * 1024])
        mg_ref[0, :, c * 1024:(c + 1) * 1024] = _sigmoid(seg).astype(BF16)


def _inproj(x, gain, w, qg, kg, tl):
    B, L, D = x.shape
    grid = (B, L // tl)
    row = lambda b, l: (b, l, 0)

    def out(n, dt):
        return jax.ShapeDtypeStruct((B, L, n), dt), pl.BlockSpec((1, tl, n), row)

    outs = [out(HEADS_W, BF16), out(LANES, BF16), out(LANES, BF16), out(LANES, BF16), out(LANES, F32),
            out(S5_WIDTH, F32), out(HEADS_W, BF16), out(HEADS_W, BF16), out(C_END - C_MG, BF16)]
    return pl.pallas_call(
        _inproj_kernel,
        grid=grid,
        in_specs=[pl.BlockSpec((1, tl, D), row), _const_spec((1, D)), _const_spec((D, C_END)),
                  _const_spec((1, LANES)), _const_spec((2, LANES))],
        out_specs=[o[1] for o in outs],
        out_shape=[o[0] for o in outs],
        compiler_params=_params("parallel", "parallel"),
        name="inproj",
    )(x, gain, w, qg, kg)


def _cmpkv_kernel(x_ref, pe_ref, wt_ref, wb_ref, kg_ref, o_ref):
    x = x_ref[0].astype(F32)
    a = _dot((x + pe_ref[0:1, :]).astype(BF16), wt_ref[...])
    b = _dot((x + pe_ref[1:2, :]).astype(BF16), wb_ref[...])
    n = x.shape[0]
    kv = a + pltpu.roll(b, n - 1, 0)
    lane = lax.broadcasted_iota(jnp.int32, (1, LANES), 1)
    lo = lane < HEAD_DIM
    ms = jnp.sum(jnp.where(lo, kv * kv, 0.0), axis=-1, keepdims=True) * (1.0 / HEAD_DIM)
    sc = jnp.where(lo, lax.rsqrt(ms + NORM_EPS) * kg_ref[...], 1.0)
    rowi = lax.broadcasted_iota(jnp.int32, (n, 1), 0)
    o_ref[0] = jnp.where(rowi < n - 1, kv * sc, 0.0).astype(BF16)


def _cmpkv(x16, pe2, wt, wb, kg):
    B, n, w = x16.shape
    return pl.pallas_call(
        _cmpkv_kernel,
        grid=(B,),
        in_specs=[pl.BlockSpec((1, n, w), lambda b: (b, 0, 0)), _const_spec((2, w)),
                  _const_spec((w, LANES)), _const_spec((w, LANES)), _const_spec((1, LANES))],
        out_specs=pl.BlockSpec((1, n, LANES), lambda b: (b, 0, 0)),
        out_shape=jax.ShapeDtypeStruct((B, n, LANES), BF16),
        compiler_params=_params("parallel"),
        name="cmpkv",
    )(x16, pe2, wt, wb, kg)


def _slope(hd):
    return 2.0 ** (-8.0 * (hd + 1) / NSA_HEADS) * LOG2E


def _nsa_cmp_kernel(q_ref, kvc_ref, g_ref, ov_ref, ocmp_ref, idx_ref, flag_ref, *, tq, tsel, n_cmp):
    qi = pl.program_id(1)
    nc_all, ns_all = ov_ref.shape
    need = (qi + 1) * (tq // CMP_STRIDE)
    bounds = list(range(LANES, nc_all, LANES)) + [nc_all]
    for v, nc in enumerate(bounds):
        lo = bounds[v - 1] if v else 0
        ns = min(ns_all, max(LANES, nc // (SEL_BLOCK // CMP_STRIDE)))
        pl.when((need > lo) & (need <= nc))(functools.partial(
            _nsa_cmp_body, q_ref, kvc_ref, g_ref, ov_ref, ocmp_ref, idx_ref, flag_ref, qi * tq,
            tq=tq, tsel=tsel, n_cmp=n_cmp, nc=nc, ns=ns))


def _nsa_cmp_body(q_ref, kvc_ref, g_ref, ov_ref, ocmp_ref, idx_ref, flag_ref, t0, *, tq, tsel, n_cmp, nc, ns):
    kvc = kvc_ref[0, 0:nc, :]
    g = g_ref[0]
    col = lax.broadcasted_iota(jnp.int32, (tq, nc), 1)
    t = t0 + lax.broadcasted_iota(jnp.int32, (tq, nc), 0)
    dist = t - (col * CMP_STRIDE + (CMP_BLOCK - 1))
    ok = (dist >= 0) & (col < n_cmp)
    distf = dist.astype(F32)
    psum = jnp.zeros((tq, nc), F32)
    for hd in range(NSA_HEADS):
        q = q_ref[0, :, hd * LANES:(hd + 1) * LANES]
        s = _dot_nt(q, kvc)
        s = jnp.where(ok, s - _slope(hd) * distf, NEG_INF)
        e = jnp.exp2(s - jnp.max(s, axis=-1, keepdims=True))
        p = jnp.where(ok, e * (1.0 / jnp.sum(e, axis=-1, keepdims=True)), 0.0)
        psum = psum + p
        o = _dot(p.astype(BF16), kvc)
        ocmp_ref[0, :, hd * LANES:(hd + 1) * LANES] = o * g[:, 3 * hd:3 * hd + 1]

    ov = ov_ref[0:nc, 0:ns]
    h1, h2, h3 = _split3(psum)
    imp = _dot(h1, ov) + _dot(h2, ov) + _dot(h3, ov)
    scol =lax.broadcasted_iota(jnp.int32, (tq, ns), 1)
    scolf = scol.astype(F32)
    trow = t0 + lax.broadcasted_iota(jnp.int32, (tq, ns), 0)
    free = (scol * SEL_BLOCK <= trow) & (scol != 0) & (scol != (trow >> SEL_SHIFT))
    score = jnp.where(free, imp, -1.0)
    lane = lax.broadcasted_iota(jnp.int32, (tq, LANES), 1)
    idx_out = jnp.zeros((tq, LANES), jnp.int32)
    hit = jnp.zeros((tq, LANES), F32)
    for k in range(SEL_TOPK - 2):
        m = jnp.max(score, axis=-1, keepdims=True)
        ikf = jnp.min(jnp.where(score == m, scolf, 1e9), axis=-1, keepdims=True)
        score = jnp.where(scolf == ikf, -3e38, score)
        ik = ikf.astype(jnp.int32)
        idx_out = jnp.where(lane == k, ik, idx_out)
        hit = jnp.where(lane == (ik >> _log2(tsel // SEL_BLOCK)), 1.0, hit)
    idx_ref[0] = idx_out
    for sub in range(tq // tsel):
        flag_ref[0, sub] = jnp.max(hit[sub * tsel:(sub + 1) * tsel], axis=0, keepdims=True).astype(jnp.int32)


def _nsa_cmp(qn, kvc, gates, ov, tq, tsel, n_cmp):
    B, L, _ = qn.shape
    nq = L // tq
    nsub = tq // tsel
    nc = kvc.shape[1]
    row = lambda b, i: (b, i, 0)
    return pl.pallas_call(
        functools.partial(_nsa_cmp_kernel, tq=tq, tsel=tsel, n_cmp=n_cmp),
        grid=(B, nq),
        in_specs=[pl.BlockSpec((1, tq, HEADS_W), row), pl.BlockSpec((1, nc, LANES), lambda b, i: (b, 0, 0)),
                  pl.BlockSpec((1, tq, LANES), row), _const_spec(ov.shape)],
        out_specs=[pl.BlockSpec((1, tq, HEADS_W), row), pl.BlockSpec((1, tq, LANES), row),
                   pl.BlockSpec((1, nsub, 1, LANES), lambda b, i: (b, i, 0, 0))],
        out_shape=[jax.ShapeDtypeStruct((B, L, HEADS_W), F32), jax.ShapeDtypeStruct((B, L, LANES), jnp.int32),
                   jax.ShapeDtypeStruct((B, L // tsel, 1, LANES), jnp.int32)],
        compiler_params=_params("parallel", "parallel"),
        name="nsa_cmp",
    )(qn, kvc, gates, ov)


def _nsa_sw_kernel(bits_ref, q_ref, skv_ref, wkv_ref, idx_ref, g_ref, ocmp_ref, wb_ref, o_ref,
                   m_sc, l_sc, acc_sc, win_sc, *, tq, nwords, nt):
    b = pl.program_id(0)
    nq = pl.num_programs(1) * nt
    rows = NSA_HEADS * tq
    lane_k = lax.broadcasted_iota(jnp.int32, (1, tq), 1)

    def tile_ctx(h):
        qi = pl.program_id(1) * nt + h
        r0 = h * tq
        qs = jnp.concatenate([q_ref[0, r0:r0 + tq, hd * LANES:(hd + 1) * LANES] for hd in range(NSA_HEADS)],
                             axis=0)
        rowi = lax.broadcasted_iota(jnp.int32, (rows, LANES), 0)
        ti = qi * tq + (rowi & (tq - 1))
        head = rowi >> _log2(tq)
        slope = jnp.where(head == 0, _slope(0), jnp.where(head == 1, _slope(1),
                          jnp.where(head == 2, _slope(2), _slope(3))))
        idx4 = jnp.concatenate([idx_ref[0, r0:r0 + tq, :]] * NSA_HEADS, axis=0)
        picks = [jnp.broadcast_to(idx4[:, k:k + 1], (rows, LANES)) for k in range(SEL_TOPK - 2)]
        return qi, qs, ti.astype(F32), ti >> SEL_SHIFT, slope, picks

    def scores(ctx, j, kv, forced_blocks, valid):
        _, qs, tf, cur_blk, slope, picks = ctx
        kpos = j * tq + lane_k
        blk = kpos >> SEL_SHIFT
        distf = tf - kpos.astype(F32)
        msk = (blk == picks[0]) | (blk == picks[1])
        if forced_blocks:
            min_dist = jnp.where(valid, 0.0, 3e38)
            msk = (msk | (blk == 0) | (blk == cur_blk)) & (distf >= min_dist)
        s = jnp.where(msk, _dot_nt(qs, kv) - slope * distf, NEG_INF)
        return s, msk

    def static_part(h):
        ctx = tile_ctx(h)
        qi, qs = ctx[0], ctx[1]
        jprev = jnp.maximum(qi - 1, 0)
        tiles = [(0, skv_ref[0, 0:tq, :], qi >= 0),
                 (jprev, skv_ref[0, pl.ds(pl.multiple_of(jprev * tq, tq), tq), :], qi >= 2),
                 (qi, skv_ref[0, pl.ds(pl.multiple_of(qi * tq, tq), tq), :], qi >= 1)]
        sm = [scores(ctx, j, kv, True, valid) for j, kv, valid in tiles]
        m0 = jnp.maximum(jnp.maximum(jnp.max(sm[0][0], axis=-1, keepdims=True),
                                     jnp.max(sm[1][0], axis=-1, keepdims=True)),
                         jnp.max(sm[2][0], axis=-1, keepdims=True))
        l0 = jnp.zeros((rows, 1), F32)
        acc0 = jnp.zeros((rows, LANES), F32)
        for (s, msk), (_, kv, _) in zip(sm, tiles):
            p = jnp.where(msk, jnp.exp2(s - m0), 0.0)
            l0 = l0 + jnp.sum(p, axis=-1, keepdims=True)
            acc0 = acc0 + _dot(p.astype(BF16), kv)
        m_sc[h] = jnp.broadcast_to(m0, (rows, LANES))
        l_sc[h] = jnp.broadcast_to(l0, (rows, LANES))
        acc_sc[h] = acc0

        t0 = qi * tq
        nband = WINDOW + tq
        start = pl.multiple_of(jnp.maximum(t0 - WINDOW, 0), tq)
        band = wkv_ref[0, pl.ds(start, nband), :]
        shift = (WINDOW - jnp.minimum(t0, WINDOW)) // LANES
        bias = jnp.concatenate([wb_ref[shift + k] for k in range(nband // LANES)], axis=-1)
        s = _dot_nt(qs, band) + bias
        e = jnp.exp2(s - jnp.max(s, axis=-1, keepdims=True))
        win_sc[h] = _dot(e.astype(BF16), band) * (1.0 / jnp.sum(e, axis=-1, keepdims=True))

    for h in range(nt):
        static_part(h)

    lane = lax.broadcasted_iota(jnp.int32, (1, LANES), 1)
    for h in range(nt):
        ctx = tile_ctx(h)
        qi = ctx[0]

        def sel_step(j, carry, h=h, ctx=ctx, qi=qi):
            word = bits_ref[(b * nq + qi) * nwords + (j >> 5)]
            needed = (lax.shift_right_logical(word, j & 31) & 1) == 1

            @pl.when(needed)
            def _():
                kv = skv_ref[0, pl.ds(pl.multiple_of(j * tq, tq), tq), :]
                s, msk = scores(ctx, j, kv, False, None)
                m_old = m_sc[h]
                m_new = jnp.maximum(m_old, jnp.max(s, axis=-1, keepdims=True))
                p = jnp.where(msk, jnp.exp2(s - m_new), 0.0)
                alpha = jnp.exp2(m_old - m_new)
                l_sc[h] = alpha * l_sc[h] + jnp.sum(p, axis=-1, keepdims=True)
                acc_sc[h] = alpha * acc_sc[h] + _dot(p.astype(BF16), kv)
                m_sc[h] = m_new
            return carry

        lax.fori_loop(1, jnp.maximum(qi - 1, 1), sel_step, 0)
        o_sel = acc_sc[h] / l_sc[h]
        o_win = win_sc[h]
        r0 = h * tq
        g = g_ref[0, r0:r0 + tq, :]
        for hd in range(NSA_HEADS):
            r = slice(hd * tq, (hd + 1) * tq)
            o = (ocmp_ref[0, r0:r0 + tq, hd * LANES:(hd + 1) * LANES]
                 + g[:, 3 * hd + 1:3 * hd + 2] * o_sel[r]
                 + g[:, 3 * hd + 2:3 * hd + 3] * o_win[r])
            o_ref[0, r0:r0 + tq, hd * LANES:(hd + 1) * LANES] = jnp.where(lane >= HEAD_DIM, o, 0.0).astype(BF16)


def _window_bias(tq):
    ncol = WINDOW + tq + WINDOW
    dist = np.arange(tq)[:, None] + WINDOW - np.arange(ncol)[None, :]
    ok = (dist >= 0) & (dist < WINDOW)
    tab = np.concatenate([np.where(ok, -_slope(hd) * dist, NEG_INF) for hd in range(NSA_HEADS)], axis=0)
    return jnp.asarray(tab.reshape(NSA_HEADS * tq, ncol // LANES, LANES).transpose(1, 0, 2), dtype=F32)


def _nsa_sw(bits, qn, skv, wkv, idx, gates, ocmp, wbias, tq, nwords, nt):
    B, L, _ = qn.shape
    row = lambda b, i, s: (b, i, 0)
    full = lambda b, i, s: (b, 0, 0)
    rows = NSA_HEADS * tq
    tr = nt * tq
    return pl.pallas_call(
        functools.partial(_nsa_sw_kernel, tq=tq, nwords=nwords, nt=nt),
        grid_spec=pltpu.PrefetchScalarGridSpec(
            num_scalar_prefetch=1,
            grid=(B, L // tr),
            in_specs=[pl.BlockSpec((1, tr, HEADS_W), row), pl.BlockSpec((1, L, LANES), full),
                      pl.BlockSpec((1, L, LANES), full), pl.BlockSpec((1, tr, LANES), row),
                      pl.BlockSpec((1, tr, LANES), row), pl.BlockSpec((1, tr, HEADS_W), row),
                      _const_spec(wbias.shape)],
            out_specs=pl.BlockSpec((1, tr, HEADS_W), row),
            scratch_shapes=[pltpu.VMEM((nt, rows, LANES), F32) for _ in range(4)],
        ),
        out_shape=jax.ShapeDtypeStruct((B, L, HEADS_W), BF16),
        compiler_params=_params("parallel", "arbitrary"),
        name="nsa_sw",
    )(bits, qn, skv, wkv, idx, gates, ocmp, wbias)


S5_X = S5_GROUPS * S5_STATE
S5_PAD = 8


def _s5_kernel(u_ref, bcat_ref, ccat_ref, a_ref, d_ref, wglu_ref, o_ref, x_sc, st_sc, *, steps, nb, nch):
    @pl.when(pl.program_id(0) == 0)
    def _():
        st_sc[...] = jnp.zeros_like(st_sc)

    stride = steps + S5_PAD
    ng = S5_X // LANES
    for b in range(nb):
        bu = _dot(u_ref[b].astype(BF16), bcat_ref[...])
        for k in range(2 * ng):
            x_sc[k, b * stride:b * stride + steps, :] = bu[:, k * LANES:(k + 1) * LANES]
    gpc = ng // nch
    for c in range(nch):
        ks = list(range(c * gpc, (c + 1) * gpc))
        ar = [jnp.broadcast_to(a_ref[0:1, k * LANES:(k + 1) * LANES], (nb, LANES)) for k in ks]
        ai = [jnp.broadcast_to(a_ref[0:1, S5_X + k * LANES:S5_X + (k + 1) * LANES], (nb, LANES)) for k in ks]

        def body(t, carry):
            xr, xi = carry
            r = pl.ds(t, nb, stride=stride)
            nr, ni = [], []
            for n, k in enumerate(ks):
                vr = ar[n] * xr[n] - ai[n] * xi[n] + x_sc[k, r, :]
                vi = ar[n] * xi[n] + ai[n] * xr[n] + x_sc[ng + k, r, :]
                x_sc[k, r, :] = vr
                x_sc[ng + k, r, :] = vi
                nr.append(vr)
                ni.append(vi)
            return tuple(nr), tuple(ni)

        init = (tuple(st_sc[:, k * LANES:(k + 1) * LANES] for k in ks),
                tuple(st_sc[:, S5_X + k * LANES:S5_X + (k + 1) * LANES] for k in ks))
        xr, xi = lax.fori_loop(0, steps, body, init, unroll=8)
        for n, k in enumerate(ks):
            st_sc[:, k * LANES:(k + 1) * LANES] = xr[n]
            st_sc[:, S5_X + k * LANES:S5_X + (k + 1) * LANES] = xi[n]

    for b in range(nb):
        xs = jnp.concatenate([x_sc[k, b * stride:b * stride + steps, :] for k in range(2 * ng)], axis=-1)
        y = _dot(xs.astype(BF16), ccat_ref[...]) + d_ref[...] * u_ref[b]
        y = _gelu(y)
        o_ref[b] = (y * _sigmoid(_dot(y.astype(BF16), wglu_ref[...]))).astype(BF16)


def _s5(u, bcat, ccat, a_cat, d, wglu, steps):
    nb, L, _ = u.shape
    return pl.pallas_call(
        functools.partial(_s5_kernel, steps=steps, nb=nb, nch=2),
        grid=(L // steps,),
        in_specs=[pl.BlockSpec((nb, steps, S5_WIDTH), lambda i: (0, i, 0)), _const_spec(bcat.shape),
                  _const_spec(ccat.shape), _const_spec(a_cat.shape), _const_spec(d.shape),
                  _const_spec(wglu.shape)],
        out_specs=pl.BlockSpec((nb, steps, S5_WIDTH), lambda i: (0, i, 0)),
        out_shape=jax.ShapeDtypeStruct((nb, L, S5_WIDTH), BF16),
        scratch_shapes=[pltpu.VMEM((2 * S5_X // LANES, nb * (steps + S5_PAD), LANES), F32),
                        pltpu.VMEM((nb, 2 * S5_X), F32)],
        compiler_params=_params("arbitrary"),
        name="s5",
    )(u, bcat, ccat, a_cat, d, wglu)


SIGN_BIT = np.int32(-2 ** 31)
SB_STOP_LOG2 = 170.0


def _sb_softplus2(z):
    neg_abs = lax.bitcast_convert_type(lax.bitcast_convert_type(z, jnp.int32) | SIGN_BIT, F32)
    return jnp.maximum(z, 0.0) + jnp.log2(1.0 + jnp.exp2(neg_abs))


def _sb_suffix(spb, tri_ref):
    half = tri_ref.shape[1]
    return jnp.concatenate([_dot(spb, tri_ref[...]), _dot(spb[:, half:], tri_ref[0:half, :])], axis=-1)


def _sb_kernel(q_ref, kv_ref, tri1_ref, tri_ref, o_ref, acc_sc, c_sc, z_sc, w_sc, *, tq, tk1, tk, nt):
    g = pl.program_id(2)

    def kv_block(j):
        return kv_ref[0, pl.ds(pl.multiple_of(j * tk, tk), tk), :]

    def first_step(h):
        i = g * nt + h
        q = q_ref[0, h * tq:(h + 1) * tq, :]
        start = jnp.maximum((i + 1) * tq - tk1, 0)
        kv1 = kv_ref[0, pl.ds(pl.multiple_of(start, tk), tk1), :]
        z = _dot_nt(q, kv1)
        t = i * tq + lax.broadcasted_iota(jnp.int32, (tq, tk1), 0)
        causal = (start + lax.broadcasted_iota(jnp.int32, (tq, tk1), 1)) < t
        spb = jnp.where(causal, _sb_softplus2(z), 0.0).astype(BF16)
        r = _sb_suffix(spb, tri1_ref)
        w = jnp.where(causal, jnp.exp2(jnp.minimum(z + r, 0.0)), 0.0)
        acc_sc[h] = _dot(w.astype(BF16), kv1)
        c_sc[h] = jnp.broadcast_to(r[:, 0:1], (tq, LANES))
        j0 = start // tk - 1
        z_sc[h] = _dot_nt(q, kv_block(jnp.maximum(j0, 0)))
        w_sc[h] = jnp.zeros((tq, tk), BF16)
        return j0

    j0s = [first_step(h) for h in range(nt)]
    cmaxs = [jnp.max(c_sc[h]) for h in range(nt)]

    def cond(carry):
        j, cmax = carry
        return (j >= 0) & (cmax > -SB_STOP_LOG2)

    lane = lax.broadcasted_iota(jnp.int32, (1, LANES), 1)
    for h in range(nt):
        q = q_ref[0, h * tq:(h + 1) * tq, :]

        def body(carry, h=h, q=q):
            j, _ = carry
            z = z_sc[h]
            z_next = _dot_nt(q, kv_block(jnp.maximum(j - 1, 0)))
            acc_sc[h] += _dot(w_sc[h], kv_block(j + 1))
            r = _sb_suffix(_sb_softplus2(z).astype(BF16), tri_ref)
            c = c_sc[h]
            cw = jnp.concatenate([c] * (tk // LANES), axis=-1)
            w_sc[h] = jnp.exp2(jnp.minimum(z + r + cw, 0.0)).astype(BF16)
            z_sc[h] = z_next
            c_sc[h] = c + r[:, 0:1]
            return j - 1, jnp.max(c_sc[h])

        j_next, _ = lax.while_loop(cond, body, (j0s[h], cmaxs[h]))

        @pl.when(j_next < j0s[h])
        def _(h=h, j_next=j_next):
            acc_sc[h] += _dot(w_sc[h], kv_block(j_next + 1))

        o_ref[0, h * tq:(h + 1) * tq, :] = jnp.where(lane >= HEAD_DIM, acc_sc[h], 0.0).astype(BF16)


def _sb(sbq, sbkv, tri1, tri, tq, tk1, tk, nt):
    B, L, _ = sbq.shape
    assert tq % tk == 0 and tk1 % tk == 0 and L >= tk1
    return pl.pallas_call(
        functools.partial(_sb_kernel, tq=tq, tk1=tk1, tk=tk, nt=nt),
        grid=(B, SB_HEADS, L // (nt * tq)),
        in_specs=[pl.BlockSpec((1, nt * tq, LANES), lambda b, h, i: (b, i, h)),
                  pl.BlockSpec((1, L, LANES), lambda b, h, i: (b, 0, h)),
                  _const_spec(tri1.shape), _const_spec(tri.shape)],
        out_specs=pl.BlockSpec((1, nt * tq, LANES), lambda b, h, i: (b, i, h)),
        out_shape=jax.ShapeDtypeStruct((B, L, SB_HEADS * LANES), BF16),
        scratch_shapes=[pltpu.VMEM((nt, tq, LANES), F32), pltpu.VMEM((nt, tq, LANES), F32),
                        pltpu.VMEM((nt, tq, tk), F32), pltpu.VMEM((nt, tq, tk), BF16)],
        compiler_params=_params("parallel", "parallel", "arbitrary"),
        name="sb_attn",
    )(sbq, sbkv, tri1, tri)


FFN_CHUNK = 1024
HALO = 8


def _merge_ffn_kernel(x_ref, oa_ref, ob_ref, oc_ref, mg_ref, wa_ref, wb_ref, wc_ref, wout_ref, gain_ref,
                      wup_ref, cw_ref, cb_ref, wdown_ref, out_ref, ubuf, carry, *, tl, dff):
    @pl.when(pl.program_id(1) == 0)
    def _():
        carry[...] = jnp.zeros_like(carry)

    d = x_ref.shape[2]
    merged = (mg_ref[0, :, 0:d].astype(F32) * _dot(oa_ref[0], wa_ref[...])
              + mg_ref[0, :, d:2 * d].astype(F32) * _dot(ob_ref[0], wb_ref[...])
              + mg_ref[0, :, 2 * d:3 * d].astype(F32) * _dot(oc_ref[0], wc_ref[...]))
    x1 = x_ref[0] + _dot(merged.astype(BF16), wout_ref[...])
    h = x1 * lax.rsqrt(jnp.mean(x1 * x1, axis=-1, keepdims=True) + NORM_EPS) * gain_ref[...]
    hb = h.astype(BF16)

    def conv_cols(off):
        cols = slice(off, off + FFN_CHUNK)
        u = _dot(hb, wup_ref[:, cols])
        ubuf[0:HALO, :] = carry[:, cols]
        ubuf[HALO:HALO + tl, :] = u
        carry[:, cols] = u[tl - HALO:tl, :]
        return (cw_ref[2:3, cols] * u + cw_ref[1:2, cols] * ubuf[HALO - 1:HALO - 1 + tl, :]
                + cw_ref[0:1, cols] * ubuf[HALO - 2:HALO - 2 + tl, :] + cb_ref[:, cols])

    acc = x1
    for j in range(dff // FFN_CHUNK):
        gate = conv_cols(j * FFN_CHUNK)
        val = conv_cols(dff + j * FFN_CHUNK)
        act = (_gelu(gate) * val).astype(BF16)
        acc = acc + _dot(act, wdown_ref[j * FFN_CHUNK:(j + 1) * FFN_CHUNK, :])
    out_ref[0] = acc


def _merge_ffn(x, oa, ob, oc, mg, wa, wb, wc, wout, gain, wup, cw, cb, wdown, tl):
    B, L, D = x.shape
    dff = wdown.shape[0]
    row = lambda b, l: (b, l, 0)
    weights = [wa, wb, wc, wout, gain, wup, cw, cb, wdown]
    return pl.pallas_call(
        functools.partial(_merge_ffn_kernel, tl=tl, dff=dff),
        grid=(B, L // tl),
        in_specs=[pl.BlockSpec((1, tl, D), row), pl.BlockSpec((1, tl, oa.shape[2]), row),
                  pl.BlockSpec((1, tl, ob.shape[2]), row), pl.BlockSpec((1, tl, oc.shape[2]), row),
                  pl.BlockSpec((1, tl, mg.shape[2]), row)] + [_const_spec(w.shape) for w in weights],
        out_specs=pl.BlockSpec((1, tl, D), row),
        out_shape=jax.ShapeDtypeStruct((B, L, D), F32),
        scratch_shapes=[pltpu.VMEM((HALO + tl, FFN_CHUNK), F32), pltpu.VMEM((HALO, 2 * dff), F32)],
        compiler_params=_params("parallel", "arbitrary"),
        name="merge_ffn",
    )(x, oa, ob, oc, mg, *weights)


def _pad_heads(w, n_heads):
    d = w.shape[0]
    w = w.reshape(d, n_heads, HEAD_DIM)
    return jnp.concatenate([w, jnp.zeros_like(w)], axis=-1).reshape(d, n_heads * LANES)


def _pack_w_in(w):
    d = w.shape[0]
    o_g = 640
    o_s5 = o_g + NSA_HEADS * 3
    o_sb = o_s5 + S5_WIDTH
    o_mg = o_sb + 3 * SB_HEADS * HEAD_DIM
    hw = SB_HEADS * HEAD_DIM
    gates = jnp.pad(w[:, o_g:o_s5], ((0, 0), (0, LANES - NSA_HEADS * 3)))
    sbk = w[:, o_sb + hw:o_sb + 2 * hw].reshape(d, SB_HEADS, 1, HEAD_DIM)
    sbv = w[:, o_sb + 2 * hw:o_mg].reshape(d, SB_HEADS, 1, HEAD_DIM)
    sbkv = jnp.concatenate([sbk, sbv], axis=2).reshape(d, SB_HEADS * LANES)
    packed = jnp.concatenate([_pad_heads(w[:, 0:256], NSA_HEADS), w[:, 256:o_g], gates, w[:, o_s5:o_sb],
                              _pad_heads(w[:, o_sb:o_sb + hw], SB_HEADS), sbkv, w[:, o_mg:]], axis=1)
    return packed.astype(BF16)


def _pad_rows(w, n_heads):
    d = w.shape[1]
    w = w.reshape(n_heads, HEAD_DIM, d)
    return jnp.concatenate([jnp.zeros_like(w), w], axis=1).reshape(n_heads * LANES, d).astype(BF16)


def _pad_gain(g, fill):
    return jnp.concatenate([g, jnp.full_like(g, fill)], axis=-1)


def _cmp_weights(wk, wv):
    z = jnp.zeros_like(wk)
    wkv = jnp.concatenate([jnp.concatenate([wk, z], axis=2), jnp.concatenate([z, wv], axis=2)], axis=1)
    half = CMP_BLOCK // 2
    wt = wkv[:half].reshape(half * LANES, LANES)
    wb = wkv[half:].reshape(half * LANES, LANES)
    return wt.astype(BF16), wb.astype(BF16)


def _s5_params(a_re, a_im, log_dt, b_re, b_im, c_re, c_im):
    dt = jnp.exp(log_dt.astype(F32))[:, None]
    A = lax.complex(a_re.astype(F32), a_im.astype(F32))
    A_bar = jnp.exp(dt * A)
    B_bar = ((A_bar - 1.0) / A)[..., None] * lax.complex(b_re.astype(F32), b_im.astype(F32))
    eye = jnp.eye(S5_GROUPS, dtype=F32)

    def bdiag_in(m):
        return jnp.einsum('gpc,gh->gchp', m, eye).reshape(S5_WIDTH, S5_X)

    def bdiag_out(m):
        return jnp.einsum('gcp,gh->gphc', m, eye).reshape(S5_X, S5_WIDTH)

    bcat = jnp.concatenate([bdiag_in(jnp.real(B_bar)), bdiag_in(jnp.imag(B_bar))], axis=1).astype(BF16)
    ccat = jnp.concatenate([bdiag_out(c_re.astype(F32)), -bdiag_out(c_im.astype(F32))], axis=0).astype(BF16)
    a_cat = jnp.concatenate([jnp.real(A_bar).reshape(1, S5_X), jnp.imag(A_bar).reshape(1, S5_X)], axis=1)
    return bcat, ccat, a_cat


def _overlap_matrix(n_rows, n_cmp, n_sel):
    cs = np.arange(n_rows)[:, None] * CMP_STRIDE
    ss = np.arange(n_sel)[None, :] * SEL_BLOCK
    ov = np.clip(np.minimum(cs + CMP_BLOCK, ss + SEL_BLOCK) - np.maximum(cs, ss), 0, None) / CMP_BLOCK
    ov[n_cmp:] = 0.0
    return jnp.asarray(ov, dtype=BF16)


def _sb_suffix_matrix(tk):
    u = (np.arange(tk)[:, None] >= np.arange(tk)[None, :]).astype(np.float32)
    return jnp.asarray(-np.concatenate([u, np.ones_like(u)], axis=0), dtype=BF16)


def kernel(x, norm_mix, w_in, nsa_q_gain, nsa_k_gain, cmp_pe, cmp_wk, cmp_wv, s5_a_re, s5_a_im, s5_log_dt,
           s5_b_re, s5_b_im, s5_c_re, s5_c_im, s5_d, s5_w_glu, w_br_nsa, w_br_s5, w_br_sb, w_out, norm_ffn,
           w_up, conv_w, conv_b, w_down):
    B, L, D = x.shape
    depth = w_in.shape[0]
    tq = 128
    tq_cmp = 512
    tl_in = min(512, L)
    tl_ffn = min(512, L)
    sb_tq, sb_tk1, sb_tk, sb_tiles = 256, 512, 256, 4
    s5_steps = min(256, L)
    assert L % tl_in == 0 and L % (sb_tq * sb_tiles) == 0 and L >= WINDOW + tq and D == D_MODEL
    n_cmp = (L - CMP_BLOCK) // CMP_STRIDE + 1
    n_rows = L // CMP_STRIDE
    n_sel = L // SEL_BLOCK
    nq = L // tq
    nwords = (nq + 31) // 32
    ov = _overlap_matrix(n_rows, n_cmp, n_sel)
    tri1, tri = _sb_suffix_matrix(sb_tk1 // 2), _sb_suffix_matrix(sb_tk // 2)
    wbias = _window_bias(tq)

    def pack_layer(p):
        wt, wb = _cmp_weights(p['cmp_wk'][0], p['cmp_wv'][0])
        bcat, ccat, a_cat = _s5_params(p['s5_a_re'], p['s5_a_im'], p['s5_log_dt'], p['s5_b_re'], p['s5_b_im'],
                                       p['s5_c_re'], p['s5_c_im'])
        return dict(
            norm_mix=p['norm_mix'][None, :], w_in=_pack_w_in(p['w_in']),
            qg=_pad_gain(p['nsa_q_gain'][None, :], 0.0), kg=_pad_gain(p['nsa_k_gain'], 1.0),
            pe=jnp.concatenate([p['cmp_pe'], p['cmp_pe']], axis=-1).reshape(2, (CMP_BLOCK // 2) * LANES),
            wt=wt, wb=wb, bcat=bcat, ccat=ccat, a_cat=a_cat, d=p['s5_d'].reshape(1, S5_WIDTH),
            wglu=p['s5_w_glu'].astype(BF16), wa=_pad_rows(p['w_br_nsa'], NSA_HEADS),
            wb5=p['w_br_s5'].astype(BF16), wc=_pad_rows(p['w_br_sb'], SB_HEADS), wout=p['w_out'].astype(BF16),
            norm_ffn=p['norm_ffn'][None, :], wup=p['w_up'].astype(BF16), cw=p['conv_w'],
            cb=p['conv_b'][None, :], wdown=p['w_down'].astype(BF16))

    packed = jax.vmap(pack_layer)(dict(
        norm_mix=norm_mix, w_in=w_in, nsa_q_gain=nsa_q_gain, nsa_k_gain=nsa_k_gain, cmp_pe=cmp_pe, cmp_wk=cmp_wk,
        cmp_wv=cmp_wv, s5_a_re=s5_a_re, s5_a_im=s5_a_im, s5_log_dt=s5_log_dt, s5_b_re=s5_b_re, s5_b_im=s5_b_im,
        s5_c_re=s5_c_re, s5_c_im=s5_c_im, s5_d=s5_d, s5_w_glu=s5_w_glu, w_br_nsa=w_br_nsa, w_br_s5=w_br_s5,
        w_br_sb=w_br_sb, w_out=w_out, norm_ffn=norm_ffn, w_up=w_up, conv_w=conv_w, conv_b=conv_b, w_down=w_down))

    for i in range(depth):
        p = {k: v[i] for k, v in packed.items()}
        qn, ckv, skv, wkv, gates, u_s5, sbq, sbkv, mg = _inproj(
            x, p['norm_mix'], p['w_in'], p['qg'], p['kg'][1:3], tl_in)

        kvc = _cmpkv(ckv.reshape(B, n_rows, CMP_STRIDE * LANES), p['pe'], p['wt'], p['wb'], p['kg'][0:1])
        ocmp, idx, flags = _nsa_cmp(qn, kvc, gates, ov, tq_cmp, tq, n_cmp)
        fl = jnp.pad(flags[:, :, 0, :nq], ((0, 0), (0, 0), (0, nwords * 32 - nq))).reshape(B, nq, nwords, 32)
        bits = jnp.sum(fl << jnp.arange(32, dtype=jnp.int32), axis=-1, dtype=jnp.int32).reshape(-1)
        o_a = _nsa_sw(bits, qn, skv, wkv, idx, gates, ocmp, wbias, tq, nwords, 2)

        o_b = _s5(u_s5, p['bcat'], p['ccat'], p['a_cat'], p['d'], p['wglu'], s5_steps)

        o_c = _sb(sbq, sbkv, tri1, tri, sb_tq, sb_tk1, sb_tk, sb_tiles)

        x = _merge_ffn(x, o_a, o_b, o_c, mg, p['wa'], p['wb5'], p['wc'], p['wout'], p['norm_ffn'],
                       p['wup'], p['cw'], p['cb'], p['wdown'], tl_ffn)
    return x
```

```python
import functools
import math

import numpy as np
import jax
import jax.numpy as jnp
from jax import lax
from jax.experimental import pallas as pl
from jax.experimental.pallas import tpu as pltpu

HEAD_DIM = 64
NSA_HEADS = 4
CMP_BLOCK = 32
CMP_STRIDE = 16
SEL_BLOCK = 32
SEL_TOPK = 4
WINDOW = 512
FORCE_SCORE = 1.0e4
S5_WIDTH = 256
S5_GROUP = 16
S5_GROUPS = S5_WIDTH // S5_GROUP
S5_STATE = 64
SB_HEADS = 4
NORM_EPS = 1e-6
NEG_INF = -1e30
LANES = 128
VMEM_LIMIT = 56 * 1024 * 1024

F32 = jnp.float32
BF16 = jnp.bfloat16


def _log2(n):
    assert n & (n - 1) == 0
    return n.bit_length() - 1


SEL_SHIFT = _log2(SEL_BLOCK)
LOG2E = math.log2(math.e)


def _dot(a, b):
    return jnp.dot(a, b, preferred_element_type=F32)


def _dot_nt(a, b):
    return lax.dot_general(a, b, (((1,), (1,)), ((), ())), preferred_element_type=F32)


def _const_spec(shape):
    nd = len(shape)
    return pl.BlockSpec(shape, lambda *_: (0,) * nd, pipeline_mode=pl.Buffered(1))


def _params(*sem):
    return pltpu.CompilerParams(dimension_semantics=sem, vmem_limit_bytes=VMEM_LIMIT)


def _gelu(x):
    return 0.5 * x * (1.0 + jnp.tanh(math.sqrt(2.0 / math.pi) * (x + 0.044715 * (x * x * x))))


def _sigmoid(x):
    return 1.0 / (1.0 + jnp.exp(-x))


def _split3(x):
    h1 = x.astype(BF16)
    r1 = x - h1.astype(F32)
    h2 = r1.astype(BF16)
    h3 = (r1 - h2.astype(F32)).astype(BF16)
    return h1, h2, h3


D_MODEL = 1024
N_BRANCHES = 3
HEADS_W = NSA_HEADS * LANES
C_QN = 0
C_KV = C_QN + HEADS_W
C_G = C_KV + 3 * LANES
C_S5 = C_G + LANES
C_SBQ = C_S5 + S5_WIDTH
C_SBKV = C_SBQ + HEADS_W
C_MG = C_SBKV + HEADS_W
C_END = C_MG + N_BRANCHES * D_MODEL


def _inproj_kernel(x_ref, gain_ref, w_ref, qg_ref, kg_ref,
                   qn_ref, ckv_ref, skv_ref, wkv_ref, g_ref, s5_ref, sbq_ref, sbkv_ref, mg_ref):
    x = x_ref[0]
    h = x * lax.rsqrt(jnp.mean(x * x, axis=-1, keepdims=True) + NORM_EPS) * gain_ref[...]
    hb = h.astype(BF16)
    lane = lax.broadcasted_iota(jnp.int32, (1, LANES), 1)
    lo = lane < HEAD_DIM

    seg = _dot(hb, w_ref[:, C_QN:C_S5])
    scale = HEAD_DIM ** -0.5 * LOG2E
    for hd in range(NSA_HEADS):
        q = seg[:, hd * LANES:(hd + 1) * LANES]
        ms = jnp.sum(q * q, axis=-1, keepdims=True) * (1.0 / HEAD_DIM)
        qn = q * lax.rsqrt(ms + NORM_EPS) * qg_ref[...] * scale
        qn_ref[0, :, hd * LANES:(hd + 1) * LANES] = qn.astype(BF16)
    ckv_ref[0] = seg[:, C_KV:C_KV + LANES].astype(BF16)
    for n, ref in ((1, skv_ref), (2, wkv_ref)):
        kv = seg[:, C_KV + n * LANES:C_KV + (n + 1) * LANES]
        ms = jnp.sum(jnp.where(lo, kv * kv, 0.0), axis=-1, keepdims=True) * (1.0 / HEAD_DIM)
        sc = jnp.where(lo, lax.rsqrt(ms + NORM_EPS) * kg_ref[n - 1:n, :], 1.0)
        ref[0] = (kv * sc).astype(BF16)
    g_ref[0] = _sigmoid(seg[:, C_G:C_G + LANES])

    seg = _dot(hb, w_ref[:, C_S5:C_MG])
    s5_ref[0] = seg[:, 0:S5_WIDTH]
    sbq_ref[0] = (seg[:, C_SBQ - C_S5:C_SBKV - C_S5] * scale).astype(BF16)
    sbkv_ref[0] = seg[:, C_SBKV - C_S5:C_MG - C_S5].astype(BF16)

    for c in range(3):
        seg = _dot(hb, w_ref[:, C_MG + c * 1024:C_MG + (c + 1) * 1024])
        mg_ref[0, :, c * 1024:(c + 1) * 1024] = _sigmoid(seg).astype(BF16)


def _inproj(x, gain, w, qg, kg, tl):
    B, L, D = x.shape
    grid = (B, L // tl)
    row = lambda b, l: (b, l, 0)

    def out(n, dt):
        return jax.ShapeDtypeStruct((B, L, n), dt), pl.BlockSpec((1, tl, n), row)

    outs = [out(HEADS_W, BF16), out(LANES, BF16), out(LANES, BF16), out(LANES, BF16), out(LANES, F32),
            out(S5_WIDTH, F32), out(HEADS_W, BF16), out(HEADS_W, BF16), out(C_END - C_MG, BF16)]
    return pl.pallas_call(
        _inproj_kernel,
        grid=grid,
        in_specs=[pl.BlockSpec((1, tl, D), row), _const_spec((1, D)), _const_spec((D, C_END)),
                  _const_spec((1, LANES)), _const_spec((2, LANES))],
        out_specs=[o[1] for o in outs],
        out_shape=[o[0] for o in outs],
        compiler_params=_params("parallel", "parallel"),
        name="inproj",
    )(x, gain, w, qg, kg)


def _cmpkv_kernel(x_ref, pe_ref, wt_ref, wb_ref, kg_ref, o_ref):
    x = x_ref[0].astype(F32)
    a = _dot((x + pe_ref[0:1, :]).astype(BF16), wt_ref[...])
    b = _dot((x + pe_ref[1:2, :]).astype(BF16), wb_ref[...])
    n = x.shape[0]
    kv = a + pltpu.roll(b, n - 1, 0)
    lane = lax.broadcasted_iota(jnp.int32, (1, LANES), 1)
    lo = lane < HEAD_DIM
    ms = jnp.sum(jnp.where(lo, kv * kv, 0.0), axis=-1, keepdims=True) * (1.0 / HEAD_DIM)
    sc = jnp.where(lo, lax.rsqrt(ms + NORM_EPS) * kg_ref[...], 1.0)
    rowi = lax.broadcasted_iota(jnp.int32, (n, 1), 0)
    o_ref[0] = jnp.where(rowi < n - 1, kv * sc, 0.0).astype(BF16)


def _cmpkv(x16, pe2, wt, wb, kg):
    B, n, w = x16.shape
    return pl.pallas_call(
        _cmpkv_kernel,
        grid=(B,),
        in_specs=[pl.BlockSpec((1, n, w), lambda b: (b, 0, 0)), _const_spec((2, w)),
                  _const_spec((w, LANES)), _const_spec((w, LANES)), _const_spec((1, LANES))],
        out_specs=pl.BlockSpec((1, n, LANES), lambda b: (b, 0, 0)),
        out_shape=jax.ShapeDtypeStruct((B, n, LANES), BF16),
        compiler_params=_params("parallel"),
        name="cmpkv",
    )(x16, pe2, wt, wb, kg)


def _slope(hd):
    return 2.0 ** (-8.0 * (hd + 1) / NSA_HEADS) * LOG2E


def _nsa_cmp_kernel(q_ref, kvc_ref, g_ref, ov_ref, ocmp_ref, idx_ref, flag_ref, *, tq, tsel, n_cmp):
    qi = pl.program_id(1)
    nc_all, ns_all = ov_ref.shape
    need = (qi + 1) * (tq // CMP_STRIDE)
    bounds = list(range(LANES, nc_all, LANES)) + [nc_all]
    for v, nc in enumerate(bounds):
        lo = bounds[v - 1] if v else 0
        ns = min(ns_all, max(LANES, nc // (SEL_BLOCK // CMP_STRIDE)))
        pl.when((need > lo) & (need <= nc))(functools.partial(
            _nsa_cmp_body, q_ref, kvc_ref, g_ref, ov_ref, ocmp_ref, idx_ref, flag_ref, qi * tq,
            tq=tq, tsel=tsel, n_cmp=n_cmp, nc=nc, ns=ns))


def _nsa_cmp_body(q_ref, kvc_ref, g_ref, ov_ref, ocmp_ref, idx_ref, flag_ref, t0, *, tq, tsel, n_cmp, nc, ns):
    kvc = kvc_ref[0, 0:nc, :]
    g = g_ref[0]
    col = lax.broadcasted_iota(jnp.int32, (tq, nc), 1)
    t = t0 + lax.broadcasted_iota(jnp.int32, (tq, nc), 0)
    dist = t - (col * CMP_STRIDE + (CMP_BLOCK - 1))
    ok = (dist >= 0) & (col < n_cmp)
    distf = dist.astype(F32)
    psum = jnp.zeros((tq, nc), F32)
    for hd in range(NSA_HEADS):
        q = q_ref[0, :, hd * LANES:(hd + 1) * LANES]
        s = _dot_nt(q, kvc)
        s = jnp.where(ok, s - _slope(hd) * distf, NEG_INF)
        e = jnp.exp2(s - jnp.max(s, axis=-1, keepdims=True))
        p = jnp.where(ok, e * (1.0 / jnp.sum(e, axis=-1, keepdims=True)), 0.0)
        psum = psum + p
        o = _dot(p.astype(BF16), kvc)
        ocmp_ref[0, :, hd * LANES:(hd + 1) * LANES] = o * g[:, 3 * hd:3 * hd + 1]

    ov = ov_ref[0:nc, 0:ns]
    h1, h2, h3 = _split3(psum)
    imp = _dot(h1, ov) + _dot(h2, ov) + _dot(h3, ov)
    scol =lax.broadcasted_iota(jnp.int32, (tq, ns), 1)
    scolf = scol.astype(F32)
    trow = t0 + lax.broadcasted_iota(jnp.int32, (tq, ns), 0)
    free = (scol * SEL_BLOCK <= trow) & (scol != 0) & (scol != (trow >> SEL_SHIFT))
    score = jnp.where(free, imp, -1.0)
    lane = lax.broadcasted_iota(jnp.int32, (tq, LANES), 1)
    idx_out = jnp.zeros((tq, LANES), jnp.int32)
    hit = jnp.zeros((tq, LANES), F32)
    for k in range(SEL_TOPK - 2):
        m = jnp.max(score, axis=-1, keepdims=True)
        ikf = jnp.min(jnp.where(score == m, scolf, 1e9), axis=-1, keepdims=True)
        score = jnp.where(scolf == ikf, -3e38, score)
        ik = ikf.astype(jnp.int32)
        idx_out = jnp.where(lane == k, ik, idx_out)
        hit = jnp.where(lane == (ik >> _log2(tsel // SEL_BLOCK)), 1.0, hit)
    idx_ref[0] = idx_out
    for sub in range(tq // tsel):
        flag_ref[0, sub] = jnp.max(hit[sub * tsel:(sub + 1) * tsel], axis=0, keepdims=True).astype(jnp.int32)


def _nsa_cmp(qn, kvc, gates, ov, tq, tsel, n_cmp):
    B, L, _ = qn.shape
    nq = L // tq
    nsub = tq // tsel
    nc = kvc.shape[1]
    row = lambda b, i: (b, i, 0)
    return pl.pallas_call(
        functools.partial(_nsa_cmp_kernel, tq=tq, tsel=tsel, n_cmp=n_cmp),
        grid=(B, nq),
        in_specs=[pl.BlockSpec((1, tq, HEADS_W), row), pl.BlockSpec((1, nc, LANES), lambda b, i: (b, 0, 0)),
                  pl.BlockSpec((1, tq, LANES), row), _const_spec(ov.shape)],
        out_specs=[pl.BlockSpec((1, tq, HEADS_W), row), pl.BlockSpec((1, tq, LANES), row),
                   pl.BlockSpec((1, nsub, 1, LANES), lambda b, i: (b, i, 0, 0))],
        out_shape=[jax.ShapeDtypeStruct((B, L, HEADS_W), F32), jax.ShapeDtypeStruct((B, L, LANES), jnp.int32),
                   jax.ShapeDtypeStruct((B, L // tsel, 1, LANES), jnp.int32)],
        compiler_params=_params("parallel", "parallel"),
        name="nsa_cmp",
    )(qn, kvc, gates, ov)


def _nsa_sw_kernel(bits_ref, q_ref, skv_ref, wkv_ref, idx_ref, g_ref, ocmp_ref, wb_ref, o_ref,
                   m_sc, l_sc, acc_sc, win_sc, *, tq, nwords, nt):
    b = pl.program_id(0)
    nq = pl.num_programs(1) * nt
    rows = NSA_HEADS * tq
    lane_k = lax.broadcasted_iota(jnp.int32, (1, tq), 1)

    def tile_ctx(h):
        qi = pl.program_id(1) * nt + h
        r0 = h * tq
        qs = jnp.concatenate([q_ref[0, r0:r0 + tq, hd * LANES:(hd + 1) * LANES] for hd in range(NSA_HEADS)],
                             axis=0)
        rowi = lax.broadcasted_iota(jnp.int32, (rows, LANES), 0)
        ti = qi * tq + (rowi & (tq - 1))
        head = rowi >> _log2(tq)
        slope = jnp.where(head == 0, _slope(0), jnp.where(head == 1, _slope(1),
                          jnp.where(head == 2, _slope(2), _slope(3))))
        idx4 = jnp.concatenate([idx_ref[0, r0:r0 + tq, :]] * NSA_HEADS, axis=0)
        picks = [jnp.broadcast_to(idx4[:, k:k + 1], (rows, LANES)) for k in range(SEL_TOPK - 2)]
        return qi, qs, ti.astype(F32), ti >> SEL_SHIFT, slope, picks

    def scores(ctx, j, kv, forced_blocks, valid):
        _, qs, tf, cur_blk, slope, picks = ctx
        kpos = j * tq + lane_k
        blk = kpos >> SEL_SHIFT
        distf = tf - kpos.astype(F32)
        msk = (blk == picks[0]) | (blk == picks[1])
        if forced_blocks:
            min_dist = jnp.where(valid, 0.0, 3e38)
            msk = (msk | (blk == 0) | (blk == cur_blk)) & (distf >= min_dist)
        s = jnp.where(msk, _dot_nt(qs, kv) - slope * distf, NEG_INF)
        return s, msk

    def static_part(h):
        ctx = tile_ctx(h)
        qi, qs = ctx[0], ctx[1]
        jprev = jnp.maximum(qi - 1, 0)
        jprev2 = jnp.maximum(qi - 2, 0)
        tiles = [(0, skv_ref[0, 0:tq, :], qi >= 0),
                 (jprev2, skv_ref[0, pl.ds(pl.multiple_of(jprev2 * tq, tq), tq), :], qi >= 3),
                 (jprev, skv_ref[0, pl.ds(pl.multiple_of(jprev * tq, tq), tq), :], qi >= 2),
                 (qi, skv_ref[0, pl.ds(pl.multiple_of(qi * tq, tq), tq), :], qi >= 1)]
        sm = [scores(ctx, j, kv, True, valid) for j, kv, valid in tiles]
        m0 = functools.reduce(jnp.maximum, [jnp.max(s, axis=-1, keepdims=True) for s, _ in sm])
        l0 = jnp.zeros((rows, 1), F32)
        acc0 = jnp.zeros((rows, LANES), F32)
        for (s, msk), (_, kv, _) in zip(sm, tiles):
            p = jnp.where(msk, jnp.exp2(s - m0), 0.0)
            l0 = l0 + jnp.sum(p, axis=-1, keepdims=True)
            acc0 = acc0 + _dot(p.astype(BF16), kv)
        m_sc[h] = jnp.broadcast_to(m0, (rows, LANES))
        l_sc[h] = jnp.broadcast_to(l0, (rows, LANES))
        acc_sc[h] = acc0

        t0 = qi * tq
        nband = WINDOW + tq
        start = pl.multiple_of(jnp.maximum(t0 - WINDOW, 0), tq)
        band = wkv_ref[0, pl.ds(start, nband), :]
        shift = (WINDOW - jnp.minimum(t0, WINDOW)) // LANES
        bias = jnp.concatenate([wb_ref[shift + k] for k in range(nband // LANES)], axis=-1)
        s = _dot_nt(qs, band) + bias
        e = jnp.exp2(s - jnp.max(s, axis=-1, keepdims=True))
        win_sc[h] = _dot(e.astype(BF16), band) * (1.0 / jnp.sum(e, axis=-1, keepdims=True))

    for h in range(nt):
        static_part(h)

    lane = lax.broadcasted_iota(jnp.int32, (1, LANES), 1)
    for h in range(nt):
        ctx = tile_ctx(h)
        qi = ctx[0]

        def sel_step(j, carry, h=h, ctx=ctx, qi=qi):
            word = bits_ref[(b * nq + qi) * nwords + (j >> 5)]
            needed = (lax.shift_right_logical(word, j & 31) & 1) == 1

            @pl.when(needed)
            def _():
                kv = skv_ref[0, pl.ds(pl.multiple_of(j * tq, tq), tq), :]
                s, msk = scores(ctx, j, kv, False, None)
                m_old = m_sc[h]
                m_new = jnp.maximum(m_old, jnp.max(s, axis=-1, keepdims=True))
                p = jnp.where(msk, jnp.exp2(s - m_new), 0.0)
                alpha = jnp.exp2(m_old - m_new)
                l_sc[h] = alpha * l_sc[h] + jnp.sum(p, axis=-1, keepdims=True)
                acc_sc[h] = alpha * acc_sc[h] + _dot(p.astype(BF16), kv)
                m_sc[h] = m_new
            return carry

        lax.fori_loop(1, jnp.maximum(qi - 2, 1), sel_step, 0)
        o_sel = acc_sc[h] / l_sc[h]
        o_win = win_sc[h]
        r0 = h * tq
        g = g_ref[0, r0:r0 + tq, :]
        for hd in range(NSA_HEADS):
            r = slice(hd * tq, (hd + 1) * tq)
            o = (ocmp_ref[0, r0:r0 + tq, hd * LANES:(hd + 1) * LANES]
                 + g[:, 3 * hd + 1:3 * hd + 2] * o_sel[r]
                 + g[:, 3 * hd + 2:3 * hd + 3] * o_win[r])
            o_ref[0, r0:r0 + tq, hd * LANES:(hd + 1) * LANES] = jnp.where(lane >= HEAD_DIM, o, 0.0).astype(BF16)


def _window_bias(tq):
    ncol = WINDOW + tq + WINDOW
    dist = np.arange(tq)[:, None] + WINDOW - np.arange(ncol)[None, :]
    ok = (dist >= 0) & (dist < WINDOW)
    tab = np.concatenate([np.where(ok, -_slope(hd) * dist, NEG_INF) for hd in range(NSA_HEADS)], axis=0)
    return jnp.asarray(tab.reshape(NSA_HEADS * tq, ncol // LANES, LANES).transpose(1, 0, 2), dtype=F32)


def _nsa_sw(bits, qn, skv, wkv, idx, gates, ocmp, wbias, tq, nwords, nt):
    B, L, _ = qn.shape
    row = lambda b, i, s: (b, i, 0)
    full = lambda b, i, s: (b, 0, 0)
    rows = NSA_HEADS * tq
    tr = nt * tq
    return pl.pallas_call(
        functools.partial(_nsa_sw_kernel, tq=tq, nwords=nwords, nt=nt),
        grid_spec=pltpu.PrefetchScalarGridSpec(
            num_scalar_prefetch=1,
            grid=(B, L // tr),
            in_specs=[pl.BlockSpec((1, tr, HEADS_W), row), pl.BlockSpec((1, L, LANES), full),
                      pl.BlockSpec((1, L, LANES), full), pl.BlockSpec((1, tr, LANES), row),
                      pl.BlockSpec((1, tr, LANES), row), pl.BlockSpec((1, tr, HEADS_W), row),
                      _const_spec(wbias.shape)],
            out_specs=pl.BlockSpec((1, tr, HEADS_W), row),
            scratch_shapes=[pltpu.VMEM((nt, rows, LANES), F32) for _ in range(4)],
        ),
        out_shape=jax.ShapeDtypeStruct((B, L, HEADS_W), BF16),
        compiler_params=_params("parallel", "arbitrary"),
        name="nsa_sw",
    )(bits, qn, skv, wkv, idx, gates, ocmp, wbias)


S5_X = S5_GROUPS * S5_STATE
S5_PAD = 8


def _s5_kernel(u_ref, bcat_ref, ccat_ref, a_ref, d_ref, wglu_ref, o_ref, x_sc, st_sc, *, steps, nb, nch):
    @pl.when(pl.program_id(0) == 0)
    def _():
        st_sc[...] = jnp.zeros_like(st_sc)

    stride = steps + S5_PAD
    ng = S5_X // LANES
    for b in range(nb):
        bu = _dot(u_ref[b].astype(BF16), bcat_ref[...])
        for k in range(2 * ng):
            x_sc[k, b * stride:b * stride + steps, :] = bu[:, k * LANES:(k + 1) * LANES]
    gpc = ng // nch
    for c in range(nch):
        ks = list(range(c * gpc, (c + 1) * gpc))
        ar = [jnp.broadcast_to(a_ref[0:1, k * LANES:(k + 1) * LANES], (nb, LANES)) for k in ks]
        ai = [jnp.broadcast_to(a_ref[0:1, S5_X + k * LANES:S5_X + (k + 1) * LANES], (nb, LANES)) for k in ks]

        def body(t, carry):
            xr, xi = carry
            r = pl.ds(t, nb, stride=stride)
            nr, ni = [], []
            for n, k in enumerate(ks):
                vr = ar[n] * xr[n] - ai[n] * xi[n] + x_sc[k, r, :]
                vi = ar[n] * xi[n] + ai[n] * xr[n] + x_sc[ng + k, r, :]
                x_sc[k, r, :] = vr
                x_sc[ng + k, r, :] = vi
                nr.append(vr)
                ni.append(vi)
            return tuple(nr), tuple(ni)

        init = (tuple(st_sc[:, k * LANES:(k + 1) * LANES] for k in ks),
                tuple(st_sc[:, S5_X + k * LANES:S5_X + (k + 1) * LANES] for k in ks))
        xr, xi = lax.fori_loop(0, steps, body, init, unroll=8)
        for n, k in enumerate(ks):
            st_sc[:, k * LANES:(k + 1) * LANES] = xr[n]
            st_sc[:, S5_X + k * LANES:S5_X + (k + 1) * LANES] = xi[n]

    for b in range(nb):
        xs = jnp.concatenate([x_sc[k, b * stride:b * stride + steps, :] for k in range(2 * ng)], axis=-1)
        y = _dot(xs.astype(BF16), ccat_ref[...]) + d_ref[...] * u_ref[b]
        y = _gelu(y)
        o_ref[b] = (y * _sigmoid(_dot(y.astype(BF16), wglu_ref[...]))).astype(BF16)


def _s5(u, bcat, ccat, a_cat, d, wglu, steps):
    nb, L, _ = u.shape
    return pl.pallas_call(
        functools.partial(_s5_kernel, steps=steps, nb=nb, nch=2),
        grid=(L // steps,),
        in_specs=[pl.BlockSpec((nb, steps, S5_WIDTH), lambda i: (0, i, 0)), _const_spec(bcat.shape),
                  _const_spec(ccat.shape), _const_spec(a_cat.shape), _const_spec(d.shape),
                  _const_spec(wglu.shape)],
        out_specs=pl.BlockSpec((nb, steps, S5_WIDTH), lambda i: (0, i, 0)),
        out_shape=jax.ShapeDtypeStruct((nb, L, S5_WIDTH), BF16),
        scratch_shapes=[pltpu.VMEM((2 * S5_X // LANES, nb * (steps + S5_PAD), LANES), F32),
                        pltpu.VMEM((nb, 2 * S5_X), F32)],
        compiler_params=_params("arbitrary"),
        name="s5",
    )(u, bcat, ccat, a_cat, d, wglu)


SIGN_BIT = np.int32(-2 ** 31)
SB_STOP_LOG2 = 170.0


def _sb_softplus2(z):
    neg_abs = lax.bitcast_convert_type(lax.bitcast_convert_type(z, jnp.int32) | SIGN_BIT, F32)
    return jnp.maximum(z, 0.0) + jnp.log2(1.0 + jnp.exp2(neg_abs))


def _sb_suffix(spb, tri_ref):
    half = tri_ref.shape[1]
    return jnp.concatenate([_dot(spb, tri_ref[...]), _dot(spb[:, half:], tri_ref[0:half, :])], axis=-1)


def _sb_kernel(q_ref, kv_ref, tri1_ref, tri_ref, o_ref, acc_sc, c_sc, z_sc, w_sc, *, tq, tk1, tk, nt):
    g = pl.program_id(2)

    def kv_block(j):
        return kv_ref[0, pl.ds(pl.multiple_of(j * tk, tk), tk), :]

    def first_step(h):
        i = g * nt + h
        q = q_ref[0, h * tq:(h + 1) * tq, :]
        start = jnp.maximum((i + 1) * tq - tk1, 0)
        kv1 = kv_ref[0, pl.ds(pl.multiple_of(start, tk), tk1), :]
        z = _dot_nt(q, kv1)
        t = i * tq + lax.broadcasted_iota(jnp.int32, (tq, tk1), 0)
        causal = (start + lax.broadcasted_iota(jnp.int32, (tq, tk1), 1)) < t
        spb = jnp.where(causal, _sb_softplus2(z), 0.0).astype(BF16)
        r = _sb_suffix(spb, tri1_ref)
        w = jnp.where(causal, jnp.exp2(jnp.minimum(z + r, 0.0)), 0.0)
        acc_sc[h] = _dot(w.astype(BF16), kv1)
        c_sc[h] = jnp.broadcast_to(r[:, 0:1], (tq, LANES))
        j0 = start // tk - 1
        z_sc[h] = _dot_nt(q, kv_block(jnp.maximum(j0, 0)))
        w_sc[h] = jnp.zeros((tq, tk), BF16)
        return j0

    j0s = [first_step(h) for h in range(nt)]
    cmaxs = [jnp.max(c_sc[h]) for h in range(nt)]

    def cond(carry):
        j, cmax = carry
        return (j >= 0) & (cmax > -SB_STOP_LOG2)

    lane = lax.broadcasted_iota(jnp.int32, (1, LANES), 1)
    for h in range(nt):
        q = q_ref[0, h * tq:(h + 1) * tq, :]

        def body(carry, h=h, q=q):
            j, _ = carry
            z = z_sc[h]
            z_next = _dot_nt(q, kv_block(jnp.maximum(j - 1, 0)))
            acc_sc[h] += _dot(w_sc[h], kv_block(j + 1))
            r = _sb_suffix(_sb_softplus2(z).astype(BF16), tri_ref)
            c = c_sc[h]
            cw = jnp.concatenate([c] * (tk // LANES), axis=-1)
            w_sc[h] = jnp.exp2(jnp.minimum(z + r + cw, 0.0)).astype(BF16)
            z_sc[h] = z_next
            c_sc[h] = c + r[:, 0:1]
            return j - 1, jnp.max(c_sc[h])

        j_next, _ = lax.while_loop(cond, body, (j0s[h], cmaxs[h]))

        @pl.when(j_next < j0s[h])
        def _(h=h, j_next=j_next):
            acc_sc[h] += _dot(w_sc[h], kv_block(j_next + 1))

        o_ref[0, h * tq:(h + 1) * tq, :] = jnp.where(lane >= HEAD_DIM, acc_sc[h], 0.0).astype(BF16)


def _sb(sbq, sbkv, tri1, tri, tq, tk1, tk, nt):
    B, L, _ = sbq.shape
    assert tq % tk == 0 and tk1 % tk == 0 and L >= tk1
    return pl.pallas_call(
        functools.partial(_sb_kernel, tq=tq, tk1=tk1, tk=tk, nt=nt),
        grid=(B, SB_HEADS, L // (nt * tq)),
        in_specs=[pl.BlockSpec((1, nt * tq, LANES), lambda b, h, i: (b, i, h)),
                  pl.BlockSpec((1, L, LANES), lambda b, h, i: (b, 0, h)),
                  _const_spec(tri1.shape), _const_spec(tri.shape)],
        out_specs=pl.BlockSpec((1, nt * tq, LANES), lambda b, h, i: (b, i, h)),
        out_shape=jax.ShapeDtypeStruct((B, L, SB_HEADS * LANES), BF16),
        scratch_shapes=[pltpu.VMEM((nt, tq, LANES), F32), pltpu.VMEM((nt, tq, LANES), F32),
                        pltpu.VMEM((nt, tq, tk), F32), pltpu.VMEM((nt, tq, tk), BF16)],
        compiler_params=_params("parallel", "parallel", "arbitrary"),
        name="sb_attn",
    )(sbq, sbkv, tri1, tri)


FFN_CHUNK = 1024
HALO = 8


def _merge_ffn_kernel(x_ref, oa_ref, ob_ref, oc_ref, mg_ref, wa_ref, wb_ref, wc_ref, wout_ref, gain_ref,
                      wup_ref, cw_ref, cb_ref, wdown_ref, out_ref, ubuf, carry, *, tl, dff):
    @pl.when(pl.program_id(1) == 0)
    def _():
        carry[...] = jnp.zeros_like(carry)

    d = x_ref.shape[2]
    merged = (mg_ref[0, :, 0:d].astype(F32) * _dot(oa_ref[0], wa_ref[...])
              + mg_ref[0, :, d:2 * d].astype(F32) * _dot(ob_ref[0], wb_ref[...])
              + mg_ref[0, :, 2 * d:3 * d].astype(F32) * _dot(oc_ref[0], wc_ref[...]))
    x1 = x_ref[0] + _dot(merged.astype(BF16), wout_ref[...])
    h = x1 * lax.rsqrt(jnp.mean(x1 * x1, axis=-1, keepdims=True) + NORM_EPS) * gain_ref[...]
    hb = h.astype(BF16)

    def conv_cols(off):
        cols = slice(off, off + FFN_CHUNK)
        u = _dot(hb, wup_ref[:, cols])
        ubuf[0:HALO, :] = carry[:, cols]
        ubuf[HALO:HALO + tl, :] = u
        carry[:, cols] = u[tl - HALO:tl, :]
        return (cw_ref[2:3, cols] * u + cw_ref[1:2, cols] * ubuf[HALO - 1:HALO - 1 + tl, :]
                + cw_ref[0:1, cols] * ubuf[HALO - 2:HALO - 2 + tl, :] + cb_ref[:, cols])

    acc = x1
    for j in range(dff // FFN_CHUNK):
        gate = conv_cols(j * FFN_CHUNK)
        val = conv_cols(dff + j * FFN_CHUNK)
        act = (_gelu(gate) * val).astype(BF16)
        acc = acc + _dot(act, wdown_ref[j * FFN_CHUNK:(j + 1) * FFN_CHUNK, :])
    out_ref[0] = acc


def _merge_ffn(x, oa, ob, oc, mg, wa, wb, wc, wout, gain, wup, cw, cb, wdown, tl):
    B, L, D = x.shape
    dff = wdown.shape[0]
    row = lambda b, l: (b, l, 0)
    weights = [wa, wb, wc, wout, gain, wup, cw, cb, wdown]
    return pl.pallas_call(
        functools.partial(_merge_ffn_kernel, tl=tl, dff=dff),
        grid=(B, L // tl),
        in_specs=[pl.BlockSpec((1, tl, D), row), pl.BlockSpec((1, tl, oa.shape[2]), row),
                  pl.BlockSpec((1, tl, ob.shape[2]), row), pl.BlockSpec((1, tl, oc.shape[2]), row),
                  pl.BlockSpec((1, tl, mg.shape[2]), row)] + [_const_spec(w.shape) for w in weights],
        out_specs=pl.BlockSpec((1, tl, D), row),
        out_shape=jax.ShapeDtypeStruct((B, L, D), F32),
        scratch_shapes=[pltpu.VMEM((HALO + tl, FFN_CHUNK), F32), pltpu.VMEM((HALO, 2 * dff), F32)],
        compiler_params=_params("parallel", "arbitrary"),
        name="merge_ffn",
    )(x, oa, ob, oc, mg, *weights)


def _pad_heads(w, n_heads):
    d = w.shape[0]
    w = w.reshape(d, n_heads, HEAD_DIM)
    return jnp.concatenate([w, jnp.zeros_like(w)], axis=-1).reshape(d, n_heads * LANES)


def _pack_w_in(w):
    d = w.shape[0]
    o_g = 640
    o_s5 = o_g + NSA_HEADS * 3
    o_sb = o_s5 + S5_WIDTH
    o_mg = o_sb + 3 * SB_HEADS * HEAD_DIM
    hw = SB_HEADS * HEAD_DIM
    gates = jnp.pad(w[:, o_g:o_s5], ((0, 0), (0, LANES - NSA_HEADS * 3)))
    sbk = w[:, o_sb + hw:o_sb + 2 * hw].reshape(d, SB_HEADS, 1, HEAD_DIM)
    sbv = w[:, o_sb + 2 * hw:o_mg].reshape(d, SB_HEADS, 1, HEAD_DIM)
    sbkv = jnp.concatenate([sbk, sbv], axis=2).reshape(d, SB_HEADS * LANES)
    packed = jnp.concatenate([_pad_heads(w[:, 0:256], NSA_HEADS), w[:, 256:o_g], gates, w[:, o_s5:o_sb],
                              _pad_heads(w[:, o_sb:o_sb + hw], SB_HEADS), sbkv, w[:, o_mg:]], axis=1)
    return packed.astype(BF16)


def _pad_rows(w, n_heads):
    d = w.shape[1]
    w = w.reshape(n_heads, HEAD_DIM, d)
    return jnp.concatenate([jnp.zeros_like(w), w], axis=1).reshape(n_heads * LANES, d).astype(BF16)


def _pad_gain(g, fill):
    return jnp.concatenate([g, jnp.full_like(g, fill)], axis=-1)


def _cmp_weights(wk, wv):
    z = jnp.zeros_like(wk)
    wkv = jnp.concatenate([jnp.concatenate([wk, z], axis=2), jnp.concatenate([z, wv], axis=2)], axis=1)
    half = CMP_BLOCK // 2
    wt = wkv[:half].reshape(half * LANES, LANES)
    wb = wkv[half:].reshape(half * LANES, LANES)
    return wt.astype(BF16), wb.astype(BF16)


def _s5_params(a_re, a_im, log_dt, b_re, b_im, c_re, c_im):
    dt = jnp.exp(log_dt.astype(F32))[:, None]
    A = lax.complex(a_re.astype(F32), a_im.astype(F32))
    A_bar = jnp.exp(dt * A)
    B_bar = ((A_bar - 1.0) / A)[..., None] * lax.complex(b_re.astype(F32), b_im.astype(F32))
    eye = jnp.eye(S5_GROUPS, dtype=F32)

    def bdiag_in(m):
        return jnp.einsum('gpc,gh->gchp', m, eye).reshape(S5_WIDTH, S5_X)

    def bdiag_out(m):
        return jnp.einsum('gcp,gh->gphc', m, eye).reshape(S5_X, S5_WIDTH)

    bcat = jnp.concatenate([bdiag_in(jnp.real(B_bar)), bdiag_in(jnp.imag(B_bar))], axis=1).astype(BF16)
    ccat = jnp.concatenate([bdiag_out(c_re.astype(F32)), -bdiag_out(c_im.astype(F32))], axis=0).astype(BF16)
    a_cat = jnp.concatenate([jnp.real(A_bar).reshape(1, S5_X), jnp.imag(A_bar).reshape(1, S5_X)], axis=1)
    return bcat, ccat, a_cat


def _overlap_matrix(n_rows, n_cmp, n_sel):
    cs = np.arange(n_rows)[:, None] * CMP_STRIDE
    ss = np.arange(n_sel)[None, :] * SEL_BLOCK
    ov = np.clip(np.minimum(cs + CMP_BLOCK, ss + SEL_BLOCK) - np.maximum(cs, ss), 0, None) / CMP_BLOCK
    ov[n_cmp:] = 0.0
    return jnp.asarray(ov, dtype=BF16)


def _sb_suffix_matrix(tk):
    u = (np.arange(tk)[:, None] >= np.arange(tk)[None, :]).astype(np.float32)
    return jnp.asarray(-np.concatenate([u, np.ones_like(u)], axis=0), dtype=BF16)


def kernel(x, norm_mix, w_in, nsa_q_gain, nsa_k_gain, cmp_pe, cmp_wk, cmp_wv, s5_a_re, s5_a_im, s5_log_dt,
           s5_b_re, s5_b_im, s5_c_re, s5_c_im, s5_d, s5_w_glu, w_br_nsa, w_br_s5, w_br_sb, w_out, norm_ffn,
           w_up, conv_w, conv_b, w_down):
    B, L, D = x.shape
    depth = w_in.shape[0]
    tq = 128
    tq_cmp = 512
    tl_in = min(512, L)
    tl_ffn = min(512, L)
    sb_tq, sb_tk1, sb_tk, sb_tiles = 256, 512, 256, 4
    s5_steps = min(256, L)
    assert L % tl_in == 0 and L % (sb_tq * sb_tiles) == 0 and L >= WINDOW + tq and D == D_MODEL
    n_cmp = (L - CMP_BLOCK) // CMP_STRIDE + 1
    n_rows = L // CMP_STRIDE
    n_sel = L // SEL_BLOCK
    nq = L // tq
    nwords = (nq + 31) // 32
    ov = _overlap_matrix(n_rows, n_cmp, n_sel)
    tri1, tri = _sb_suffix_matrix(sb_tk1 // 2), _sb_suffix_matrix(sb_tk // 2)
    wbias = _window_bias(tq)

    def pack_layer(p):
        wt, wb = _cmp_weights(p['cmp_wk'][0], p['cmp_wv'][0])
        bcat, ccat, a_cat = _s5_params(p['s5_a_re'], p['s5_a_im'], p['s5_log_dt'], p['s5_b_re'], p['s5_b_im'],
                                       p['s5_c_re'], p['s5_c_im'])
        return dict(
            norm_mix=p['norm_mix'][None, :], w_in=_pack_w_in(p['w_in']),
            qg=_pad_gain(p['nsa_q_gain'][None, :], 0.0), kg=_pad_gain(p['nsa_k_gain'], 1.0),
            pe=jnp.concatenate([p['cmp_pe'], p['cmp_pe']], axis=-1).reshape(2, (CMP_BLOCK // 2) * LANES),
            wt=wt, wb=wb, bcat=bcat, ccat=ccat, a_cat=a_cat, d=p['s5_d'].reshape(1, S5_WIDTH),
            wglu=p['s5_w_glu'].astype(BF16), wa=_pad_rows(p['w_br_nsa'], NSA_HEADS),
            wb5=p['w_br_s5'].astype(BF16), wc=_pad_rows(p['w_br_sb'], SB_HEADS), wout=p['w_out'].astype(BF16),
            norm_ffn=p['norm_ffn'][None, :], wup=p['w_up'].astype(BF16), cw=p['conv_w'],
            cb=p['conv_b'][None, :], wdown=p['w_down'].astype(BF16))

    packed = jax.vmap(pack_layer)(dict(
        norm_mix=norm_mix, w_in=w_in, nsa_q_gain=nsa_q_gain, nsa_k_gain=nsa_k_gain, cmp_pe=cmp_pe, cmp_wk=cmp_wk,
        cmp_wv=cmp_wv, s5_a_re=s5_a_re, s5_a_im=s5_a_im, s5_log_dt=s5_log_dt, s5_b_re=s5_b_re, s5_b_im=s5_b_im,
        s5_c_re=s5_c_re, s5_c_im=s5_c_im, s5_d=s5_d, s5_w_glu=s5_w_glu, w_br_nsa=w_br_nsa, w_br_s5=w_br_s5,
        w_br_sb=w_br_sb, w_out=w_out, norm_ffn=norm_ffn, w_up=w_up, conv_w=conv_w, conv_b=conv_b, w_down=w_down))

    for i in range(depth):
        p = {k: v[i] for k, v in packed.items()}
        qn, ckv, skv, wkv, gates, u_s5, sbq, sbkv, mg = _inproj(
            x, p['norm_mix'], p['w_in'], p['qg'], p['kg'][1:3], tl_in)

        kvc = _cmpkv(ckv.reshape(B, n_rows, CMP_STRIDE * LANES), p['pe'], p['wt'], p['wb'], p['kg'][0:1])
        ocmp, idx, flags = _nsa_cmp(qn, kvc, gates, ov, tq_cmp, tq, n_cmp)
        fl = jnp.pad(flags[:, :, 0, :nq], ((0, 0), (0, 0), (0, nwords * 32 - nq))).reshape(B, nq, nwords, 32)
        bits = jnp.sum(fl << jnp.arange(32, dtype=jnp.int32), axis=-1, dtype=jnp.int32).reshape(-1)
        o_a = _nsa_sw(bits, qn, skv, wkv, idx, gates, ocmp, wbias, tq, nwords, 2)

        o_b = _s5(u_s5, p['bcat'], p['ccat'], p['a_cat'], p['d'], p['wglu'], s5_steps)

        o_c = _sb(sbq, sbkv, tri1, tri, sb_tq, sb_tk1, sb_tk, sb_tiles)

        x = _merge_ffn(x, o_a, o_b, o_c, mg, p['wa'], p['wb5'], p['wc'], p['wout'], p['norm_ffn'],
                       p['wup'], p['cw'], p['cb'], p['wdown'], tl_ffn)
    return x
```
